```python
import math
import jax
import jax.numpy as jnp
from jax import lax
import numpy as np

D_MODEL = 4096
BATCH = 4
SEQ = 4096
DEPTH = 2

N_EVEN = (DEPTH + 1) // 2
N_ODD = DEPTH // 2
EPS = 1e-6

LRU_WIDTH = 5 * D_MODEL // 8
LRU_HEADS = 10
LRU_HEAD_DIM = LRU_WIDTH // LRU_HEADS
LRU_CONV = 4
LRU_C = 8.0
S5_WIDTH = D_MODEL - LRU_WIDTH
S5_GROUP = 16
S5_GROUPS = S5_WIDTH // S5_GROUP
S5_STATE = 64
AB_IN = 2 * LRU_WIDTH + S5_WIDTH
AB_OUT = LRU_WIDTH + S5_WIDTH

SG_WIDTH = D_MODEL // 2
SG_CHUNK = 128
SG_HEADS = 16
SG_HEAD_DIM = SG_WIDTH // SG_HEADS
DA_WIDTH = D_MODEL - SG_WIDTH
DA_HEADS = 8
DA_VDIM = DA_WIDTH // DA_HEADS
DA_QK = DA_VDIM // 2
Q_BLOCK = 128
REL_BUCKETS = 32
REL_MAX_DIST = 128
CD_IN = 2 * SG_WIDTH + 3 * DA_WIDTH
CD_OUT = SG_WIDTH + DA_WIDTH

MOE_GROUPS = 4
MOE_PER_GROUP = 8
MOE_EXPERTS = MOE_GROUPS * MOE_PER_GROUP
MOE_TOPK = 2
MOE_HIDDEN = D_MODEL // 8

NEG_INF = -1e30

kernel_name = "hybrid_rglru_s5_sgu_diffattn_hmoe"

F32 = jnp.float32


def rms_norm(x, g):
    x32 = x.astype(F32)
    y = x32 * lax.rsqrt(jnp.mean(x32 * x32, axis=-1, keepdims=True) + EPS)
    return (y * g.astype(F32)).astype(x.dtype)


def layer_norm(x, g, b):
    x32 = x.astype(F32)
    mu = jnp.mean(x32, axis=-1, keepdims=True)
    xc = x32 - mu
    var = jnp.mean(xc * xc, axis=-1, keepdims=True)
    return (xc * lax.rsqrt(var + EPS) * g.astype(F32) + b.astype(F32)).astype(x.dtype)


def causal_depthwise_conv(x, w, b):
    ch = x.shape[-1]
    y = lax.conv_general_dilated(
        x, w[:, None, :].astype(x.dtype), window_strides=(1,),
        padding=[(w.shape[0] - 1, 0)], dimension_numbers=("NWC", "WIO", "NWC"),
        feature_group_count=ch)
    return y + b.astype(x.dtype)


def _linear_combine(left, right):
    a1, b1 = left
    a2, b2 = right
    return a1 * a2, a2 * b1 + b2


def _complex_combine(left, right):
    ar1, ai1, br1, bi1 = left
    ar2, ai2, br2, bi2 = right
    return (ar1 * ar2 - ai1 * ai2, ar1 * ai2 + ai1 * ar2,
            ar2 * br1 - ai2 * bi1 + br2, ar2 * bi1 + ai2 * br1 + bi2)


def rg_lru(xb, conv_w, conv_b, wa, ba, wx, bx, lam):
    bsz, s, _ = xb.shape
    xc = causal_depthwise_conv(xb, conv_w, conv_b)
    xh = xc.reshape(bsz, s, LRU_HEADS, LRU_HEAD_DIM)
    r = jax.nn.sigmoid((jnp.einsum('bshi,hij->bshj', xh, wa).reshape(bsz, s, LRU_WIDTH) + ba).astype(F32))
    i = jax.nn.sigmoid((jnp.einsum('bshi,hij->bshj', xh, wx).reshape(bsz, s, LRU_WIDTH) + bx).astype(F32))
    log_a = -LRU_C * r * jax.nn.softplus(-lam.astype(F32))
    a = jnp.exp(log_a)
    b_in = jnp.sqrt(-jnp.expm1(2.0 * log_a)) * (i * xc.astype(F32))
    _, h = lax.associative_scan(_linear_combine, (a, b_in), axis=1)
    return h.astype(xb.dtype)


def s5_glu(u, lam_re, lam_im, log_dt, b_re, b_im, c_re, c_im, d, glu_w, glu_b):
    bsz, s, _ = u.shape
    ug = u.reshape(bsz, s, S5_GROUPS, S5_GROUP).astype(F32)
    dt = jnp.exp(log_dt.astype(F32))[:, None]
    lr = lam_re.astype(F32)
    li = lam_im.astype(F32)
    mag = jnp.exp(lr * dt)
    ab_re = mag * jnp.cos(li * dt)
    ab_im = mag * jnp.sin(li * dt)
    den = lr * lr + li * li
    zr = ab_re - 1.0
    coef_re = (zr * lr + ab_im * li) / den
    coef_im = (ab_im * lr - zr * li) / den
    br = b_re.astype(F32)
    bi = b_im.astype(F32)
    bb_re = coef_re[..., None] * br - coef_im[..., None] * bi
    bb_im = coef_re[..., None] * bi + coef_im[..., None] * br
    bu_re = jnp.einsum('bsgc,gnc->bsgn', ug, bb_re)
    bu_im = jnp.einsum('bsgc,gnc->bsgn', ug, bb_im)
    a_re = jnp.broadcast_to(ab_re, (1, s, S5_GROUPS, S5_STATE))
    a_im = jnp.broadcast_to(ab_im, (1, s, S5_GROUPS, S5_STATE))
    _, _, h_re, h_im = lax.associative_scan(_complex_combine, (a_re, a_im, bu_re, bu_im), axis=1)
    y = (jnp.einsum('gcn,bsgn->bsgc', c_re.astype(F32), h_re)
         - jnp.einsum('gcn,bsgn->bsgc', c_im.astype(F32), h_im)
         + d.astype(F32).reshape(S5_GROUPS, S5_GROUP) * ug)
    y = y.reshape(bsz, s, S5_WIDTH)
    g = jax.nn.gelu(y)
    out = g * jax.nn.sigmoid(g @ glu_w.astype(F32) + glu_b.astype(F32))
    return out.astype(u.dtype)


def mixer_rglru_s5(h, w_in, w_out, conv_w, conv_b, wa, ba, wx, bx, lam,
                   lam_re, lam_im, log_dt, b_re, b_im, c_re, c_im, d, glu_w, glu_b):
    z = h @ w_in
    x_lru, gate_lru, u_s5 = jnp.split(z, [LRU_WIDTH, 2 * LRU_WIDTH], axis=-1)
    y_a = jax.nn.gelu(gate_lru) * rg_lru(x_lru, conv_w, conv_b, wa, ba, wx, bx, lam)
    y_b = s5_glu(u_s5, lam_re, lam_im, log_dt, b_re, b_im, c_re, c_im, d, glu_w, glu_b)
    return jnp.concatenate([y_a, y_b], axis=-1) @ w_out


def chunked_spatial_gating(z, ln_g, ln_b, w_s, b_s):
    bsz, s, _ = z.shape
    z = jax.nn.gelu(z)
    u, v = jnp.split(z, 2, axis=-1)
    v = layer_norm(v, ln_g, ln_b)
    n_chunks = s // SG_CHUNK
    vc = v.reshape(bsz, n_chunks, SG_CHUNK, SG_HEADS, SG_HEAD_DIM)
    mask = jnp.tril(jnp.ones((SG_CHUNK, SG_CHUNK), dtype=bool))
    w = jnp.where(mask, w_s, 0).astype(v.dtype)
    g = jnp.einsum('hts,bnshc->bnthc', w, vc) + b_s.T.astype(v.dtype)[None, None, :, :, None]
    return u * g.reshape(bsz, s, SG_WIDTH)


def t5_bucket(n):
    max_exact = REL_BUCKETS // 2
    nf = jnp.maximum(n, 1).astype(F32)
    large = max_exact + (jnp.log(nf / max_exact) / math.log(REL_MAX_DIST / max_exact)
                         * (REL_BUCKETS - max_exact)).astype(jnp.int32)
    large = jnp.minimum(large, REL_BUCKETS - 1)
    return jnp.where(n < max_exact, n, large)


def diff_attention(q, k, v, q_g, k_g, lq1, lk1, lq2, lk2, sub_g, rel_bias, layer):
    bsz, s = q.shape[:2]
    qf = rms_norm(q, q_g).astype(F32) * (DA_QK ** -0.5)
    kf = rms_norm(k, k_g).astype(F32)
    vf = v.astype(F32)
    lam_init = 0.8 - 0.6 * math.exp(-0.3 * layer)
    lam = (jnp.exp(jnp.sum(lq1.astype(F32) * lk1.astype(F32)))
           - jnp.exp(jnp.sum(lq2.astype(F32) * lk2.astype(F32))) + lam_init)
    n_blocks = s // Q_BLOCK
    qb = qf.reshape(bsz, n_blocks, Q_BLOCK, DA_HEADS, 2, DA_QK).transpose(1, 0, 2, 3, 4, 5)
    kpos = jnp.arange(s)
    table = rel_bias.astype(F32)

    def block(args):
        qblk, idx = args
        qpos = idx * Q_BLOCK + jnp.arange(Q_BLOCK)
        dist = qpos[:, None] - kpos[None, :]
        bias = jnp.transpose(table[t5_bucket(jnp.maximum(dist, 0))], (2, 0, 1))
        logits = jnp.einsum('bqhmd,bkhmd->bmhqk', qblk, kf) + bias[None, None]
        logits = jnp.where((dist >= 0)[None, None, None], logits, NEG_INF)
        p = jax.nn.softmax(logits, axis=-1)
        pd = p[:, 0] - lam * p[:, 1]
        return jnp.einsum('bhqk,bkhd->bqhd', pd, vf)

    outs = lax.map(block, (qb, jnp.arange(n_blocks)))
    o = outs.transpose(1, 0, 2, 3, 4).reshape(bsz, s, DA_HEADS, DA_VDIM)
    o = rms_norm(o, sub_g) * (1.0 - lam_init)
    return o.reshape(bsz, s, DA_WIDTH).astype(v.dtype)


def mixer_sgu_diffattn(h, w_in, w_out, ln_g, ln_b, w_s, b_s, q_g, k_g, lq1, lk1, lq2, lk2, sub_g,
                       rel_bias, layer):
    bsz, s, _ = h.shape
    z = h @ w_in
    z_sg, q, k, v = jnp.split(z, [2 * SG_WIDTH, 2 * SG_WIDTH + DA_WIDTH, 2 * SG_WIDTH + 2 * DA_WIDTH], axis=-1)
    y_c = chunked_spatial_gating(z_sg, ln_g, ln_b, w_s, b_s)
    q = q.reshape(bsz, s, DA_HEADS, 2, DA_QK)
    k = k.reshape(bsz, s, DA_HEADS, 2, DA_QK)
    v = v.reshape(bsz, s, DA_HEADS, DA_VDIM)
    y_d = diff_attention(q, k, v, q_g, k_g, lq1, lk1, lq2, lk2, sub_g, rel_bias, layer)
    return jnp.concatenate([y_c, y_d], axis=-1) @ w_out


def hier_moe(h, wg, bg, we, be, w_gate, w_up, w_down):
    bsz, s, d = h.shape
    xt = h.reshape(-1, d)
    gprob = jax.nn.softmax((xt @ wg).astype(F32) + bg.astype(F32), axis=-1)
    gp, gidx = lax.top_k(gprob, 1)
    elog = jnp.einsum('td,gde->tge', xt, we).astype(F32) + be.astype(F32)
    elog = jnp.take_along_axis(elog, gidx[:, :, None], axis=1)[:, 0]
    ep, eidx = lax.top_k(jax.nn.softmax(elog, axis=-1), MOE_TOPK)
    ep = ep / jnp.sum(ep, axis=-1, keepdims=True)
    wts = gp * ep
    expert = gidx * MOE_PER_GROUP + eidx
    comb = jnp.sum(jax.nn.one_hot(expert, MOE_EXPERTS, dtype=F32) * wts[..., None], axis=1)
    hg = jnp.einsum('td,edf->tef', xt, w_gate)
    hu = jnp.einsum('td,edf->tef', xt, w_up)
    act = jax.nn.silu(hg) * hu * comb[:, :, None].astype(xt.dtype)
    y = jnp.einsum('tef,efd->td', act, w_down)
    return y.reshape(bsz, s, d)


def setup_inputs(seed: int = 0) -> dict:
    key = jax.random.key(seed)
    ks = iter(jax.random.split(key, 64))
    D = D_MODEL

    def nrm(shape, scale):
        return jax.random.normal(next(ks), shape, F32) * scale

    def gain(shape):
        return 1.0 + nrm(shape, 0.01)

    x = nrm((BATCH, SEQ, D), 1.0)
    c = nrm((BATCH, D), 1.0)
    norm1_g = gain((DEPTH, D))
    norm2_g = gain((DEPTH, D))
    ada_w = nrm((DEPTH, D, 6 * D), 0.3 * D ** -0.5)
    ada_b = nrm((DEPTH, 6 * D), 0.01)

    ab_w_in = nrm((N_EVEN, D, AB_IN), D ** -0.5)
    ab_w_out = nrm((N_EVEN, AB_OUT, D), AB_OUT ** -0.5)
    lru_conv_w = nrm((N_EVEN, LRU_CONV, LRU_WIDTH), LRU_CONV ** -0.5)
    lru_conv_b = nrm((N_EVEN, LRU_WIDTH), 0.01)
    lru_wa = nrm((N_EVEN, LRU_HEADS, LRU_HEAD_DIM, LRU_HEAD_DIM), LRU_HEAD_DIM ** -0.5)
    lru_ba = nrm((N_EVEN, LRU_WIDTH), 0.01)
    lru_wx = nrm((N_EVEN, LRU_HEADS, LRU_HEAD_DIM, LRU_HEAD_DIM), LRU_HEAD_DIM ** -0.5)
    lru_bx = nrm((N_EVEN, LRU_WIDTH), 0.01)
    a_pow_c = jax.random.uniform(next(ks), (N_EVEN, LRU_WIDTH), F32, 0.9, 0.999)
    a0 = a_pow_c ** (1.0 / LRU_C)
    lru_lambda = jnp.log(a0) - jnp.log1p(-a0)

    n_idx = jnp.arange(S5_STATE, dtype=F32)
    s5_lambda_re = -0.5 + nrm((N_EVEN, S5_GROUPS, S5_STATE), 0.01)
    s5_lambda_im = math.pi * n_idx + nrm((N_EVEN, S5_GROUPS, S5_STATE), 0.01)
    s5_log_dt = jax.random.uniform(next(ks), (N_EVEN, S5_GROUPS), F32, math.log(1e-3), math.log(1e-1))
    s5_b_re = nrm((N_EVEN, S5_GROUPS, S5_STATE, S5_GROUP), (2 * S5_GROUP) ** -0.5)
    s5_b_im = nrm((N_EVEN, S5_GROUPS, S5_STATE, S5_GROUP), (2 * S5_GROUP) ** -0.5)
    s5_c_re = nrm((N_EVEN, S5_GROUPS, S5_GROUP, S5_STATE), S5_STATE ** -0.5)
    s5_c_im = nrm((N_EVEN, S5_GROUPS, S5_GROUP, S5_STATE), S5_STATE ** -0.5)
    s5_d = nrm((N_EVEN, S5_WIDTH), 1.0)
    s5_glu_w = nrm((N_EVEN, S5_WIDTH, S5_WIDTH), S5_WIDTH ** -0.5)
    s5_glu_b = nrm((N_EVEN, S5_WIDTH), 0.01)

    cd_w_in = nrm((N_ODD, D, CD_IN), D ** -0.5)
    cd_w_out = nrm((N_ODD, CD_OUT, D), CD_OUT ** -0.5)
    sg_ln_g = gain((N_ODD, SG_WIDTH))
    sg_ln_b = nrm((N_ODD, SG_WIDTH), 0.01)
    sg_w = nrm((N_ODD, SG_HEADS, SG_CHUNK, SG_CHUNK), 0.05)
    sg_b = 1.0 + nrm((N_ODD, SG_HEADS, SG_CHUNK), 0.01)
    da_q_norm = gain((N_ODD, DA_QK))
    da_k_norm = gain((N_ODD, DA_QK))
    da_lq1 = nrm((N_ODD, DA_QK), 0.1)
    da_lk1 = nrm((N_ODD, DA_QK), 0.1)
    da_lq2 = nrm((N_ODD, DA_QK), 0.1)
    da_lk2 = nrm((N_ODD, DA_QK), 0.1)
    da_sub_g = gain((N_ODD, DA_VDIM))
    rel_bias = nrm((REL_BUCKETS, DA_HEADS), 0.5)

    moe_wg = nrm((DEPTH, D, MOE_GROUPS), D ** -0.5)
    moe_bg = nrm((DEPTH, MOE_GROUPS), 0.01)
    moe_we = nrm((DEPTH, MOE_GROUPS, D, MOE_PER_GROUP), D ** -0.5)
    moe_be = nrm((DEPTH, MOE_GROUPS, MOE_PER_GROUP), 0.01)
    moe_w_gate = nrm((DEPTH, MOE_EXPERTS, D, MOE_HIDDEN), D ** -0.5)
    moe_w_up = nrm((DEPTH, MOE_EXPERTS, D, MOE_HIDDEN), D ** -0.5)
    moe_w_down = nrm((DEPTH, MOE_EXPERTS, MOE_HIDDEN, D), MOE_HIDDEN ** -0.5)

    return {"x": x, "c": c, "norm1_g": norm1_g, "norm2_g": norm2_g, "ada_w": ada_w, "ada_b": ada_b,
            "ab_w_in": ab_w_in, "ab_w_out": ab_w_out, "lru_conv_w": lru_conv_w, "lru_conv_b": lru_conv_b,
            "lru_wa": lru_wa, "lru_ba": lru_ba, "lru_wx": lru_wx, "lru_bx": lru_bx, "lru_lambda": lru_lambda,
            "s5_lambda_re": s5_lambda_re, "s5_lambda_im": s5_lambda_im, "s5_log_dt": s5_log_dt,
            "s5_b_re": s5_b_re, "s5_b_im": s5_b_im, "s5_c_re": s5_c_re, "s5_c_im": s5_c_im, "s5_d": s5_d,
            "s5_glu_w": s5_glu_w, "s5_glu_b": s5_glu_b,
            "cd_w_in": cd_w_in, "cd_w_out": cd_w_out, "sg_ln_g": sg_ln_g, "sg_ln_b": sg_ln_b,
            "sg_w": sg_w, "sg_b": sg_b, "da_q_norm": da_q_norm, "da_k_norm": da_k_norm,
            "da_lq1": da_lq1, "da_lk1": da_lk1, "da_lq2": da_lq2, "da_lk2": da_lk2, "da_sub_g": da_sub_g,
            "rel_bias": rel_bias,
            "moe_wg": moe_wg, "moe_bg": moe_bg, "moe_we": moe_we, "moe_be": moe_be,
            "moe_w_gate": moe_w_gate, "moe_w_up": moe_w_up, "moe_w_down": moe_w_down}


def reference(x, c, norm1_g, norm2_g, ada_w, ada_b,
              ab_w_in, ab_w_out, lru_conv_w, lru_conv_b, lru_wa, lru_ba, lru_wx, lru_bx, lru_lambda,
              s5_lambda_re, s5_lambda_im, s5_log_dt, s5_b_re, s5_b_im, s5_c_re, s5_c_im, s5_d,
              s5_glu_w, s5_glu_b,
              cd_w_in, cd_w_out, sg_ln_g, sg_ln_b, sg_w, sg_b,
              da_q_norm, da_k_norm, da_lq1, da_lk1, da_lq2, da_lk2, da_sub_g, rel_bias,
              moe_wg, moe_bg, moe_we, moe_be, moe_w_gate, moe_w_up, moe_w_down):
    cond = jax.nn.silu(c)
    for layer in range(DEPTH):
        mod = cond @ ada_w[layer] + ada_b[layer]
        sh1, sc1, g1, sh2, sc2, g2 = jnp.split(mod[:, None, :], 6, axis=-1)
        hmix = rms_norm(x, norm1_g[layer]) * (1.0 + sc1) + sh1
        j = layer // 2
        if layer % 2 == 0:
            y = mixer_rglru_s5(hmix, ab_w_in[j], ab_w_out[j], lru_conv_w[j], lru_conv_b[j],
                               lru_wa[j], lru_ba[j], lru_wx[j], lru_bx[j], lru_lambda[j],
                               s5_lambda_re[j], s5_lambda_im[j], s5_log_dt[j], s5_b_re[j], s5_b_im[j],
                               s5_c_re[j], s5_c_im[j], s5_d[j], s5_glu_w[j], s5_glu_b[j])
        else:
            y = mixer_sgu_diffattn(hmix, cd_w_in[j], cd_w_out[j], sg_ln_g[j], sg_ln_b[j], sg_w[j], sg_b[j],
                                   da_q_norm[j], da_k_norm[j], da_lq1[j], da_lk1[j], da_lq2[j], da_lk2[j],
                                   da_sub_g[j], rel_bias, layer)
        x = x + g1 * y
        hffn = rms_norm(x, norm2_g[layer]) * (1.0 + sc2) + sh2
        x = x + g2 * hier_moe(hffn, moe_wg[layer], moe_bg[layer], moe_we[layer], moe_be[layer],
                              moe_w_gate[layer], moe_w_up[layer], moe_w_down[layer])
    return x
```

```python
import functools
import math

import jax
import jax.numpy as jnp
from jax import lax
from jax.experimental import pallas as pl
from jax.experimental.pallas import tpu as pltpu

F32 = jnp.float32
BF16 = jnp.bfloat16

EPS = 1e-6
LRU_C = 8.0
REL_MAX_DIST = 128
MOE_TOPK = 2
NEG_INF = -1e30

V7X_LANES = 128
V7X_SUBLANES = 8
V7X_VMEM_LIMIT_BYTES = 56 * 1024 * 1024


def _cparams(semantics):
    return pltpu.CompilerParams(dimension_semantics=semantics, vmem_limit_bytes=V7X_VMEM_LIMIT_BYTES)


def _sigmoid(x):
    return 1.0 / (1.0 + jnp.exp(-x))


def _gelu(x):
    return 0.5 * x * (1.0 + jnp.tanh(math.sqrt(2.0 / math.pi) * (x + 0.044715 * (x * x * x))))


def _tile(n, want):
    t = min(n, want)
    while n % t:
        t -= 1
    return t


def _ada_kernel(c_ref, w_ref, b_ref, o_ref):
    c = c_ref[...]
    cond = c * _sigmoid(c)
    o_ref[...] = jnp.dot(cond.astype(BF16), w_ref[...].astype(BF16), preferred_element_type=F32) + b_ref[...]


def ada_modulation(c, ada_w, ada_b):
    bsz, d = c.shape
    depth, _, n = ada_w.shape
    rows = 16
    cp = jnp.zeros((rows, d), F32).at[:bsz].set(c)
    tn = _tile(n, 512)
    out = pl.pallas_call(
        _ada_kernel,
        grid=(depth, n // tn),
        in_specs=[pl.BlockSpec((rows, d), lambda l, j: (0, 0)),
                  pl.BlockSpec((None, d, tn), lambda l, j: (l, 0, j)),
                  pl.BlockSpec((None, 1, tn), lambda l, j: (l, 0, j))],
        out_specs=pl.BlockSpec((None, rows, tn), lambda l, j: (l, 0, j)),
        out_shape=jax.ShapeDtypeStruct((depth, rows, n), F32),
        compiler_params=_cparams(("arbitrary", "arbitrary")),
        name="ada_modulation",
    )(cp, ada_w, ada_b.reshape(depth, 1, n))
    return out[:, :bsz]


def _norm_mod_kernel(x_ref, g_ref, sc_ref, sh_ref, o_ref):
    x = x_ref[...]
    ms = jnp.mean(x * x, axis=-1, keepdims=True)
    y = x * lax.rsqrt(ms + EPS) * g_ref[...]
    o_ref[...] = (y * (1.0 + sc_ref[...]) + sh_ref[...]).astype(o_ref.dtype)


def norm_modulate(x, g, sc, sh, seq, out_dtype=BF16):
    t, d = x.shape
    bsz = sc.shape[0]
    tm = _tile(seq, 256)
    per = seq // tm
    return pl.pallas_call(
        _norm_mod_kernel,
        grid=(t // tm,),
        in_specs=[pl.BlockSpec((tm, d), lambda i: (i, 0)),
                  pl.BlockSpec((1, d), lambda i: (0, 0)),
                  pl.BlockSpec((None, 1, d), lambda i: (i // per, 0, 0)),
                  pl.BlockSpec((None, 1, d), lambda i: (i // per, 0, 0))],
        out_specs=pl.BlockSpec((tm, d), lambda i: (i, 0)),
        out_shape=jax.ShapeDtypeStruct((t, d), out_dtype),
        compiler_params=_cparams(("arbitrary",)),
        name="norm_modulate",
    )(x, g.reshape(1, d), sc.reshape(bsz, 1, d), sh.reshape(bsz, 1, d))


def _matmul_kernel(*refs, n_lhs, has_res):
    a_refs = refs[:n_lhs]
    w_refs = refs[n_lhs:2 * n_lhs]
    o_ref = refs[-1]
    acc = jnp.dot(a_refs[0][...], w_refs[0][...], preferred_element_type=F32)
    for a_ref, w_ref in zip(a_refs[1:], w_refs[1:]):
        acc = acc + jnp.dot(a_ref[...], w_ref[...], preferred_element_type=F32)
    if has_res:
        res_ref, gate_ref = refs[2 * n_lhs], refs[2 * n_lhs + 1]
        acc = res_ref[...] + gate_ref[...] * acc
    o_ref[...] = acc.astype(o_ref.dtype)


def matmul(lhs, rhs, out_dtype, res=None, gate=None, seq=None):
    m = lhs[0].shape[0]
    n = rhs[0].shape[1]
    tm = _tile(m if seq is None else seq, 1024)
    tn = _tile(n, 512)
    in_specs = [pl.BlockSpec((tm, a.shape[1]), lambda i, j: (i, 0)) for a in lhs]
    in_specs += [pl.BlockSpec((w.shape[0], tn), lambda i, j: (0, j)) for w in rhs]
    args = list(lhs) + list(rhs)
    if res is not None:
        per = seq // tm
        bsz = gate.shape[0]
        in_specs += [pl.BlockSpec((tm, tn), lambda i, j: (i, j)),
                     pl.BlockSpec((None, 1, tn), lambda i, j: (i // per, 0, j))]
        args += [res, gate.reshape(bsz, 1, n)]
    return pl.pallas_call(
        functools.partial(_matmul_kernel, n_lhs=len(lhs), has_res=res is not None),
        grid=(m // tm, n // tn),
        in_specs=in_specs,
        out_specs=pl.BlockSpec((tm, tn), lambda i, j: (i, j)),
        out_shape=jax.ShapeDtypeStruct((m, n), out_dtype),
        compiler_params=_cparams(("arbitrary", "arbitrary")),
        name="matmul",
    )(*args)


def _lru_kernel(x_ref, gate_ref, cw_ref, cb_ref, wax_ref, bax_ref, lam_ref, o_ref,
                xs_ref, a_ref, b_ref, carry_ref, *, heads, hd, tc, kconv):
    s = pl.program_id(1)
    pad = V7X_SUBLANES

    @pl.when(s == 0)
    def _():
        xs_ref[0:pad, :] = jnp.zeros((pad, heads * hd), F32)
        carry_ref[...] = jnp.zeros_like(carry_ref)

    @pl.when(s > 0)
    def _():
        xs_ref[0:pad, :] = xs_ref[tc:tc + pad, :]

    xs_ref[pad:pad + tc, :] = x_ref[...].astype(F32)

    row = lax.broadcasted_iota(jnp.int32, (V7X_SUBLANES, hd), 0)

    for h in range(heads):
        cols = slice(h * hd, (h + 1) * hd)
        xc = cb_ref[:, cols] + cw_ref[0:1, cols] * xs_ref[pl.ds(pad - kconv + 1, tc), cols]
        for k in range(1, kconv):
            xc = xc + cw_ref[k:k + 1, cols] * xs_ref[pl.ds(pad - kconv + 1 + k, tc), cols]
        pre = jnp.dot(xc.astype(BF16), wax_ref[h], preferred_element_type=F32) + bax_ref[h]
        r = _sigmoid(pre[:, :hd])
        gi = _sigmoid(pre[:, hd:])
        nl = -lam_ref[:, cols]
        sp = jnp.maximum(nl, 0.0) + jnp.log1p(jnp.exp(-jnp.abs(nl)))
        log_a = (-LRU_C) * r * sp
        a_ref[...] = jnp.exp(log_a)
        b_ref[...] = jnp.sqrt(1.0 - jnp.exp(2.0 * log_a)) * (gi * xc)

        def group(g, carry):
            r0 = pl.multiple_of(g * V7X_SUBLANES, V7X_SUBLANES)
            a = a_ref[pl.ds(r0, V7X_SUBLANES), :]
            b = b_ref[pl.ds(r0, V7X_SUBLANES), :]
            for d in (1, 2, 4):
                keep = row >= d
                sa = jnp.where(keep, pltpu.roll(a, d, 0), 1.0)
                sb = jnp.where(keep, pltpu.roll(b, d, 0), 0.0)
                b = b + a * sb
                a = a * sa
            hh = b + a * carry
            b_ref[pl.ds(r0, V7X_SUBLANES), :] = hh
            return jnp.broadcast_to(hh[V7X_SUBLANES - 1:V7X_SUBLANES, :], (V7X_SUBLANES, hd))

        carry = lax.fori_loop(0, tc // V7X_SUBLANES, group, carry_ref[:, cols], unroll=4)
        carry_ref[:, cols] = carry
        o_ref[:, cols] = (_gelu(gate_ref[:, cols].astype(F32)) * b_ref[...]).astype(o_ref.dtype)


def lru_mixer(z, conv_w, conv_b, wa, ba, wx, bx, lam, bsz, seq):
    t = z.shape[0]
    heads, hd, _ = wa.shape
    w = heads * hd
    kconv = conv_w.shape[0]
    tc = _tile(seq, 256)
    ns = seq // tc
    wax = jnp.concatenate([wa, wx], axis=-1).astype(BF16)
    bax = jnp.concatenate([ba.reshape(heads, 1, hd), bx.reshape(heads, 1, hd)], axis=-1)
    return pl.pallas_call(
        functools.partial(_lru_kernel, heads=heads, hd=hd, tc=tc, kconv=kconv),
        grid=(bsz, ns),
        in_specs=[pl.BlockSpec((tc, w), lambda b, s: (b * ns + s, 0)),
                  pl.BlockSpec((tc, w), lambda b, s: (b * ns + s, 1)),
                  pl.BlockSpec((kconv, w), lambda b, s: (0, 0)),
                  pl.BlockSpec((1, w), lambda b, s: (0, 0)),
                  pl.BlockSpec((heads, hd, 2 * hd), lambda b, s: (0, 0, 0)),
                  pl.BlockSpec((heads, 1, 2 * hd), lambda b, s: (0, 0, 0)),
                  pl.BlockSpec((1, w), lambda b, s: (0, 0))],
        out_specs=pl.BlockSpec((tc, w), lambda b, s: (b * ns + s, 0)),
        out_shape=jax.ShapeDtypeStruct((t, w), BF16),
        scratch_shapes=[pltpu.VMEM((tc + 2 * V7X_SUBLANES, w), F32),
                        pltpu.VMEM((tc, hd), F32),
                        pltpu.VMEM((tc, hd), F32),
                        pltpu.VMEM((V7X_SUBLANES, w), F32)],
        compiler_params=_cparams(("arbitrary", "arbitrary")),
        name="lru_mixer",
    )(z, z, conv_w, conv_b.reshape(1, w), wax, bax, lam.reshape(1, w))


S5_GROUPS_PER_BLOCK = 8


def _s5_kernel(u_ref, lre_ref, lim_ref, ldt_ref, bre_ref, bim_ref, cre_ref, cim_ref, d_ref, gw_ref, gb_ref,
               o_ref, wb_ref, am_ref, p_ref, carry_ref, re_ref, im_ref, y_ref, *, tc, nblk, cin, nst, lane_chunk):
    s = pl.program_id(1)
    gn = nblk * nst
    sub = V7X_SUBLANES

    @pl.when(s == 0)
    def _():
        lr = lre_ref[...]
        li = lim_ref[...]
        dt = jnp.exp(ldt_ref[...])
        mag = jnp.exp(lr * dt)
        ar = mag * jnp.cos(li * dt)
        ai = mag * jnp.sin(li * dt)
        den = lr * lr + li * li
        zr = ar - 1.0
        cr = (zr * lr + ai * li) / den
        ci = (ai * lr - zr * li) / den
        for j in range(nblk):
            cols = slice(j * nst, (j + 1) * nst)
            br = bre_ref[j]
            bi = bim_ref[j]
            wb_ref[j, :, 0:nst] = (cr[:, cols] * br - ci[:, cols] * bi).astype(BF16)
            wb_ref[j, :, nst:2 * nst] = (cr[:, cols] * bi + ci[:, cols] * br).astype(BF16)
        row = lax.broadcasted_iota(jnp.int32, (sub, gn), 0)
        pr, pi = ar, ai
        accr = jnp.zeros((sub, gn), F32)
        acci = jnp.zeros((sub, gn), F32)
        powers = {}
        for r in range(sub):
            powers[r + 1] = (pr, pi)
            accr = jnp.where(row == r, pr, accr)
            acci = jnp.where(row == r, pi, acci)
            pr, pi = pr * ar - pi * ai, pr * ai + pi * ar
        p_ref[0] = accr
        p_ref[1] = acci
        for idx, dd in enumerate((1, 2, 4)):
            qr, qi = powers[dd]
            am_ref[2 * idx] = jnp.where(row >= dd, qr, 0.0)
            am_ref[2 * idx + 1] = jnp.where(row >= dd, qi, 0.0)
        carry_ref[...] = jnp.zeros_like(carry_ref)

    u = u_ref[...].astype(F32)
    ub = u.astype(BF16)
    for j in range(nblk):
        bu = jnp.dot(ub[:, j * cin:(j + 1) * cin], wb_ref[j], preferred_element_type=F32)
        re_ref[:, j * nst:(j + 1) * nst] = bu[:, :nst]
        im_ref[:, j * nst:(j + 1) * nst] = bu[:, nst:]

    for c in range(gn // lane_chunk):
        lsl = slice(c * lane_chunk, (c + 1) * lane_chunk)

        def group(g, carry, lsl=lsl):
            cr_, ci_ = carry
            r0 = pl.multiple_of(g * sub, sub)
            rows = pl.ds(r0, sub)
            xr = re_ref[rows, lsl]
            xi = im_ref[rows, lsl]
            for idx, dd in enumerate((1, 2, 4)):
                mr = am_ref[2 * idx, :, lsl]
                mi = am_ref[2 * idx + 1, :, lsl]
                sr = pltpu.roll(xr, dd, 0)
                si = pltpu.roll(xi, dd, 0)
                xr, xi = xr + (mr * sr - mi * si), xi + (mr * si + mi * sr)
            pr_ = p_ref[0, :, lsl]
            pi_ = p_ref[1, :, lsl]
            hr = xr + (pr_ * cr_ - pi_ * ci_)
            hi = xi + (pr_ * ci_ + pi_ * cr_)
            re_ref[rows, lsl] = hr
            im_ref[rows, lsl] = hi
            return (jnp.broadcast_to(hr[sub - 1:sub, :], (sub, lane_chunk)),
                    jnp.broadcast_to(hi[sub - 1:sub, :], (sub, lane_chunk)))

        cr_, ci_ = lax.fori_loop(0, tc // sub, group, (carry_ref[0, :, lsl], carry_ref[1, :, lsl]), unroll=2)
        carry_ref[0, :, lsl] = cr_
        carry_ref[1, :, lsl] = ci_

    for j in range(nblk):
        cols = slice(j * nst, (j + 1) * nst)
        yj = (jnp.dot(re_ref[:, cols].astype(BF16), cre_ref[j], preferred_element_type=F32)
              - jnp.dot(im_ref[:, cols].astype(BF16), cim_ref[j], preferred_element_type=F32))
        ucols = slice(j * cin, (j + 1) * cin)
        y_ref[:, ucols] = yj + d_ref[:, ucols] * u[:, ucols]
    g = _gelu(y_ref[...])
    gate = _sigmoid(jnp.dot(g.astype(BF16), gw_ref[...], preferred_element_type=F32) + gb_ref[...])
    o_ref[...] = (g * gate).astype(o_ref.dtype)


def s5_mixer(u, lam_re, lam_im, log_dt, b_re, b_im, c_re, c_im, d, glu_w, glu_b, bsz, seq):
    t, ws = u.shape
    groups, nstate, gch = b_re.shape
    gpb = S5_GROUPS_PER_BLOCK
    nblk = groups // gpb
    cin = gpb * gch
    nst = gpb * nstate
    gn = groups * nstate
    tc = _tile(seq, 256)
    ns = seq // tc
    lane_chunk = _tile(gn, 512)
    eye = jnp.eye(gpb, dtype=F32)

    def bdiag_in(b):
        bb = b.reshape(nblk, gpb, nstate, gch).transpose(0, 1, 3, 2)
        return (bb[:, :, :, None, :] * eye[None, :, None, :, None]).reshape(nblk, cin, nst)

    def bdiag_out(c):
        cc = c.reshape(nblk, gpb, gch, nstate).transpose(0, 1, 3, 2)
        return (cc[:, :, :, None, :] * eye[None, :, None, :, None]).reshape(nblk, nst, cin)

    const2 = lambda b, s: (0, 0)
    const3 = lambda b, s: (0, 0, 0)
    return pl.pallas_call(
        functools.partial(_s5_kernel, tc=tc, nblk=nblk, cin=cin, nst=nst, lane_chunk=lane_chunk),
        grid=(bsz, ns),
        in_specs=[pl.BlockSpec((tc, ws), lambda b, s: (b * ns + s, 0)),
                  pl.BlockSpec((1, gn), const2), pl.BlockSpec((1, gn), const2), pl.BlockSpec((1, gn), const2),
                  pl.BlockSpec((nblk, cin, nst), const3), pl.BlockSpec((nblk, cin, nst), const3),
                  pl.BlockSpec((nblk, nst, cin), const3), pl.BlockSpec((nblk, nst, cin), const3),
                  pl.BlockSpec((1, ws), const2),
                  pl.BlockSpec((ws, ws), const2),
                  pl.BlockSpec((1, ws), const2)],
        out_specs=pl.BlockSpec((tc, ws), lambda b, s: (b * ns + s, 0)),
        out_shape=jax.ShapeDtypeStruct((t, ws), BF16),
        scratch_shapes=[pltpu.VMEM((nblk, cin, 2 * nst), BF16),
                        pltpu.VMEM((6, V7X_SUBLANES, gn), F32),
                        pltpu.VMEM((2, V7X_SUBLANES, gn), F32),
                        pltpu.VMEM((2, V7X_SUBLANES, gn), F32),
                        pltpu.VMEM((tc, gn), F32),
                        pltpu.VMEM((tc, gn), F32),
                        pltpu.VMEM((tc, ws), F32)],
        compiler_params=_cparams(("arbitrary", "arbitrary")),
        name="s5_mixer",
    )(u, lam_re.reshape(1, gn), lam_im.reshape(1, gn),
      jnp.broadcast_to(log_dt[:, None], (groups, nstate)).reshape(1, gn),
      bdiag_in(b_re), bdiag_in(b_im), bdiag_out(c_re).astype(BF16), bdiag_out(c_im).astype(BF16),
      d.reshape(1, ws), glu_w.astype(BF16), glu_b.reshape(1, ws))


def _sgu_kernel(u_ref, v_ref, g_ref, b_ref, w_ref, bs_ref, o_ref, *, heads, hd, chunk, nchunk):
    v = _gelu(v_ref[...].astype(F32))
    mu = jnp.mean(v, axis=-1, keepdims=True)
    vc = v - mu
    var = jnp.mean(vc * vc, axis=-1, keepdims=True)
    vn = (vc * lax.rsqrt(var + EPS) * g_ref[...] + b_ref[...]).astype(BF16)
    r = lax.broadcasted_iota(jnp.int32, (chunk, chunk), 0)
    c = lax.broadcasted_iota(jnp.int32, (chunk, chunk), 1)
    tril = r >= c
    for h in range(heads):
        wh = jnp.where(tril, w_ref[h], 0.0).astype(BF16)
        cols = slice(h * hd, (h + 1) * hd)
        for n in range(nchunk):
            rows = slice(n * chunk, (n + 1) * chunk)
            gsp = jnp.dot(wh, vn[rows, cols], preferred_element_type=F32) + bs_ref[h]
            o_ref[rows, cols] = (_gelu(u_ref[rows, cols].astype(F32)) * gsp).astype(o_ref.dtype)


def sgu_mixer(z, ln_g, ln_b, w_s, b_s):
    t = z.shape[0]
    heads, chunk, _ = w_s.shape
    w = ln_g.shape[0]
    hd = w // heads
    nchunk = 2 if (t // chunk) % 2 == 0 else 1
    tm = nchunk * chunk
    bs = jnp.broadcast_to(b_s[:, :, None], (heads, chunk, hd))
    return pl.pallas_call(
        functools.partial(_sgu_kernel, heads=heads, hd=hd, chunk=chunk, nchunk=nchunk),
        grid=(t // tm,),
        in_specs=[pl.BlockSpec((tm, w), lambda i: (i, 0)),
                  pl.BlockSpec((tm, w), lambda i: (i, 1)),
                  pl.BlockSpec((1, w), lambda i: (0, 0)),
                  pl.BlockSpec((1, w), lambda i: (0, 0)),
                  pl.BlockSpec((heads, chunk, chunk), lambda i: (0, 0, 0)),
                  pl.BlockSpec((heads, chunk, hd), lambda i: (0, 0, 0))],
        out_specs=pl.BlockSpec((tm, w), lambda i: (i, 0)),
        out_shape=jax.ShapeDtypeStruct((t, w), BF16),
        compiler_params=_cparams(("arbitrary",)),
        name="sgu_mixer",
    )(z, z, ln_g.reshape(1, w), ln_b.reshape(1, w), w_s, bs)


def _qk_norm_kernel(x_ref, g_ref, o_ref, *, nseg, seg):
    x = x_ref[...].astype(F32)
    for i in range(nseg):
        cols = slice(i * seg, (i + 1) * seg)
        xs = x[:, cols]
        ms = jnp.mean(xs * xs, axis=-1, keepdims=True)
        o_ref[:, cols] = (xs * lax.rsqrt(ms + EPS) * g_ref[:, cols]).astype(o_ref.dtype)


def qk_norm(z, col_block, width, gain_row, seg):
    t = z.shape[0]
    tm = _tile(t, 512)
    return pl.pallas_call(
        functools.partial(_qk_norm_kernel, nseg=width // seg, seg=seg),
        grid=(t // tm,),
        in_specs=[pl.BlockSpec((tm, width), lambda i: (i, col_block)),
                  pl.BlockSpec((1, width), lambda i: (0, 0))],
        out_specs=pl.BlockSpec((tm, width), lambda i: (i, 0)),
        out_shape=jax.ShapeDtypeStruct((t, width), BF16),
        compiler_params=_cparams(("arbitrary",)),
        name="qk_norm",
    )(z, gain_row)


def _attn_kernel(q_ref, k_ref, v_ref, bias_ref, lq1_ref, lk1_ref, lq2_ref, lk2_ref, sg_ref, o_ref,
                 acc_ref, m_ref, l_ref, *, tq, dqk, lam_init):
    qi = pl.program_id(2)
    m_ref[...] = jnp.full_like(m_ref, NEG_INF)
    l_ref[...] = jnp.zeros_like(l_ref)
    acc_ref[...] = jnp.zeros_like(acc_ref)
    r = lax.broadcasted_iota(jnp.int32, (tq, tq), 0)
    c = lax.broadcasted_iota(jnp.int32, (tq, tq), 1)
    causal = r >= c

    def block(kb, bias_idx, masked):
        k0 = pl.multiple_of(kb * tq, tq)
        v = v_ref[pl.ds(k0, tq), :]
        for mp in range(2):
            q = q_ref[:, mp * dqk:(mp + 1) * dqk]
            k = k_ref[pl.ds(k0, tq), mp * dqk:(mp + 1) * dqk]
            sc = lax.dot_general(q, k, (((1,), (1,)), ((), ())), preferred_element_type=F32)
            if bias_idx is not None:
                sc = sc + bias_ref[bias_idx]
            if masked:
                sc = jnp.where(causal, sc, NEG_INF)
            m_old = m_ref[mp]
            m_new = jnp.maximum(m_old, jnp.max(sc, axis=-1, keepdims=True))
            alpha = jnp.exp(m_old - m_new)
            p = jnp.exp(sc - m_new)
            l_ref[mp] = alpha * l_ref[mp] + jnp.sum(p, axis=-1, keepdims=True)
            acc_ref[mp] = alpha * acc_ref[mp] + jnp.dot(p.astype(BF16), v, preferred_element_type=F32)
            m_ref[mp] = m_new

    def far(kb, _):
        block(kb, None, False)
        return 0

    lax.fori_loop(0, jnp.maximum(qi - 1, 0), far, 0)

    @pl.when(qi > 0)
    def _():
        block(qi - 1, 1, False)

    block(qi, 0, True)

    lam = (jnp.exp(jnp.sum(lq1_ref[...] * lk1_ref[...], axis=-1, keepdims=True))
           - jnp.exp(jnp.sum(lq2_ref[...] * lk2_ref[...], axis=-1, keepdims=True)) + lam_init)
    o = acc_ref[0] / l_ref[0] - lam * (acc_ref[1] / l_ref[1])
    ms = jnp.mean(o * o, axis=-1, keepdims=True)
    o_ref[...] = (o * lax.rsqrt(ms + EPS) * sg_ref[...] * (1.0 - lam_init)).astype(o_ref.dtype)


def _t5_bucket(n, buckets):
    max_exact = buckets // 2
    nf = jnp.maximum(n, 1).astype(F32)
    large = max_exact + (jnp.log(nf / max_exact) / math.log(REL_MAX_DIST / max_exact)
                         * (buckets - max_exact)).astype(jnp.int32)
    large = jnp.minimum(large, buckets - 1)
    return jnp.where(n < max_exact, n, large)


def diff_attention(qn, kn, zqkv, v_col_block, rel_bias, lq1, lk1, lq2, lk2, sub_g, bsz, seq, layer):
    t = qn.shape[0]
    buckets, heads = rel_bias.shape
    dv = sub_g.shape[0]
    dqk = dv // 2
    tq = _tile(seq, 256)
    assert tq >= REL_MAX_DIST, "far key blocks must all fall in the last relative-position bucket"
    nq = seq // tq
    lam_init = 0.8 - 0.6 * math.exp(-0.3 * layer)
    idx = jnp.arange(tq)
    dist = idx[:, None] - idx[None, :]
    table = rel_bias.astype(F32)
    far_const = table[_t5_bucket(jnp.int32(2 * tq), buckets)]
    tiles = jnp.stack([table[_t5_bucket(jnp.maximum(dist, 0), buckets)],
                       table[_t5_bucket(dist + tq, buckets)]], axis=0)
    bias = jnp.transpose(tiles - far_const, (3, 0, 1, 2))
    row = lambda b, h, i: (0, 0)
    return pl.pallas_call(
        functools.partial(_attn_kernel, tq=tq, dqk=dqk, lam_init=lam_init),
        grid=(bsz, heads, nq),
        in_specs=[pl.BlockSpec((tq, dv), lambda b, h, i: (b * nq + i, h)),
                  pl.BlockSpec((seq, dv), lambda b, h, i: (b, h)),
                  pl.BlockSpec((seq, dv), lambda b, h, i: (b, v_col_block + h)),
                  pl.BlockSpec((None, 2, tq, tq), lambda b, h, i: (h, 0, 0, 0)),
                  pl.BlockSpec((1, dqk), row), pl.BlockSpec((1, dqk), row),
                  pl.BlockSpec((1, dqk), row), pl.BlockSpec((1, dqk), row),
                  pl.BlockSpec((1, dv), row)],
        out_specs=pl.BlockSpec((tq, dv), lambda b, h, i: (b * nq + i, h)),
        out_shape=jax.ShapeDtypeStruct((t, heads * dv), BF16),
        scratch_shapes=[pltpu.VMEM((2, tq, dv), F32),
                        pltpu.VMEM((2, tq, 1), F32),
                        pltpu.VMEM((2, tq, 1), F32)],
        compiler_params=_cparams(("arbitrary", "arbitrary", "arbitrary")),
        name="diff_attention",
    )(qn, kn, zqkv, bias, lq1.reshape(1, dqk), lk1.reshape(1, dqk), lq2.reshape(1, dqk), lk2.reshape(1, dqk),
      sub_g.reshape(1, dv))


ROUTER_LANES = 128


def _norm_router_kernel(x_ref, g_ref, sc_ref, sh_ref, wr_ref, wlo_ref, br_ref, h_ref, route_ref, *, ngroups, per_group):
    x = x_ref[...]
    ms = jnp.mean(x * x, axis=-1, keepdims=True)
    h = (x * lax.rsqrt(ms + EPS) * g_ref[...]) * (1.0 + sc_ref[...]) + sh_ref[...]
    h_ref[...] = h
    hi = h.astype(BF16)
    lo = (h - hi.astype(F32)).astype(BF16)
    both = jnp.dot(hi, wr_ref[...], preferred_element_type=F32)
    nl = ROUTER_LANES
    logits = both[:, :nl] + both[:, nl:] + jnp.dot(lo, wlo_ref[...], preferred_element_type=F32) + br_ref[...]
    lane = lax.broadcasted_iota(jnp.int32, logits.shape, 1).astype(F32)
    big = float(nl)
    glog = jnp.where(lane < ngroups, logits, NEG_INF)
    gmax = jnp.max(glog, axis=-1, keepdims=True)
    gsum = jnp.sum(jnp.exp(glog - gmax), axis=-1, keepdims=True)
    gp = 1.0 / gsum
    gidx = jnp.min(jnp.where(glog == gmax, lane, big), axis=-1, keepdims=True)
    lo_lane = ngroups + gidx * per_group
    emask = (lane >= lo_lane) & (lane < lo_lane + per_group)
    elog = jnp.where(emask, logits, NEG_INF)
    emax = jnp.max(elog, axis=-1, keepdims=True)
    eexp = jnp.where(emask, jnp.exp(elog - emax), -1.0)
    i0 = jnp.min(jnp.where(eexp == 1.0, lane, big), axis=-1, keepdims=True)
    rest = jnp.where(lane == i0, -1.0, eexp)
    p1 = jnp.max(rest, axis=-1, keepdims=True)
    i1 = jnp.min(jnp.where(rest == p1, lane, big), axis=-1, keepdims=True)
    denom = 1.0 + p1
    w0 = gp * (1.0 / denom)
    w1 = gp * (p1 / denom)
    e0 = i0 - ngroups
    e1 = i1 - ngroups
    route_ref[...] = jnp.where(lane == 0, e0, jnp.where(lane == 1, e1, jnp.where(lane == 2, w0,
                               jnp.where(lane == 3, w1, 0.0))))


def norm_router(x, g, sc, sh, wg, bg, we, be, seq):
    t, d = x.shape
    bsz = sc.shape[0]
    ngroups = wg.shape[-1]
    per_group = we.shape[-1]
    nexp = ngroups * per_group
    nl = ROUTER_LANES
    assert ngroups + nexp <= nl
    wr = jnp.concatenate([wg, jnp.transpose(we, (1, 0, 2)).reshape(d, nexp)], axis=-1)
    wr = jnp.zeros((d, nl), F32).at[:, :ngroups + nexp].set(wr)
    w_hi = wr.astype(BF16)
    w_lo = (wr - w_hi.astype(F32)).astype(BF16)
    br = jnp.zeros((1, nl), F32).at[0, :ngroups + nexp].set(jnp.concatenate([bg, be.reshape(nexp)]))
    tm = _tile(seq, 256)
    per = seq // tm
    h, route = pl.pallas_call(
        functools.partial(_norm_router_kernel, ngroups=ngroups, per_group=per_group),
        grid=(t // tm,),
        in_specs=[pl.BlockSpec((tm, d), lambda i: (i, 0)),
                  pl.BlockSpec((1, d), lambda i: (0, 0)),
                  pl.BlockSpec((None, 1, d), lambda i: (i // per, 0, 0)),
                  pl.BlockSpec((None, 1, d), lambda i: (i // per, 0, 0)),
                  pl.BlockSpec((d, 2 * nl), lambda i: (0, 0)),
                  pl.BlockSpec((d, nl), lambda i: (0, 0)),
                  pl.BlockSpec((1, nl), lambda i: (0, 0))],
        out_specs=[pl.BlockSpec((tm, d), lambda i: (i, 0)),
                   pl.BlockSpec((tm, nl), lambda i: (i, 0))],
        out_shape=[jax.ShapeDtypeStruct((t, d), F32), jax.ShapeDtypeStruct((t, nl), F32)],
        compiler_params=_cparams(("arbitrary",)),
        name="norm_router",
    )(x, g.reshape(1, d), sc.reshape(bsz, 1, d), sh.reshape(bsz, 1, d),
      jnp.concatenate([w_hi, w_lo], axis=-1), w_hi, br)
    return h, route[:, 0:2].astype(jnp.int32), route[:, 2:4]


def _expert_kernel(texp_ref, tvalid_ref, src_ref, dst_ref, h_hbm, wrow_ref, wg_ref, wu_ref, wd_ref, y_hbm,
                   xbuf, obuf, sem_in, sem_out, *, tm):
    i = pl.program_id(0)
    nvalid = tvalid_ref[i]

    @pl.when(nvalid > 0)
    def _():
        base = i * tm

        def gather(r, _):
            pltpu.make_async_copy(h_hbm.at[pl.ds(src_ref[base + r], 1)], xbuf.at[pl.ds(r, 1)], sem_in).start()
            return 0

        lax.fori_loop(0, tm, gather, 0, unroll=8)

        def gather_wait(r, _):
            pltpu.make_async_copy(h_hbm.at[pl.ds(0, 1)], xbuf.at[pl.ds(r, 1)], sem_in).wait()
            return 0

        lax.fori_loop(0, tm, gather_wait, 0)

        x = xbuf[...].astype(BF16)
        hg = jnp.dot(x, wg_ref[...], preferred_element_type=F32)
        hu = jnp.dot(x, wu_ref[...], preferred_element_type=F32)
        act = (hg * _sigmoid(hg)) * hu * wrow_ref[...]
        obuf[...] = jnp.dot(act.astype(BF16), wd_ref[...], preferred_element_type=F32)

        def scatter(r, _):
            pltpu.make_async_copy(obuf.at[pl.ds(r, 1)], y_hbm.at[pl.ds(dst_ref[base + r], 1)], sem_out).start()
            return 0

        lax.fori_loop(0, nvalid, scatter, 0)

        def scatter_wait(r, _):
            pltpu.make_async_copy(obuf.at[pl.ds(r, 1)], y_hbm.at[pl.ds(0, 1)], sem_out).wait()
            return 0

        lax.fori_loop(0, nvalid, scatter_wait, 0)


def routed_experts(h, eid, wts, w_gate, w_up, w_down):
    t, d = h.shape
    nexp, _, f = w_gate.shape
    k = eid.shape[1]
    npair = t * k
    tm = _tile(npair // nexp, 256) if npair // nexp >= 8 else 8
    ntile = npair // tm + nexp
    ppad = ntile * tm
    flat_e = jnp.transpose(eid).reshape(npair)
    flat_w = jnp.transpose(wts).reshape(npair)
    onehot = (flat_e[:, None] == jnp.arange(nexp)[None, :]).astype(jnp.int32)
    csum = jnp.cumsum(onehot, axis=0)
    rank = jnp.sum(csum * onehot, axis=1) - 1
    counts = csum[-1]
    tiles_per = (counts + tm - 1) // tm
    tile_end = jnp.cumsum(tiles_per)
    tile_start = tile_end - tiles_per
    pos = tile_start[flat_e] * tm + rank
    pair = jnp.arange(npair, dtype=jnp.int32)
    src = jnp.zeros((ppad,), jnp.int32).at[pos].set(pair % t)
    dst = jnp.zeros((ppad,), jnp.int32).at[pos].set(pair)
    wrow = jnp.zeros((ppad,), F32).at[pos].set(flat_w)
    tile_id = jnp.arange(ntile, dtype=jnp.int32)
    tact = (tile_id < tile_end[-1]).astype(jnp.int32)
    texp = jnp.minimum(jnp.sum((tile_id[:, None] >= tile_end[None, :]).astype(jnp.int32), axis=1), nexp - 1)
    last_exp = jnp.max(jnp.where(counts > 0, jnp.arange(nexp), 0))
    tvalid = jnp.where(tact > 0, jnp.clip(counts[texp] - (tile_id - tile_start[texp]) * tm, 0, tm), 0)
    tvalid = tvalid.astype(jnp.int32)
    texp = jnp.where(tact > 0, texp, last_exp).astype(jnp.int32)

    grid_spec = pltpu.PrefetchScalarGridSpec(
        num_scalar_prefetch=4,
        grid=(ntile,),
        in_specs=[pl.BlockSpec(memory_space=pl.ANY),
                  pl.BlockSpec((None, tm, 1), lambda i, te, ta, s, dd: (i, 0, 0)),
                  pl.BlockSpec((None, d, f), lambda i, te, ta, s, dd: (te[i], 0, 0)),
                  pl.BlockSpec((None, d, f), lambda i, te, ta, s, dd: (te[i], 0, 0)),
                  pl.BlockSpec((None, f, d), lambda i, te, ta, s, dd: (te[i], 0, 0))],
        out_specs=pl.BlockSpec(memory_space=pl.ANY),
        scratch_shapes=[pltpu.VMEM((tm, d), F32), pltpu.VMEM((tm, d), F32),
                        pltpu.SemaphoreType.DMA(()), pltpu.SemaphoreType.DMA(())])
    return pl.pallas_call(
        functools.partial(_expert_kernel, tm=tm),
        grid_spec=grid_spec,
        out_shape=jax.ShapeDtypeStruct((npair, d), F32),
        compiler_params=_cparams(("arbitrary",)),
        name="routed_experts",
    )(texp, tvalid, src, dst, h, wrow.reshape(ntile, tm, 1), w_gate, w_up, w_down)


def _combine_kernel(x_ref, y0_ref, y1_ref, g_ref, o_ref):
    o_ref[...] = x_ref[...] + g_ref[...] * (y0_ref[...] + y1_ref[...])


def moe_combine(x, y, gate, seq):
    t, d = x.shape
    bsz = gate.shape[0]
    tm = _tile(seq, 256)
    per = seq // tm
    nt = t // tm
    return pl.pallas_call(
        _combine_kernel,
        grid=(nt,),
        in_specs=[pl.BlockSpec((tm, d), lambda i: (i, 0)),
                  pl.BlockSpec((tm, d), lambda i: (i, 0)),
                  pl.BlockSpec((tm, d), lambda i: (i + nt, 0)),
                  pl.BlockSpec((None, 1, d), lambda i: (i // per, 0, 0))],
        out_specs=pl.BlockSpec((tm, d), lambda i: (i, 0)),
        out_shape=jax.ShapeDtypeStruct((t, d), F32),
        compiler_params=_cparams(("arbitrary",)),
        name="moe_combine",
    )(x, y, y, gate.reshape(bsz, 1, d))


def kernel(x, c, norm1_g, norm2_g, ada_w, ada_b, ab_w_in, ab_w_out, lru_conv_w, lru_conv_b, lru_wa, lru_ba, lru_wx, lru_bx, lru_lambda, s5_lambda_re, s5_lambda_im, s5_log_dt, s5_b_re, s5_b_im, s5_c_re, s5_c_im, s5_d, s5_glu_w, s5_glu_b, cd_w_in, cd_w_out, sg_ln_g, sg_ln_b, sg_w, sg_b, da_q_norm, da_k_norm, da_lq1, da_lk1, da_lq2, da_lk2, da_sub_g, rel_bias, moe_wg, moe_bg, moe_we, moe_be, moe_w_gate, moe_w_up, moe_w_down):
    bsz, seq, d = x.shape
    depth = norm1_g.shape[0]
    t = bsz * seq
    xt = x.reshape(t, d)
    mod = ada_modulation(c, ada_w, ada_b)

    for layer in range(depth):
        sh1, sc1, g1, sh2, sc2, g2 = [mod[layer, :, i * d:(i + 1) * d] for i in range(6)]
        hmix = norm_modulate(xt, norm1_g[layer], sc1, sh1, seq)
        j = layer // 2
        if layer % 2 == 0:
            lw = lru_conv_w.shape[-1]
            w_in = ab_w_in[j].astype(BF16)
            w_out = ab_w_out[j].astype(BF16)
            z_lru = matmul([hmix], [w_in[:, :2 * lw]], F32)
            z_s5 = matmul([hmix], [w_in[:, 2 * lw:]], F32)
            y_a = lru_mixer(z_lru, lru_conv_w[j], lru_conv_b[j], lru_wa[j], lru_ba[j], lru_wx[j], lru_bx[j],
                            lru_lambda[j], bsz, seq)
            y_b = s5_mixer(z_s5, s5_lambda_re[j], s5_lambda_im[j], s5_log_dt[j], s5_b_re[j], s5_b_im[j],
                           s5_c_re[j], s5_c_im[j], s5_d[j], s5_glu_w[j], s5_glu_b[j], bsz, seq)
            xt = matmul([y_a, y_b], [w_out[:lw], w_out[lw:]], F32, res=xt, gate=g1, seq=seq)
        else:
            sgw = sg_ln_g.shape[-1]
            dqk = da_q_norm.shape[-1]
            dv = da_sub_g.shape[-1]
            heads = rel_bias.shape[1]
            daw = heads * dv
            w_in = cd_w_in[j].astype(BF16)
            w_out = cd_w_out[j].astype(BF16)
            z_sg = matmul([hmix], [w_in[:, :2 * sgw]], BF16)
            z_qkv = matmul([hmix], [w_in[:, 2 * sgw:]], BF16)
            y_c = sgu_mixer(z_sg, sg_ln_g[j], sg_ln_b[j], sg_w[j], sg_b[j])
            q_gain = jnp.tile(da_q_norm[j] * (dqk ** -0.5), daw // dqk).reshape(1, daw)
            k_gain = jnp.tile(da_k_norm[j], daw // dqk).reshape(1, daw)
            qn = qk_norm(z_qkv, 0, daw, q_gain, dqk)
            kn = qk_norm(z_qkv, 1, daw, k_gain, dqk)
            y_d = diff_attention(qn, kn, z_qkv, 2 * daw // dv, rel_bias, da_lq1[j], da_lk1[j], da_lq2[j], da_lk2[j],
                                 da_sub_g[j], bsz, seq, layer)
            xt = matmul([y_c, y_d], [w_out[:sgw], w_out[sgw:]], F32, res=xt, gate=g1, seq=seq)
        hffn, eid, wts = norm_router(xt, norm2_g[layer], sc2, sh2, moe_wg[layer], moe_bg[layer], moe_we[layer],
                                     moe_be[layer], seq)
        y = routed_experts(hffn, eid, wts, moe_w_gate[layer].astype(BF16), moe_w_up[layer].astype(BF16),
                           moe_w_down[layer].astype(BF16))
        xt = moe_combine(xt, y, g2, seq)
    return xt.reshape(bsz, seq, d)
```

```python
import functools
import math

import jax
import jax.numpy as jnp
from jax import lax
from jax.experimental import pallas as pl
from jax.experimental.pallas import tpu as pltpu

F32 = jnp.float32
BF16 = jnp.bfloat16

EPS = 1e-6
LRU_C = 8.0
REL_MAX_DIST = 128
MOE_TOPK = 2
NEG_INF = -1e30
LOG2E = math.log2(math.e)

V7X_LANES = 128
V7X_SUBLANES = 8
V7X_VMEM_LIMIT_BYTES = 56 * 1024 * 1024


def _cparams(semantics):
    return pltpu.CompilerParams(dimension_semantics=semantics, vmem_limit_bytes=V7X_VMEM_LIMIT_BYTES)


def _sigmoid(x):
    return 1.0 / (1.0 + jnp.exp(-x))


def _gelu(x):
    return 0.5 * x * (1.0 + jnp.tanh(math.sqrt(2.0 / math.pi) * (x + 0.044715 * (x * x * x))))


def _tile(n, want):
    t = min(n, want)
    while n % t:
        t -= 1
    return t


def _ada_kernel(c_ref, w_ref, b_ref, o_ref):
    c = c_ref[...]
    cond = c * _sigmoid(c)
    o_ref[...] = jnp.dot(cond.astype(BF16), w_ref[...].astype(BF16), preferred_element_type=F32) + b_ref[...]


def ada_modulation(c, ada_w, ada_b):
    bsz, d = c.shape
    depth, _, n = ada_w.shape
    rows = 16
    cp = jnp.zeros((rows, d), F32).at[:bsz].set(c)
    tn = _tile(n, 512)
    out = pl.pallas_call(
        _ada_kernel,
        grid=(depth, n // tn),
        in_specs=[pl.BlockSpec((rows, d), lambda l, j: (0, 0)),
                  pl.BlockSpec((None, d, tn), lambda l, j: (l, 0, j)),
                  pl.BlockSpec((None, 1, tn), lambda l, j: (l, 0, j))],
        out_specs=pl.BlockSpec((None, rows, tn), lambda l, j: (l, 0, j)),
        out_shape=jax.ShapeDtypeStruct((depth, rows, n), F32),
        compiler_params=_cparams(("arbitrary", "arbitrary")),
        name="ada_modulation",
    )(cp, ada_w, ada_b.reshape(depth, 1, n))
    return out[:, :bsz]


def _norm_mod_kernel(x_ref, g_ref, sc_ref, sh_ref, o_ref):
    x = x_ref[...]
    ms = jnp.mean(x * x, axis=-1, keepdims=True)
    y = x * lax.rsqrt(ms + EPS) * g_ref[...]
    o_ref[...] = (y * (1.0 + sc_ref[...]) + sh_ref[...]).astype(o_ref.dtype)


def norm_modulate(x, g, sc, sh, seq, out_dtype=BF16):
    t, d = x.shape
    bsz = sc.shape[0]
    tm = _tile(seq, 256)
    per = seq // tm
    return pl.pallas_call(
        _norm_mod_kernel,
        grid=(t // tm,),
        in_specs=[pl.BlockSpec((tm, d), lambda i: (i, 0)),
                  pl.BlockSpec((1, d), lambda i: (0, 0)),
                  pl.BlockSpec((None, 1, d), lambda i: (i // per, 0, 0)),
                  pl.BlockSpec((None, 1, d), lambda i: (i // per, 0, 0))],
        out_specs=pl.BlockSpec((tm, d), lambda i: (i, 0)),
        out_shape=jax.ShapeDtypeStruct((t, d), out_dtype),
        compiler_params=_cparams(("arbitrary",)),
        name="norm_modulate",
    )(x, g.reshape(1, d), sc.reshape(bsz, 1, d), sh.reshape(bsz, 1, d))


def _matmul_kernel(*refs, ksplit, has_res):
    n_lhs = len(ksplit)
    a_refs = refs[:n_lhs]
    w_ref = refs[n_lhs]
    o_ref = refs[-1]
    acc = None
    k0 = 0
    for a_ref, kk in zip(a_refs, ksplit):
        part = jnp.dot(a_ref[...], w_ref[k0:k0 + kk, :], preferred_element_type=F32)
        acc = part if acc is None else acc + part
        k0 += kk
    if has_res:
        res_ref, gate_ref = refs[n_lhs + 1], refs[n_lhs + 2]
        acc = res_ref[...] + gate_ref[...] * acc
    o_ref[...] = acc.astype(o_ref.dtype)


def matmul(lhs, w, out_dtype, col_off=0, ncols=None, res=None, gate=None, seq=None):
    m = lhs[0].shape[0]
    ktot = w.shape[0]
    ksplit = tuple(a.shape[1] for a in lhs)
    assert sum(ksplit) == ktot
    n = w.shape[1] - col_off if ncols is None else ncols
    tm = _tile(m if seq is None else seq, 1024)
    tn = _tile(math.gcd(n, col_off) if col_off else n, 512)
    joff = col_off // tn
    in_specs = [pl.BlockSpec((tm, kk), lambda i, j: (i, 0)) for kk in ksplit]
    in_specs += [pl.BlockSpec((ktot, tn), lambda i, j: (0, j + joff))]
    args = list(lhs) + [w]
    if res is not None:
        per = seq // tm
        bsz = gate.shape[0]
        in_specs += [pl.BlockSpec((tm, tn), lambda i, j: (i, j)),
                     pl.BlockSpec((None, 1, tn), lambda i, j: (i // per, 0, j))]
        args += [res, gate.reshape(bsz, 1, n)]
    return pl.pallas_call(
        functools.partial(_matmul_kernel, ksplit=ksplit, has_res=res is not None),
        grid=(m // tm, n // tn),
        in_specs=in_specs,
        out_specs=pl.BlockSpec((tm, tn), lambda i, j: (i, j)),
        out_shape=jax.ShapeDtypeStruct((m, n), out_dtype),
        compiler_params=_cparams(("arbitrary", "arbitrary")),
        name="matmul",
    )(*args)


def _lru_kernel(x_ref, gate_ref, cw_ref, cb_ref, wax_ref, bax_ref, lam_ref, o_ref,
                xs_ref, a_ref, b_ref, carry_ref, *, heads, hd, tc, kconv):
    s = pl.program_id(1)
    pad = V7X_SUBLANES

    @pl.when(s == 0)
    def _():
        xs_ref[0:pad, :] = jnp.zeros((pad, heads * hd), F32)
        carry_ref[...] = jnp.zeros_like(carry_ref)

    @pl.when(s > 0)
    def _():
        xs_ref[0:pad, :] = xs_ref[tc:tc + pad, :]

    xs_ref[pad:pad + tc, :] = x_ref[...].astype(F32)

    row = lax.broadcasted_iota(jnp.int32, (V7X_SUBLANES, hd), 0)

    for h in range(heads):
        cols = slice(h * hd, (h + 1) * hd)
        xc = cb_ref[:, cols] + cw_ref[0:1, cols] * xs_ref[pl.ds(pad - kconv + 1, tc), cols]
        for k in range(1, kconv):
            xc = xc + cw_ref[k:k + 1, cols] * xs_ref[pl.ds(pad - kconv + 1 + k, tc), cols]
        pre = jnp.dot(xc.astype(BF16), wax_ref[h], preferred_element_type=F32) + bax_ref[h]
        r = _sigmoid(pre[:, :hd])
        gi = _sigmoid(pre[:, hd:])
        nl = -lam_ref[:, cols]
        sp = jnp.maximum(nl, 0.0) + jnp.log1p(jnp.exp(-jnp.abs(nl)))
        log_a = (-LRU_C) * r * sp
        a_ref[...] = jnp.exp(log_a)
        b_ref[...] = jnp.sqrt(1.0 - jnp.exp(2.0 * log_a)) * (gi * xc)

        def group(g, carry):
            r0 = pl.multiple_of(g * V7X_SUBLANES, V7X_SUBLANES)
            a = a_ref[pl.ds(r0, V7X_SUBLANES), :]
            b = b_ref[pl.ds(r0, V7X_SUBLANES), :]
            for d in (1, 2, 4):
                keep = row >= d
                sa = jnp.where(keep, pltpu.roll(a, d, 0), 1.0)
                sb = jnp.where(keep, pltpu.roll(b, d, 0), 0.0)
                b = b + a * sb
                a = a * sa
            hh = b + a * carry
            b_ref[pl.ds(r0, V7X_SUBLANES), :] = hh
            return jnp.broadcast_to(hh[V7X_SUBLANES - 1:V7X_SUBLANES, :], (V7X_SUBLANES, hd))

        carry = lax.fori_loop(0, tc // V7X_SUBLANES, group, carry_ref[:, cols], unroll=4)
        carry_ref[:, cols] = carry
        o_ref[:, cols] = (_gelu(gate_ref[:, cols].astype(F32)) * b_ref[...]).astype(o_ref.dtype)


def lru_mixer(z, conv_w, conv_b, wa, ba, wx, bx, lam, bsz, seq):
    t = z.shape[0]
    heads, hd, _ = wa.shape
    w = heads * hd
    kconv = conv_w.shape[0]
    tc = _tile(seq, 256)
    ns = seq // tc
    wax = jnp.concatenate([wa, wx], axis=-1).astype(BF16)
    bax = jnp.concatenate([ba.reshape(heads, 1, hd), bx.reshape(heads, 1, hd)], axis=-1)
    return pl.pallas_call(
        functools.partial(_lru_kernel, heads=heads, hd=hd, tc=tc, kconv=kconv),
        grid=(bsz, ns),
        in_specs=[pl.BlockSpec((tc, w), lambda b, s: (b * ns + s, 0)),
                  pl.BlockSpec((tc, w), lambda b, s: (b * ns + s, 1)),
                  pl.BlockSpec((kconv, w), lambda b, s: (0, 0)),
                  pl.BlockSpec((1, w), lambda b, s: (0, 0)),
                  pl.BlockSpec((heads, hd, 2 * hd), lambda b, s: (0, 0, 0)),
                  pl.BlockSpec((heads, 1, 2 * hd), lambda b, s: (0, 0, 0)),
                  pl.BlockSpec((1, w), lambda b, s: (0, 0))],
        out_specs=pl.BlockSpec((tc, w), lambda b, s: (b * ns + s, 0)),
        out_shape=jax.ShapeDtypeStruct((t, w), BF16),
        scratch_shapes=[pltpu.VMEM((tc + 2 * V7X_SUBLANES, w), F32),
                        pltpu.VMEM((tc, hd), F32),
                        pltpu.VMEM((tc, hd), F32),
                        pltpu.VMEM((V7X_SUBLANES, w), F32)],
        compiler_params=_cparams(("arbitrary", "arbitrary")),
        name="lru_mixer",
    )(z, z, conv_w, conv_b.reshape(1, w), wax, bax, lam.reshape(1, w))


S5_GROUPS_PER_BLOCK = 8


def _s5_kernel(u_ref, lre_ref, lim_ref, ldt_ref, bre_ref, bim_ref, cre_ref, cim_ref, d_ref, gw_ref, gb_ref,
               o_ref, wb_ref, am_ref, p_ref, carry_ref, re_ref, im_ref, y_ref, *, tc, nblk, cin, nst, lane_chunk):
    s = pl.program_id(1)
    gn = nblk * nst
    sub = V7X_SUBLANES

    @pl.when(s == 0)
    def _():
        lr = lre_ref[...]
        li = lim_ref[...]
        dt = jnp.exp(ldt_ref[...])
        mag = jnp.exp(lr * dt)
        ar = mag * jnp.cos(li * dt)
        ai = mag * jnp.sin(li * dt)
        den = lr * lr + li * li
        zr = ar - 1.0
        cr = (zr * lr + ai * li) / den
        ci = (ai * lr - zr * li) / den
        for j in range(nblk):
            cols = slice(j * nst, (j + 1) * nst)
            br = bre_ref[j]
            bi = bim_ref[j]
            wb_ref[j, :, 0:nst] = (cr[:, cols] * br - ci[:, cols] * bi).astype(BF16)
            wb_ref[j, :, nst:2 * nst] = (cr[:, cols] * bi + ci[:, cols] * br).astype(BF16)
        row = lax.broadcasted_iota(jnp.int32, (sub, gn), 0)
        pr, pi = ar, ai
        accr = jnp.zeros((sub, gn), F32)
        acci = jnp.zeros((sub, gn), F32)
        powers = {}
        for r in range(sub):
            powers[r + 1] = (pr, pi)
            accr = jnp.where(row == r, pr, accr)
            acci = jnp.where(row == r, pi, acci)
            pr, pi = pr * ar - pi * ai, pr * ai + pi * ar
        p_ref[0] = accr
        p_ref[1] = acci
        for idx, dd in enumerate((1, 2, 4)):
            qr, qi = powers[dd]
            am_ref[2 * idx] = jnp.where(row >= dd, qr, 0.0)
            am_ref[2 * idx + 1] = jnp.where(row >= dd, qi, 0.0)
        carry_ref[...] = jnp.zeros_like(carry_ref)

    u = u_ref[...].astype(F32)
    ub = u.astype(BF16)
    for j in range(nblk):
        bu = jnp.dot(ub[:, j * cin:(j + 1) * cin], wb_ref[j], preferred_element_type=F32)
        re_ref[:, j * nst:(j + 1) * nst] = bu[:, :nst]
        im_ref[:, j * nst:(j + 1) * nst] = bu[:, nst:]

    for c in range(gn // lane_chunk):
        lsl = slice(c * lane_chunk, (c + 1) * lane_chunk)

        def group(g, carry, lsl=lsl):
            cr_, ci_ = carry
            r0 = pl.multiple_of(g * sub, sub)
            rows = pl.ds(r0, sub)
            xr = re_ref[rows, lsl]
            xi = im_ref[rows, lsl]
            for idx, dd in enumerate((1, 2, 4)):
                mr = am_ref[2 * idx, :, lsl]
                mi = am_ref[2 * idx + 1, :, lsl]
                sr = pltpu.roll(xr, dd, 0)
                si = pltpu.roll(xi, dd, 0)
                xr, xi = xr + (mr * sr - mi * si), xi + (mr * si + mi * sr)
            pr_ = p_ref[0, :, lsl]
            pi_ = p_ref[1, :, lsl]
            hr = xr + (pr_ * cr_ - pi_ * ci_)
            hi = xi + (pr_ * ci_ + pi_ * cr_)
            re_ref[rows, lsl] = hr
            im_ref[rows, lsl] = hi
            return (jnp.broadcast_to(hr[sub - 1:sub, :], (sub, lane_chunk)),
                    jnp.broadcast_to(hi[sub - 1:sub, :], (sub, lane_chunk)))

        cr_, ci_ = lax.fori_loop(0, tc // sub, group, (carry_ref[0, :, lsl], carry_ref[1, :, lsl]), unroll=2)
        carry_ref[0, :, lsl] = cr_
        carry_ref[1, :, lsl] = ci_

    for j in range(nblk):
        cols = slice(j * nst, (j + 1) * nst)
        yj = (jnp.dot(re_ref[:, cols].astype(BF16), cre_ref[j], preferred_element_type=F32)
              - jnp.dot(im_ref[:, cols].astype(BF16), cim_ref[j], preferred_element_type=F32))
        ucols = slice(j * cin, (j + 1) * cin)
        y_ref[:, ucols] = yj + d_ref[:, ucols] * u[:, ucols]
    g = _gelu(y_ref[...])
    gate = _sigmoid(jnp.dot(g.astype(BF16), gw_ref[...], preferred_element_type=F32) + gb_ref[...])
    o_ref[...] = (g * gate).astype(o_ref.dtype)


def s5_mixer(u, lam_re, lam_im, log_dt, b_re, b_im, c_re, c_im, d, glu_w, glu_b, bsz, seq):
    t, ws = u.shape
    groups, nstate, gch = b_re.shape
    gpb = S5_GROUPS_PER_BLOCK
    nblk = groups // gpb
    cin = gpb * gch
    nst = gpb * nstate
    gn = groups * nstate
    tc = _tile(seq, 256)
    ns = seq // tc
    lane_chunk = _tile(gn, 512)
    eye = jnp.eye(gpb, dtype=F32)

    def bdiag_in(b):
        bb = b.reshape(nblk, gpb, nstate, gch).transpose(0, 1, 3, 2)
        return (bb[:, :, :, None, :] * eye[None, :, None, :, None]).reshape(nblk, cin, nst)

    def bdiag_out(c):
        cc = c.reshape(nblk, gpb, gch, nstate).transpose(0, 1, 3, 2)
        return (cc[:, :, :, None, :] * eye[None, :, None, :, None]).reshape(nblk, nst, cin)

    const2 = lambda b, s: (0, 0)
    const3 = lambda b, s: (0, 0, 0)
    return pl.pallas_call(
        functools.partial(_s5_kernel, tc=tc, nblk=nblk, cin=cin, nst=nst, lane_chunk=lane_chunk),
        grid=(bsz, ns),
        in_specs=[pl.BlockSpec((tc, ws), lambda b, s: (b * ns + s, 0)),
                  pl.BlockSpec((1, gn), const2), pl.BlockSpec((1, gn), const2), pl.BlockSpec((1, gn), const2),
                  pl.BlockSpec((nblk, cin, nst), const3), pl.BlockSpec((nblk, cin, nst), const3),
                  pl.BlockSpec((nblk, nst, cin), const3), pl.BlockSpec((nblk, nst, cin), const3),
                  pl.BlockSpec((1, ws), const2),
                  pl.BlockSpec((ws, ws), const2),
                  pl.BlockSpec((1, ws), const2)],
        out_specs=pl.BlockSpec((tc, ws), lambda b, s: (b * ns + s, 0)),
        out_shape=jax.ShapeDtypeStruct((t, ws), BF16),
        scratch_shapes=[pltpu.VMEM((nblk, cin, 2 * nst), BF16),
                        pltpu.VMEM((6, V7X_SUBLANES, gn), F32),
                        pltpu.VMEM((2, V7X_SUBLANES, gn), F32),
                        pltpu.VMEM((2, V7X_SUBLANES, gn), F32),
                        pltpu.VMEM((tc, gn), F32),
                        pltpu.VMEM((tc, gn), F32),
                        pltpu.VMEM((tc, ws), F32)],
        compiler_params=_cparams(("arbitrary", "arbitrary")),
        name="s5_mixer",
    )(u, lam_re.reshape(1, gn), lam_im.reshape(1, gn),
      jnp.broadcast_to(log_dt[:, None], (groups, nstate)).reshape(1, gn),
      bdiag_in(b_re), bdiag_in(b_im), bdiag_out(c_re).astype(BF16), bdiag_out(c_im).astype(BF16),
      d.reshape(1, ws), glu_w.astype(BF16), glu_b.reshape(1, ws))


def _sgu_kernel(u_ref, v_ref, g_ref, b_ref, w_ref, bs_ref, o_ref, *, heads, hd, chunk, nchunk):
    v = _gelu(v_ref[...].astype(F32))
    mu = jnp.mean(v, axis=-1, keepdims=True)
    vc = v - mu
    var = jnp.mean(vc * vc, axis=-1, keepdims=True)
    vn = (vc * lax.rsqrt(var + EPS) * g_ref[...] + b_ref[...]).astype(BF16)
    r = lax.broadcasted_iota(jnp.int32, (chunk, chunk), 0)
    c = lax.broadcasted_iota(jnp.int32, (chunk, chunk), 1)
    tril = r >= c
    for h in range(heads):
        wh = jnp.where(tril, w_ref[h], 0.0).astype(BF16)
        cols = slice(h * hd, (h + 1) * hd)
        for n in range(nchunk):
            rows = slice(n * chunk, (n + 1) * chunk)
            gsp = jnp.dot(wh, vn[rows, cols], preferred_element_type=F32) + bs_ref[h]
            o_ref[rows, cols] = (_gelu(u_ref[rows, cols].astype(F32)) * gsp).astype(o_ref.dtype)


def sgu_mixer(z, ln_g, ln_b, w_s, b_s):
    t = z.shape[0]
    heads, chunk, _ = w_s.shape
    w = ln_g.shape[0]
    hd = w // heads
    nchunk = 2 if (t // chunk) % 2 == 0 else 1
    tm = nchunk * chunk
    bs = jnp.broadcast_to(b_s[:, :, None], (heads, chunk, hd))
    return pl.pallas_call(
        functools.partial(_sgu_kernel, heads=heads, hd=hd, chunk=chunk, nchunk=nchunk),
        grid=(t // tm,),
        in_specs=[pl.BlockSpec((tm, w), lambda i: (i, 0)),
                  pl.BlockSpec((tm, w), lambda i: (i, 1)),
                  pl.BlockSpec((1, w), lambda i: (0, 0)),
                  pl.BlockSpec((1, w), lambda i: (0, 0)),
                  pl.BlockSpec((heads, chunk, chunk), lambda i: (0, 0, 0)),
                  pl.BlockSpec((heads, chunk, hd), lambda i: (0, 0, 0))],
        out_specs=pl.BlockSpec((tm, w), lambda i: (i, 0)),
        out_shape=jax.ShapeDtypeStruct((t, w), BF16),
        compiler_params=_cparams(("arbitrary",)),
        name="sgu_mixer",
    )(z, z, ln_g.reshape(1, w), ln_b.reshape(1, w), w_s, bs)


def _qk_norm_kernel(x_ref, g_ref, o_ref, *, nseg, seg):
    x = x_ref[...].astype(F32)
    for i in range(nseg):
        cols = slice(i * seg, (i + 1) * seg)
        xs = x[:, cols]
        ms = jnp.mean(xs * xs, axis=-1, keepdims=True)
        o_ref[:, cols] = (xs * lax.rsqrt(ms + EPS) * g_ref[:, cols]).astype(o_ref.dtype)


def qk_norm(z, col_block, width, gain_row, seg):
    t = z.shape[0]
    tm = _tile(t, 512)
    return pl.pallas_call(
        functools.partial(_qk_norm_kernel, nseg=width // seg, seg=seg),
        grid=(t // tm,),
        in_specs=[pl.BlockSpec((tm, width), lambda i: (i, col_block)),
                  pl.BlockSpec((1, width), lambda i: (0, 0))],
        out_specs=pl.BlockSpec((tm, width), lambda i: (i, 0)),
        out_shape=jax.ShapeDtypeStruct((t, width), BF16),
        compiler_params=_cparams(("arbitrary",)),
        name="qk_norm",
    )(z, gain_row)


def _attn_kernel(q_ref, k_ref, v_ref, bias_ref, lq1_ref, lk1_ref, lq2_ref, lk2_ref, sg_ref, o_ref,
                 acc0_ref, acc1_ref, m0_ref, m1_ref, l0_ref, l1_ref, *, tq, dqk, lam_init):
    qi = pl.program_id(2)
    accs, ms, ls = (acc0_ref, acc1_ref), (m0_ref, m1_ref), (l0_ref, l1_ref)
    for mp in range(2):
        ms[mp][...] = jnp.full_like(ms[mp], NEG_INF)
        ls[mp][...] = jnp.zeros_like(ls[mp])
        accs[mp][...] = jnp.zeros_like(accs[mp])
    r = lax.broadcasted_iota(jnp.int32, (tq, tq), 0)
    c = lax.broadcasted_iota(jnp.int32, (tq, tq), 1)
    causal = r >= c
    krep = tq // V7X_LANES
    vrep = accs[0].shape[1] // V7X_LANES

    def block(kb, bias_idx, masked):
        k0 = pl.multiple_of(kb * tq, tq)
        v = v_ref[pl.ds(k0, tq), :]
        for mp in range(2):
            q = q_ref[:, mp * dqk:(mp + 1) * dqk]
            k = k_ref[pl.ds(k0, tq), mp * dqk:(mp + 1) * dqk]
            sc = lax.dot_general(q, k, (((1,), (1,)), ((), ())), preferred_element_type=F32)
            if bias_idx is not None:
                sc = sc + bias_ref[bias_idx]
            if masked:
                sc = jnp.where(causal, sc, NEG_INF)
            m_old = ms[mp][...]
            m_new = jnp.maximum(m_old, jnp.max(sc, axis=-1, keepdims=True))
            alpha = jnp.exp2(m_old - m_new)
            p = jnp.exp2(sc - jnp.tile(m_new, (1, krep)))
            ls[mp][...] = alpha * ls[mp][...] + jnp.sum(p, axis=-1, keepdims=True)
            accs[mp][...] = (jnp.tile(alpha, (1, vrep)) * accs[mp][...]
                             + jnp.dot(p.astype(BF16), v, preferred_element_type=F32))
            ms[mp][...] = m_new

    def far(kb, _):
        block(kb, None, False)
        return 0

    lax.fori_loop(0, jnp.maximum(qi - 1, 0), far, 0)

    @pl.when(qi > 0)
    def _():
        block(qi - 1, 1, False)

    block(qi, 0, True)

    lam = (jnp.exp(jnp.sum(lq1_ref[...] * lk1_ref[...], axis=-1, keepdims=True))
           - jnp.exp(jnp.sum(lq2_ref[...] * lk2_ref[...], axis=-1, keepdims=True)) + lam_init)
    o = (accs[0][...] / jnp.tile(ls[0][...], (1, vrep))
         - lam * (accs[1][...] / jnp.tile(ls[1][...], (1, vrep))))
    ms_o = jnp.mean(o * o, axis=-1, keepdims=True)
    o_ref[...] = (o * lax.rsqrt(ms_o + EPS) * sg_ref[...] * (1.0 - lam_init)).astype(o_ref.dtype)


def _t5_bucket(n, buckets):
    max_exact = buckets // 2
    nf = jnp.maximum(n, 1).astype(F32)
    large = max_exact + (jnp.log(nf / max_exact) / math.log(REL_MAX_DIST / max_exact)
                         * (buckets - max_exact)).astype(jnp.int32)
    large = jnp.minimum(large, buckets - 1)
    return jnp.where(n < max_exact, n, large)


def diff_attention(qn, kn, zqkv, v_col_block, rel_bias, lq1, lk1, lq2, lk2, sub_g, bsz, seq, layer):
    t = qn.shape[0]
    buckets, heads = rel_bias.shape
    dv = sub_g.shape[0]
    dqk = dv // 2
    tq = _tile(seq, 512)
    assert tq >= REL_MAX_DIST, "far key blocks must all fall in the last relative-position bucket"
    nq = seq // tq
    lam_init = 0.8 - 0.6 * math.exp(-0.3 * layer)
    idx = jnp.arange(tq)
    dist = idx[:, None] - idx[None, :]
    table = rel_bias.astype(F32)
    far_const = table[_t5_bucket(jnp.int32(2 * tq), buckets)]
    tiles = jnp.stack([table[_t5_bucket(jnp.maximum(dist, 0), buckets)],
                       table[_t5_bucket(dist + tq, buckets)]], axis=0)
    bias = jnp.transpose(tiles - far_const, (3, 0, 1, 2)) * LOG2E
    row = lambda b, h, i: (0, 0)
    return pl.pallas_call(
        functools.partial(_attn_kernel, tq=tq, dqk=dqk, lam_init=lam_init),
        grid=(bsz, heads, nq),
        in_specs=[pl.BlockSpec((tq, dv), lambda b, h, i: (b * nq + i, h)),
                  pl.BlockSpec((seq, dv), lambda b, h, i: (b, h)),
                  pl.BlockSpec((seq, dv), lambda b, h, i: (b, v_col_block + h)),
                  pl.BlockSpec((None, 2, tq, tq), lambda b, h, i: (h, 0, 0, 0)),
                  pl.BlockSpec((1, dqk), row), pl.BlockSpec((1, dqk), row),
                  pl.BlockSpec((1, dqk), row), pl.BlockSpec((1, dqk), row),
                  pl.BlockSpec((1, dv), row)],
        out_specs=pl.BlockSpec((tq, dv), lambda b, h, i: (b * nq + i, h)),
        out_shape=jax.ShapeDtypeStruct((t, heads * dv), BF16),
        scratch_shapes=[pltpu.VMEM((tq, dv), F32), pltpu.VMEM((tq, dv), F32),
                        pltpu.VMEM((tq, V7X_LANES), F32), pltpu.VMEM((tq, V7X_LANES), F32),
                        pltpu.VMEM((tq, V7X_LANES), F32), pltpu.VMEM((tq, V7X_LANES), F32)],
        compiler_params=_cparams(("arbitrary", "arbitrary", "arbitrary")),
        name="diff_attention",
    )(qn, kn, zqkv, bias, lq1.reshape(1, dqk), lk1.reshape(1, dqk), lq2.reshape(1, dqk), lk2.reshape(1, dqk),
      sub_g.reshape(1, dv))


ROUTER_LANES = 128


def _norm_mod(x, g_ref, sc_ref, sh_ref):
    ms = jnp.mean(x * x, axis=-1, keepdims=True)
    return (x * lax.rsqrt(ms + EPS) * g_ref[...]) * (1.0 + sc_ref[...]) + sh_ref[...]


def _router_kernel(x_ref, g_ref, sc_ref, sh_ref, wr_ref, wlo_ref, br_ref, route_ref, *, ngroups, per_group):
    h = _norm_mod(x_ref[...], g_ref, sc_ref, sh_ref)
    hi = h.astype(BF16)
    lo = (h - hi.astype(F32)).astype(BF16)
    both = jnp.dot(hi, wr_ref[...], preferred_element_type=F32)
    nl = ROUTER_LANES
    logits = both[:, :nl] + both[:, nl:] + jnp.dot(lo, wlo_ref[...], preferred_element_type=F32) + br_ref[...]
    lane = lax.broadcasted_iota(jnp.int32, logits.shape, 1).astype(F32)
    big = float(nl)
    glog = jnp.where(lane < ngroups, logits, NEG_INF)
    gmax = jnp.max(glog, axis=-1, keepdims=True)
    gsum = jnp.sum(jnp.exp(glog - gmax), axis=-1, keepdims=True)
    gp = 1.0 / gsum
    gidx = jnp.min(jnp.where(glog == gmax, lane, big), axis=-1, keepdims=True)
    lo_lane = ngroups + gidx * per_group
    emask = (lane >= lo_lane) & (lane < lo_lane + per_group)
    elog = jnp.where(emask, logits, NEG_INF)
    emax = jnp.max(elog, axis=-1, keepdims=True)
    eexp = jnp.where(emask, jnp.exp(elog - emax), -1.0)
    i0 = jnp.min(jnp.where(eexp == 1.0, lane, big), axis=-1, keepdims=True)
    rest = jnp.where(lane == i0, -1.0, eexp)
    p1 = jnp.max(rest, axis=-1, keepdims=True)
    i1 = jnp.min(jnp.where(rest == p1, lane, big), axis=-1, keepdims=True)
    denom = 1.0 + p1
    w0 = gp * (1.0 / denom)
    w1 = gp * (p1 / denom)
    e0 = i0 - ngroups
    e1 = i1 - ngroups
    route_ref[...] = jnp.where(lane == 0, e0, jnp.where(lane == 1, e1, jnp.where(lane == 2, w0,
                               jnp.where(lane == 3, w1, 0.0))))


def moe_router(x, g, sc, sh, wg, bg, we, be, seq):
    t, d = x.shape
    bsz = sc.shape[0]
    ngroups = wg.shape[-1]
    per_group = we.shape[-1]
    nexp = ngroups * per_group
    nl = ROUTER_LANES
    assert ngroups + nexp <= nl
    wr = jnp.concatenate([wg, jnp.transpose(we, (1, 0, 2)).reshape(d, nexp)], axis=-1)
    wr = jnp.zeros((d, nl), F32).at[:, :ngroups + nexp].set(wr)
    w_hi = wr.astype(BF16)
    w_lo = (wr - w_hi.astype(F32)).astype(BF16)
    br = jnp.zeros((1, nl), F32).at[0, :ngroups + nexp].set(jnp.concatenate([bg, be.reshape(nexp)]))
    tm = _tile(seq, 256)
    per = seq // tm
    return pl.pallas_call(
        functools.partial(_router_kernel, ngroups=ngroups, per_group=per_group),
        grid=(t // tm,),
        in_specs=[pl.BlockSpec((tm, d), lambda i: (i, 0)),
                  pl.BlockSpec((1, d), lambda i: (0, 0)),
                  pl.BlockSpec((None, 1, d), lambda i: (i // per, 0, 0)),
                  pl.BlockSpec((None, 1, d), lambda i: (i // per, 0, 0)),
                  pl.BlockSpec((d, 2 * nl), lambda i: (0, 0)),
                  pl.BlockSpec((d, nl), lambda i: (0, 0)),
                  pl.BlockSpec((1, nl), lambda i: (0, 0))],
        out_specs=pl.BlockSpec((tm, nl), lambda i: (i, 0)),
        out_shape=jax.ShapeDtypeStruct((t, nl), F32),
        compiler_params=_cparams(("arbitrary",)),
        name="moe_router",
    )(x, g.reshape(1, d), sc.reshape(bsz, 1, d), sh.reshape(bsz, 1, d),
      jnp.concatenate([w_hi, w_lo], axis=-1), w_hi, br)


HI16 = 0xFFFF0000


def _row_wait(src, dst, sem, count):
    def body(r, _):
        pltpu.make_async_copy(src, dst, sem).wait()
        return 0
    lax.fori_loop(0, count, body, 0)


def _dispatch_kernel(pos_ref, zf_ref, x_ref, g_ref, sc_ref, sh_ref, xs_hbm, buf, zbuf, sem, zsem,
                     *, tm, tme, ntok, ntile, topk, half):
    i = pl.program_id(0)
    nsteps = pl.num_programs(0)
    slot = lax.rem(i, 2)

    @pl.when(i == 0)
    def _():
        zbuf[...] = jnp.zeros_like(zbuf)

        def zcopy(tl):
            return pltpu.make_async_copy(zbuf, xs_hbm.at[pl.ds(tl * tme, tme)], zsem)

        def zstart(tl, _):
            @pl.when(zf_ref[tl] > 0)
            def _():
                zcopy(tl).start()
            return 0

        def zwait(tl, _):
            @pl.when(zf_ref[tl] > 0)
            def _():
                zcopy(tl).wait()
            return 0

        lax.fori_loop(0, ntile, zstart, 0)
        lax.fori_loop(0, ntile, zwait, 0)

    h = _norm_mod(x_ref[...], g_ref, sc_ref, sh_ref)
    bits = pltpu.bitcast(h.astype(BF16).astype(F32), jnp.uint32)
    buf[slot] = (bits[:, half:] & jnp.uint32(HI16)) | (bits[:, :half] >> 16)

    base = i * tm

    def issue(r, _):
        for kk in range(topk):
            pltpu.make_async_copy(buf.at[slot, pl.ds(r, 1)],
                                  xs_hbm.at[pl.ds(pos_ref[kk * ntok + base + r], 1)], sem.at[slot]).start()
        return 0

    lax.fori_loop(0, tm, issue, 0, unroll=4)

    def drain(s):
        _row_wait(buf.at[s, pl.ds(0, 1)], xs_hbm.at[pl.ds(0, 1)], sem.at[s], topk * tm)

    @pl.when(i > 0)
    def _():
        drain(1 - slot)

    @pl.when(i == nsteps - 1)
    def _():
        drain(slot)


def _expert_kernel(texp_ref, tvalid_ref, xidx_ref, xs_ref, wg_ref, wu_ref, wd_ref, y_ref, *, half):
    i = pl.program_id(0)

    @pl.when(tvalid_ref[i] > 0)
    def _():
        xp = xs_ref[...]
        x_lo = pltpu.bitcast(xp << 16, F32).astype(BF16)
        x_hi = pltpu.bitcast(xp & jnp.uint32(HI16), F32).astype(BF16)
        hg = (jnp.dot(x_lo, wg_ref[0:half, :], preferred_element_type=F32)
              + jnp.dot(x_hi, wg_ref[half:, :], preferred_element_type=F32))
        hu = (jnp.dot(x_lo, wu_ref[0:half, :], preferred_element_type=F32)
              + jnp.dot(x_hi, wu_ref[half:, :], preferred_element_type=F32))
        act = (hg * _sigmoid(hg)) * hu
        y_ref[...] = jnp.dot(act.astype(BF16), wd_ref[...], preferred_element_type=F32)

    @pl.when(tvalid_ref[i] == 0)
    def _():
        y_ref[...] = jnp.zeros_like(y_ref)


def _combine_kernel(pos_ref, x_ref, route_ref, g_ref, y_hbm, o_ref, ybuf, sem, *, tm, ntok, topk):
    i = pl.program_id(0)
    nsteps = pl.num_programs(0)
    slot = lax.rem(i, 2)

    def gather(step, s):
        base = step * tm

        def issue(r, _):
            for kk in range(topk):
                pltpu.make_async_copy(y_hbm.at[pl.ds(pos_ref[kk * ntok + base + r], 1)],
                                      ybuf.at[s, kk, pl.ds(r, 1)], sem.at[s]).start()
            return 0

        lax.fori_loop(0, tm, issue, 0, unroll=4)

    @pl.when(i == 0)
    def _():
        gather(0, 0)

    @pl.when(i + 1 < nsteps)
    def _():
        gather(i + 1, 1 - slot)

    _row_wait(y_hbm.at[pl.ds(0, 1)], ybuf.at[slot, 0, pl.ds(0, 1)], sem.at[slot], topk * tm)
    w = route_ref[...]
    y = w[:, topk:topk + 1] * ybuf[slot, 0]
    for kk in range(1, topk):
        y = y + w[:, topk + kk:topk + kk + 1] * ybuf[slot, kk]
    o_ref[...] = x_ref[...] + g_ref[...] * y


def moe_layer(x, route, g, sc, sh, gate, w_gate, w_up, w_down, seq):
    t, d = x.shape
    bsz = sc.shape[0]
    nexp, _, f = w_gate.shape
    topk = MOE_TOPK
    half = d // 2
    npair = t * topk
    tme = _tile(npair // nexp, 256) if npair // nexp >= 8 else 8
    ntile = npair // tme + nexp
    tm = _tile(seq, 256)
    per = seq // tm

    flat_e = jnp.transpose(route[:, 0:topk]).astype(jnp.int32).reshape(npair)
    onehot = (flat_e[:, None] == jnp.arange(nexp, dtype=jnp.int32)[None, :]).astype(jnp.int32)
    csum = jnp.cumsum(onehot, axis=0)
    counts = csum[-1]
    tiles_per = (counts + tme - 1) // tme
    tile_end = jnp.cumsum(tiles_per)
    tile_start = tile_end - tiles_per
    pos = jnp.sum(onehot * (tile_start[None, :] * tme + csum - 1), axis=1).astype(jnp.int32)
    n_used = tile_end[-1]
    tile_id = jnp.arange(ntile, dtype=jnp.int32)
    active = tile_id < n_used
    texp = jnp.minimum(jnp.sum((tile_id[:, None] >= tile_end[None, :]).astype(jnp.int32), axis=1), nexp - 1)
    tvalid = jnp.where(active, jnp.clip(counts[texp] - (tile_id - tile_start[texp]) * tme, 0, tme), 0)
    tvalid = tvalid.astype(jnp.int32)
    texp = jnp.where(active, texp, texp[jnp.maximum(n_used - 1, 0)]).astype(jnp.int32)
    xidx = jnp.minimum(tile_id, jnp.maximum(n_used - 1, 0)).astype(jnp.int32)
    zflag = (tvalid < tme).astype(jnp.int32)

    mod_specs = [pl.BlockSpec((1, d), lambda i, *_: (0, 0)),
                 pl.BlockSpec((None, 1, d), lambda i, *_: (i // per, 0, 0)),
                 pl.BlockSpec((None, 1, d), lambda i, *_: (i // per, 0, 0))]
    xs = pl.pallas_call(
        functools.partial(_dispatch_kernel, tm=tm, tme=tme, ntok=t, ntile=ntile, topk=topk, half=half),
        grid_spec=pltpu.PrefetchScalarGridSpec(
            num_scalar_prefetch=2,
            grid=(t // tm,),
            in_specs=[pl.BlockSpec((tm, d), lambda i, *_: (i, 0))] + mod_specs,
            out_specs=pl.BlockSpec(memory_space=pl.ANY),
            scratch_shapes=[pltpu.VMEM((2, tm, half), jnp.uint32), pltpu.VMEM((tme, half), jnp.uint32),
                            pltpu.SemaphoreType.DMA((2,)), pltpu.SemaphoreType.DMA(())]),
        out_shape=jax.ShapeDtypeStruct((ntile * tme, half), jnp.uint32),
        compiler_params=_cparams(("arbitrary",)),
        name="moe_dispatch",
    )(pos, zflag, x, g.reshape(1, d), sc.reshape(bsz, 1, d), sh.reshape(bsz, 1, d))

    y = pl.pallas_call(
        functools.partial(_expert_kernel, half=half),
        grid_spec=pltpu.PrefetchScalarGridSpec(
            num_scalar_prefetch=3,
            grid=(ntile,),
            in_specs=[pl.BlockSpec((tme, half), lambda i, te, tv, xi: (xi[i], 0)),
                      pl.BlockSpec((None, d, f), lambda i, te, tv, xi: (te[i], 0, 0)),
                      pl.BlockSpec((None, d, f), lambda i, te, tv, xi: (te[i], 0, 0)),
                      pl.BlockSpec((None, f, d), lambda i, te, tv, xi: (te[i], 0, 0))],
            out_specs=pl.BlockSpec((tme, d), lambda i, te, tv, xi: (i, 0))),
        out_shape=jax.ShapeDtypeStruct((ntile * tme, d), F32),
        compiler_params=_cparams(("arbitrary",)),
        name="moe_experts",
    )(texp, tvalid, xidx, xs, w_gate, w_up, w_down)

    return pl.pallas_call(
        functools.partial(_combine_kernel, tm=tm, ntok=t, topk=topk),
        grid_spec=pltpu.PrefetchScalarGridSpec(
            num_scalar_prefetch=1,
            grid=(t // tm,),
            in_specs=[pl.BlockSpec((tm, d), lambda i, *_: (i, 0)),
                      pl.BlockSpec((tm, ROUTER_LANES), lambda i, *_: (i, 0)),
                      pl.BlockSpec((None, 1, d), lambda i, *_: (i // per, 0, 0)),
                      pl.BlockSpec(memory_space=pl.ANY)],
            out_specs=pl.BlockSpec((tm, d), lambda i, *_: (i, 0)),
            scratch_shapes=[pltpu.VMEM((2, topk, tm, d), F32), pltpu.SemaphoreType.DMA((2,))]),
        out_shape=jax.ShapeDtypeStruct((t, d), F32),
        compiler_params=_cparams(("arbitrary",)),
        name="moe_combine",
    )(pos, x, route, gate.reshape(bsz, 1, d), y)


def kernel(x, c, norm1_g, norm2_g, ada_w, ada_b, ab_w_in, ab_w_out, lru_conv_w, lru_conv_b, lru_wa, lru_ba, lru_wx, lru_bx, lru_lambda, s5_lambda_re, s5_lambda_im, s5_log_dt, s5_b_re, s5_b_im, s5_c_re, s5_c_im, s5_d, s5_glu_w, s5_glu_b, cd_w_in, cd_w_out, sg_ln_g, sg_ln_b, sg_w, sg_b, da_q_norm, da_k_norm, da_lq1, da_lk1, da_lq2, da_lk2, da_sub_g, rel_bias, moe_wg, moe_bg, moe_we, moe_be, moe_w_gate, moe_w_up, moe_w_down):
    bsz, seq, d = x.shape
    depth = norm1_g.shape[0]
    t = bsz * seq
    xt = x.reshape(t, d)
    mod = ada_modulation(c, ada_w, ada_b)

    for layer in range(depth):
        sh1, sc1, g1, sh2, sc2, g2 = [mod[layer, :, i * d:(i + 1) * d] for i in range(6)]
        hmix = norm_modulate(xt, norm1_g[layer], sc1, sh1, seq)
        j = layer // 2
        if layer % 2 == 0:
            lw = lru_conv_w.shape[-1]
            w_in = ab_w_in[j].astype(BF16)
            w_out = ab_w_out[j].astype(BF16)
            z_lru = matmul([hmix], w_in, F32, col_off=0, ncols=2 * lw)
            z_s5 = matmul([hmix], w_in, F32, col_off=2 * lw)
            y_a = lru_mixer(z_lru, lru_conv_w[j], lru_conv_b[j], lru_wa[j], lru_ba[j], lru_wx[j], lru_bx[j],
                            lru_lambda[j], bsz, seq)
            y_b = s5_mixer(z_s5, s5_lambda_re[j], s5_lambda_im[j], s5_log_dt[j], s5_b_re[j], s5_b_im[j],
                           s5_c_re[j], s5_c_im[j], s5_d[j], s5_glu_w[j], s5_glu_b[j], bsz, seq)
            xt = matmul([y_a, y_b], w_out, F32, res=xt, gate=g1, seq=seq)
        else:
            sgw = sg_ln_g.shape[-1]
            dqk = da_q_norm.shape[-1]
            dv = da_sub_g.shape[-1]
            heads = rel_bias.shape[1]
            daw = heads * dv
            w_in = cd_w_in[j].astype(BF16)
            w_out = cd_w_out[j].astype(BF16)
            z_sg = matmul([hmix], w_in, BF16, col_off=0, ncols=2 * sgw)
            z_qkv = matmul([hmix], w_in, BF16, col_off=2 * sgw)
            y_c = sgu_mixer(z_sg, sg_ln_g[j], sg_ln_b[j], sg_w[j], sg_b[j])
            q_gain = jnp.tile(da_q_norm[j] * (dqk ** -0.5 * LOG2E), daw // dqk).reshape(1, daw)
            k_gain = jnp.tile(da_k_norm[j], daw // dqk).reshape(1, daw)
            qn = qk_norm(z_qkv, 0, daw, q_gain, dqk)
            kn = qk_norm(z_qkv, 1, daw, k_gain, dqk)
            y_d = diff_attention(qn, kn, z_qkv, 2 * daw // dv, rel_bias, da_lq1[j], da_lk1[j], da_lq2[j], da_lk2[j],
                                 da_sub_g[j], bsz, seq, layer)
            xt = matmul([y_c, y_d], w_out, F32, res=xt, gate=g1, seq=seq)
        route = moe_router(xt, norm2_g[layer], sc2, sh2, moe_wg[layer], moe_bg[layer], moe_we[layer], moe_be[layer],
                           seq)
        xt = moe_layer(xt, route, norm2_g[layer], sc2, sh2, g2, moe_w_gate[layer].astype(BF16),
                       moe_w_up[layer].astype(BF16), moe_w_down[layer].astype(BF16), seq)
    return xt.reshape(bsz, seq, d)
```

```python
import functools
import math

import jax
import jax.numpy as jnp
import numpy as np
from jax import lax
from jax.experimental import pallas as pl
from jax.experimental.pallas import tpu as pltpu

F32 = jnp.float32
BF16 = jnp.bfloat16

EPS = 1e-6
LRU_C = 8.0
REL_MAX_DIST = 128
MOE_TOPK = 2
NEG_INF = -1e30
LOG2E = math.log2(math.e)

V7X_LANES = 128
V7X_SUBLANES = 8
V7X_VMEM_LIMIT_BYTES = 56 * 1024 * 1024


def _cparams(semantics):
    return pltpu.CompilerParams(dimension_semantics=semantics, vmem_limit_bytes=V7X_VMEM_LIMIT_BYTES)


def _sigmoid(x):
    return 0.5 * jnp.tanh(0.5 * x) + 0.5


def _gelu(x):
    return 0.5 * x * (1.0 + jnp.tanh(math.sqrt(2.0 / math.pi) * (x + 0.044715 * (x * x * x))))


def _tile(n, want):
    t = min(n, want)
    while n % t:
        t -= 1
    return t


def _ada_kernel(c_ref, w_ref, b_ref, o_ref):
    c = c_ref[...]
    cond = c * _sigmoid(c)
    o_ref[...] = jnp.dot(cond.astype(BF16), w_ref[...].astype(BF16), preferred_element_type=F32) + b_ref[...]


def ada_modulation(c, ada_w, ada_b):
    bsz, d = c.shape
    depth, _, n = ada_w.shape
    rows = 16
    cp = jnp.zeros((rows, d), F32).at[:bsz].set(c)
    tn = _tile(n, 512)
    out = pl.pallas_call(
        _ada_kernel,
        grid=(depth, n // tn),
        in_specs=[pl.BlockSpec((rows, d), lambda l, j: (0, 0)),
                  pl.BlockSpec((None, d, tn), lambda l, j: (l, 0, j)),
                  pl.BlockSpec((None, 1, tn), lambda l, j: (l, 0, j))],
        out_specs=pl.BlockSpec((None, rows, tn), lambda l, j: (l, 0, j)),
        out_shape=jax.ShapeDtypeStruct((depth, rows, n), F32),
        compiler_params=_cparams(("arbitrary", "arbitrary")),
        name="ada_modulation",
    )(cp, ada_w, ada_b.reshape(depth, 1, n))
    return out[:, :bsz]


def _norm_mod_kernel(x_ref, g_ref, sc_ref, sh_ref, o_ref):
    x = x_ref[...]
    ms = jnp.mean(x * x, axis=-1, keepdims=True)
    y = x * lax.rsqrt(ms + EPS) * g_ref[...]
    o_ref[...] = (y * (1.0 + sc_ref[...]) + sh_ref[...]).astype(o_ref.dtype)


def norm_modulate(x, g, sc, sh, seq, out_dtype=BF16):
    t, d = x.shape
    bsz = sc.shape[0]
    tm = _tile(seq, 256)
    per = seq // tm
    return pl.pallas_call(
        _norm_mod_kernel,
        grid=(t // tm,),
        in_specs=[pl.BlockSpec((tm, d), lambda i: (i, 0)),
                  pl.BlockSpec((1, d), lambda i: (0, 0)),
                  pl.BlockSpec((None, 1, d), lambda i: (i // per, 0, 0)),
                  pl.BlockSpec((None, 1, d), lambda i: (i // per, 0, 0))],
        out_specs=pl.BlockSpec((tm, d), lambda i: (i, 0)),
        out_shape=jax.ShapeDtypeStruct((t, d), out_dtype),
        compiler_params=_cparams(("arbitrary",)),
        name="norm_modulate",
    )(x, g.reshape(1, d), sc.reshape(bsz, 1, d), sh.reshape(bsz, 1, d))


def _matmul_kernel(*refs, ksplit, has_res):
    n_lhs = len(ksplit)
    a_refs = refs[:n_lhs]
    w_ref = refs[n_lhs]
    o_ref = refs[-1]
    acc = None
    k0 = 0
    for a_ref, kk in zip(a_refs, ksplit):
        part = jnp.dot(a_ref[...], w_ref[k0:k0 + kk, :], preferred_element_type=F32)
        acc = part if acc is None else acc + part
        k0 += kk
    if has_res:
        res_ref, gate_ref = refs[n_lhs + 1], refs[n_lhs + 2]
        acc = res_ref[...] + gate_ref[...] * acc
    o_ref[...] = acc.astype(o_ref.dtype)


def matmul(lhs, w, out_dtype, col_off=0, ncols=None, res=None, gate=None, seq=None):
    m = lhs[0].shape[0]
    ktot = w.shape[0]
    ksplit = tuple(a.shape[1] for a in lhs)
    assert sum(ksplit) == ktot
    n = w.shape[1] - col_off if ncols is None else ncols
    tm = _tile(m if seq is None else seq, 1024)
    tn = _tile(math.gcd(n, col_off) if col_off else n, 512)
    joff = col_off // tn
    in_specs = [pl.BlockSpec((tm, kk), lambda i, j: (i, 0)) for kk in ksplit]
    in_specs += [pl.BlockSpec((ktot, tn), lambda i, j: (0, j + joff))]
    args = list(lhs) + [w]
    if res is not None:
        per = seq // tm
        bsz = gate.shape[0]
        in_specs += [pl.BlockSpec((tm, tn), lambda i, j: (i, j)),
                     pl.BlockSpec((None, 1, tn), lambda i, j: (i // per, 0, j))]
        args += [res, gate.reshape(bsz, 1, n)]
    return pl.pallas_call(
        functools.partial(_matmul_kernel, ksplit=ksplit, has_res=res is not None),
        grid=(m // tm, n // tn),
        in_specs=in_specs,
        out_specs=pl.BlockSpec((tm, tn), lambda i, j: (i, j)),
        out_shape=jax.ShapeDtypeStruct((m, n), out_dtype),
        compiler_params=_cparams(("arbitrary", "arbitrary")),
        name="matmul",
    )(*args)


def _lru_kernel(x_ref, gate_ref, cw_ref, cb_ref, wax_ref, bax_ref, lam_ref, o_ref,
                xs_ref, a_ref, b_ref, carry_ref, *, heads, hd, tc, kconv):
    s = pl.program_id(1)
    pad = V7X_SUBLANES

    @pl.when(s == 0)
    def _():
        xs_ref[0:pad, :] = jnp.zeros((pad, heads * hd), F32)
        carry_ref[...] = jnp.zeros_like(carry_ref)

    @pl.when(s > 0)
    def _():
        xs_ref[0:pad, :] = xs_ref[tc:tc + pad, :]

    xs_ref[pad:pad + tc, :] = x_ref[...].astype(F32)

    row = lax.broadcasted_iota(jnp.int32, (V7X_SUBLANES, hd), 0)

    for h in range(heads):
        cols = slice(h * hd, (h + 1) * hd)
        xc = cb_ref[:, cols] + cw_ref[0:1, cols] * xs_ref[pl.ds(pad - kconv + 1, tc), cols]
        for k in range(1, kconv):
            xc = xc + cw_ref[k:k + 1, cols] * xs_ref[pl.ds(pad - kconv + 1 + k, tc), cols]
        pre = jnp.dot(xc.astype(BF16), wax_ref[h], preferred_element_type=F32) + bax_ref[h]
        r = _sigmoid(pre[:, :hd])
        gi = _sigmoid(pre[:, hd:])
        nl = -lam_ref[:, cols]
        sp = jnp.maximum(nl, 0.0) + jnp.log1p(jnp.exp(-jnp.abs(nl)))
        log_a = (-LRU_C) * r * sp
        a_ref[...] = jnp.exp(log_a)
        b_ref[...] = jnp.sqrt(1.0 - jnp.exp(2.0 * log_a)) * (gi * xc)

        def group(g, carry):
            r0 = pl.multiple_of(g * V7X_SUBLANES, V7X_SUBLANES)
            a = a_ref[pl.ds(r0, V7X_SUBLANES), :]
            b = b_ref[pl.ds(r0, V7X_SUBLANES), :]
            for d in (1, 2, 4):
                keep = row >= d
                sa = jnp.where(keep, pltpu.roll(a, d, 0), 1.0)
                sb = jnp.where(keep, pltpu.roll(b, d, 0), 0.0)
                b = b + a * sb
                a = a * sa
            hh = b + a * carry
            b_ref[pl.ds(r0, V7X_SUBLANES), :] = hh
            return jnp.broadcast_to(hh[V7X_SUBLANES - 1:V7X_SUBLANES, :], (V7X_SUBLANES, hd))

        carry = lax.fori_loop(0, tc // V7X_SUBLANES, group, carry_ref[:, cols], unroll=4)
        carry_ref[:, cols] = carry
        o_ref[:, cols] = (_gelu(gate_ref[:, cols].astype(F32)) * b_ref[...]).astype(o_ref.dtype)


def lru_mixer(z, conv_w, conv_b, wa, ba, wx, bx, lam, bsz, seq):
    t = z.shape[0]
    heads, hd, _ = wa.shape
    w = heads * hd
    kconv = conv_w.shape[0]
    tc = _tile(seq, 256)
    ns = seq // tc
    wax = jnp.concatenate([wa, wx], axis=-1).astype(BF16)
    bax = jnp.concatenate([ba.reshape(heads, 1, hd), bx.reshape(heads, 1, hd)], axis=-1)
    return pl.pallas_call(
        functools.partial(_lru_kernel, heads=heads, hd=hd, tc=tc, kconv=kconv),
        grid=(bsz, ns),
        in_specs=[pl.BlockSpec((tc, w), lambda b, s: (b * ns + s, 0)),
                  pl.BlockSpec((tc, w), lambda b, s: (b * ns + s, 1)),
                  pl.BlockSpec((kconv, w), lambda b, s: (0, 0)),
                  pl.BlockSpec((1, w), lambda b, s: (0, 0)),
                  pl.BlockSpec((heads, hd, 2 * hd), lambda b, s: (0, 0, 0)),
                  pl.BlockSpec((heads, 1, 2 * hd), lambda b, s: (0, 0, 0)),
                  pl.BlockSpec((1, w), lambda b, s: (0, 0))],
        out_specs=pl.BlockSpec((tc, w), lambda b, s: (b * ns + s, 0)),
        out_shape=jax.ShapeDtypeStruct((t, w), BF16),
        scratch_shapes=[pltpu.VMEM((tc + 2 * V7X_SUBLANES, w), F32),
                        pltpu.VMEM((tc, hd), F32),
                        pltpu.VMEM((tc, hd), F32),
                        pltpu.VMEM((V7X_SUBLANES, w), F32)],
        compiler_params=_cparams(("arbitrary", "arbitrary")),
        name="lru_mixer",
    )(z, z, conv_w, conv_b.reshape(1, w), wax, bax, lam.reshape(1, w))


S5_GROUPS_PER_BLOCK = 8


def _s5_kernel(u_ref, lre_ref, lim_ref, ldt_ref, bre_ref, bim_ref, cre_ref, cim_ref, d_ref, gw_ref, gb_ref,
               o_ref, wb_ref, am_ref, p_ref, carry_ref, re_ref, im_ref, y_ref, *, tc, nblk, cin, nst, lane_chunk):
    s = pl.program_id(1)
    gn = nblk * nst
    sub = V7X_SUBLANES

    @pl.when(s == 0)
    def _():
        lr = lre_ref[...]
        li = lim_ref[...]
        dt = jnp.exp(ldt_ref[...])
        mag = jnp.exp(lr * dt)
        ar = mag * jnp.cos(li * dt)
        ai = mag * jnp.sin(li * dt)
        den = lr * lr + li * li
        zr = ar - 1.0
        cr = (zr * lr + ai * li) / den
        ci = (ai * lr - zr * li) / den
        for j in range(nblk):
            cols = slice(j * nst, (j + 1) * nst)
            br = bre_ref[j]
            bi = bim_ref[j]
            wb_ref[j, :, 0:nst] = (cr[:, cols] * br - ci[:, cols] * bi).astype(BF16)
            wb_ref[j, :, nst:2 * nst] = (cr[:, cols] * bi + ci[:, cols] * br).astype(BF16)
        row = lax.broadcasted_iota(jnp.int32, (sub, gn), 0)
        pr, pi = ar, ai
        accr = jnp.zeros((sub, gn), F32)
        acci = jnp.zeros((sub, gn), F32)
        powers = {}
        for r in range(sub):
            powers[r + 1] = (pr, pi)
            accr = jnp.where(row == r, pr, accr)
            acci = jnp.where(row == r, pi, acci)
            pr, pi = pr * ar - pi * ai, pr * ai + pi * ar
        p_ref[0] = accr
        p_ref[1] = acci
        for idx, dd in enumerate((1, 2, 4)):
            qr, qi = powers[dd]
            am_ref[2 * idx] = jnp.where(row >= dd, qr, 0.0)
            am_ref[2 * idx + 1] = jnp.where(row >= dd, qi, 0.0)
        carry_ref[...] = jnp.zeros_like(carry_ref)

    u = u_ref[...].astype(F32)
    ub = u.astype(BF16)
    for j in range(nblk):
        bu = jnp.dot(ub[:, j * cin:(j + 1) * cin], wb_ref[j], preferred_element_type=F32)
        re_ref[:, j * nst:(j + 1) * nst] = bu[:, :nst]
        im_ref[:, j * nst:(j + 1) * nst] = bu[:, nst:]

    for c in range(gn // lane_chunk):
        lsl = slice(c * lane_chunk, (c + 1) * lane_chunk)

        def group(g, carry, lsl=lsl):
            cr_, ci_ = carry
            r0 = pl.multiple_of(g * sub, sub)
            rows = pl.ds(r0, sub)
            xr = re_ref[rows, lsl]
            xi = im_ref[rows, lsl]
            for idx, dd in enumerate((1, 2, 4)):
                mr = am_ref[2 * idx, :, lsl]
                mi = am_ref[2 * idx + 1, :, lsl]
                sr = pltpu.roll(xr, dd, 0)
                si = pltpu.roll(xi, dd, 0)
                xr, xi = xr + (mr * sr - mi * si), xi + (mr * si + mi * sr)
            pr_ = p_ref[0, :, lsl]
            pi_ = p_ref[1, :, lsl]
            hr = xr + (pr_ * cr_ - pi_ * ci_)
            hi = xi + (pr_ * ci_ + pi_ * cr_)
            re_ref[rows, lsl] = hr
            im_ref[rows, lsl] = hi
            return (jnp.broadcast_to(hr[sub - 1:sub, :], (sub, lane_chunk)),
                    jnp.broadcast_to(hi[sub - 1:sub, :], (sub, lane_chunk)))

        cr_, ci_ = lax.fori_loop(0, tc // sub, group, (carry_ref[0, :, lsl], carry_ref[1, :, lsl]), unroll=2)
        carry_ref[0, :, lsl] = cr_
        carry_ref[1, :, lsl] = ci_

    for j in range(nblk):
        cols = slice(j * nst, (j + 1) * nst)
        yj = (jnp.dot(re_ref[:, cols].astype(BF16), cre_ref[j], preferred_element_type=F32)
              - jnp.dot(im_ref[:, cols].astype(BF16), cim_ref[j], preferred_element_type=F32))
        ucols = slice(j * cin, (j + 1) * cin)
        y_ref[:, ucols] = yj + d_ref[:, ucols] * u[:, ucols]
    g = _gelu(y_ref[...])
    gate = _sigmoid(jnp.dot(g.astype(BF16), gw_ref[...], preferred_element_type=F32) + gb_ref[...])
    o_ref[...] = (g * gate).astype(o_ref.dtype)


def s5_mixer(u, lam_re, lam_im, log_dt, b_re, b_im, c_re, c_im, d, glu_w, glu_b, bsz, seq):
    t, ws = u.shape
    groups, nstate, gch = b_re.shape
    gpb = S5_GROUPS_PER_BLOCK
    nblk = groups // gpb
    cin = gpb * gch
    nst = gpb * nstate
    gn = groups * nstate
    tc = _tile(seq, 256)
    ns = seq // tc
    lane_chunk = _tile(gn, 512)
    eye = jnp.eye(gpb, dtype=F32)

    def bdiag_in(b):
        bb = b.reshape(nblk, gpb, nstate, gch).transpose(0, 1, 3, 2)
        return (bb[:, :, :, None, :] * eye[None, :, None, :, None]).reshape(nblk, cin, nst)

    def bdiag_out(c):
        cc = c.reshape(nblk, gpb, gch, nstate).transpose(0, 1, 3, 2)
        return (cc[:, :, :, None, :] * eye[None, :, None, :, None]).reshape(nblk, nst, cin)

    const2 = lambda b, s: (0, 0)
    const3 = lambda b, s: (0, 0, 0)
    return pl.pallas_call(
        functools.partial(_s5_kernel, tc=tc, nblk=nblk, cin=cin, nst=nst, lane_chunk=lane_chunk),
        grid=(bsz, ns),
        in_specs=[pl.BlockSpec((tc, ws), lambda b, s: (b * ns + s, 0)),
                  pl.BlockSpec((1, gn), const2), pl.BlockSpec((1, gn), const2), pl.BlockSpec((1, gn), const2),
                  pl.BlockSpec((nblk, cin, nst), const3), pl.BlockSpec((nblk, cin, nst), const3),
                  pl.BlockSpec((nblk, nst, cin), const3), pl.BlockSpec((nblk, nst, cin), const3),
                  pl.BlockSpec((1, ws), const2),
                  pl.BlockSpec((ws, ws), const2),
                  pl.BlockSpec((1, ws), const2)],
        out_specs=pl.BlockSpec((tc, ws), lambda b, s: (b * ns + s, 0)),
        out_shape=jax.ShapeDtypeStruct((t, ws), BF16),
        scratch_shapes=[pltpu.VMEM((nblk, cin, 2 * nst), BF16),
                        pltpu.VMEM((6, V7X_SUBLANES, gn), F32),
                        pltpu.VMEM((2, V7X_SUBLANES, gn), F32),
                        pltpu.VMEM((2, V7X_SUBLANES, gn), F32),
                        pltpu.VMEM((tc, gn), F32),
                        pltpu.VMEM((tc, gn), F32),
                        pltpu.VMEM((tc, ws), F32)],
        compiler_params=_cparams(("arbitrary", "arbitrary")),
        name="s5_mixer",
    )(u, lam_re.reshape(1, gn), lam_im.reshape(1, gn),
      jnp.broadcast_to(log_dt[:, None], (groups, nstate)).reshape(1, gn),
      bdiag_in(b_re), bdiag_in(b_im), bdiag_out(c_re).astype(BF16), bdiag_out(c_im).astype(BF16),
      d.reshape(1, ws), glu_w.astype(BF16), glu_b.reshape(1, ws))


def _sgu_kernel(u_ref, v_ref, g_ref, b_ref, w_ref, bs_ref, o_ref, *, heads, hd, chunk, nchunk):
    v = _gelu(v_ref[...].astype(F32))
    mu = jnp.mean(v, axis=-1, keepdims=True)
    vc = v - mu
    var = jnp.mean(vc * vc, axis=-1, keepdims=True)
    vn = (vc * lax.rsqrt(var + EPS) * g_ref[...] + b_ref[...]).astype(BF16)
    r = lax.broadcasted_iota(jnp.int32, (chunk, chunk), 0)
    c = lax.broadcasted_iota(jnp.int32, (chunk, chunk), 1)
    tril = r >= c
    for h in range(heads):
        wh = jnp.where(tril, w_ref[h], 0.0).astype(BF16)
        cols = slice(h * hd, (h + 1) * hd)
        for n in range(nchunk):
            rows = slice(n * chunk, (n + 1) * chunk)
            gsp = jnp.dot(wh, vn[rows, cols], preferred_element_type=F32) + bs_ref[h]
            o_ref[rows, cols] = (_gelu(u_ref[rows, cols].astype(F32)) * gsp).astype(o_ref.dtype)


def sgu_mixer(z, ln_g, ln_b, w_s, b_s):
    t = z.shape[0]
    heads, chunk, _ = w_s.shape
    w = ln_g.shape[0]
    hd = w // heads
    nchunk = 2 if (t // chunk) % 2 == 0 else 1
    tm = nchunk * chunk
    bs = jnp.broadcast_to(b_s[:, :, None], (heads, chunk, hd))
    return pl.pallas_call(
        functools.partial(_sgu_kernel, heads=heads, hd=hd, chunk=chunk, nchunk=nchunk),
        grid=(t // tm,),
        in_specs=[pl.BlockSpec((tm, w), lambda i: (i, 0)),
                  pl.BlockSpec((tm, w), lambda i: (i, 1)),
                  pl.BlockSpec((1, w), lambda i: (0, 0)),
                  pl.BlockSpec((1, w), lambda i: (0, 0)),
                  pl.BlockSpec((heads, chunk, chunk), lambda i: (0, 0, 0)),
                  pl.BlockSpec((heads, chunk, hd), lambda i: (0, 0, 0))],
        out_specs=pl.BlockSpec((tm, w), lambda i: (i, 0)),
        out_shape=jax.ShapeDtypeStruct((t, w), BF16),
        compiler_params=_cparams(("arbitrary",)),
        name="sgu_mixer",
    )(z, z, ln_g.reshape(1, w), ln_b.reshape(1, w), w_s, bs)


def _qk_norm_kernel(x_ref, g_ref, o_ref, *, nseg, seg):
    x = x_ref[...].astype(F32)
    for i in range(nseg):
        cols = slice(i * seg, (i + 1) * seg)
        xs = x[:, cols]
        ms = jnp.mean(xs * xs, axis=-1, keepdims=True)
        o_ref[:, cols] = (xs * lax.rsqrt(ms + EPS) * g_ref[:, cols]).astype(o_ref.dtype)


def qk_norm(z, col_block, width, gain_row, seg):
    t = z.shape[0]
    tm = _tile(t, 512)
    return pl.pallas_call(
        functools.partial(_qk_norm_kernel, nseg=width // seg, seg=seg),
        grid=(t // tm,),
        in_specs=[pl.BlockSpec((tm, width), lambda i: (i, col_block)),
                  pl.BlockSpec((1, width), lambda i: (0, 0))],
        out_specs=pl.BlockSpec((tm, width), lambda i: (i, 0)),
        out_shape=jax.ShapeDtypeStruct((t, width), BF16),
        compiler_params=_cparams(("arbitrary",)),
        name="qk_norm",
    )(z, gain_row)


def _attn_kernel(q_ref, k_ref, v_ref, bias_ref, lq1_ref, lk1_ref, lq2_ref, lk2_ref, sg_ref, o_ref,
                 acc0_ref, acc1_ref, m0_ref, m1_ref, l0_ref, l1_ref, *, tq, dqk, lam_init):
    qi = pl.program_id(2)
    accs, ms, ls = (acc0_ref, acc1_ref), (m0_ref, m1_ref), (l0_ref, l1_ref)
    for mp in range(2):
        ms[mp][...] = jnp.full_like(ms[mp], NEG_INF)
        ls[mp][...] = jnp.zeros_like(ls[mp])
        accs[mp][...] = jnp.zeros_like(accs[mp])
    r = lax.broadcasted_iota(jnp.int32, (tq, tq), 0)
    c = lax.broadcasted_iota(jnp.int32, (tq, tq), 1)
    causal = r >= c
    krep = tq // V7X_LANES
    vrep = accs[0].shape[1] // V7X_LANES

    def block(kb, bias_idx, masked):
        k0 = pl.multiple_of(kb * tq, tq)
        v = v_ref[pl.ds(k0, tq), :]
        for mp in range(2):
            q = q_ref[:, mp * dqk:(mp + 1) * dqk]
            k = k_ref[pl.ds(k0, tq), mp * dqk:(mp + 1) * dqk]
            sc = lax.dot_general(q, k, (((1,), (1,)), ((), ())), preferred_element_type=F32)
            if bias_idx is not None:
                sc = sc + bias_ref[bias_idx]
            if masked:
                sc = jnp.where(causal, sc, NEG_INF)
            m_old = ms[mp][...]
            m_new = jnp.maximum(m_old, jnp.max(sc, axis=-1, keepdims=True))
            alpha = jnp.exp2(m_old - m_new)
            p = jnp.exp2(sc - jnp.tile(m_new, (1, krep)))
            ls[mp][...] = alpha * ls[mp][...] + jnp.sum(p, axis=-1, keepdims=True)
            accs[mp][...] = (jnp.tile(alpha, (1, vrep)) * accs[mp][...]
                             + jnp.dot(p.astype(BF16), v, preferred_element_type=F32))
            ms[mp][...] = m_new

    def far(kb, _):
        block(kb, None, False)
        return 0

    lax.fori_loop(0, jnp.maximum(qi - 1, 0), far, 0)

    @pl.when(qi > 0)
    def _():
        block(qi - 1, 1, False)

    block(qi, 0, True)

    lam = (jnp.exp(jnp.sum(lq1_ref[...] * lk1_ref[...], axis=-1, keepdims=True))
           - jnp.exp(jnp.sum(lq2_ref[...] * lk2_ref[...], axis=-1, keepdims=True)) + lam_init)
    o = (accs[0][...] * jnp.tile(1.0 / ls[0][...], (1, vrep))
         - lam * (accs[1][...] * jnp.tile(1.0 / ls[1][...], (1, vrep))))
    ms_o = jnp.mean(o * o, axis=-1, keepdims=True)
    o_ref[...] = (o * lax.rsqrt(ms_o + EPS) * sg_ref[...] * (1.0 - lam_init)).astype(o_ref.dtype)


def _t5_bucket(n, buckets):
    max_exact = buckets // 2
    nf = jnp.maximum(n, 1).astype(F32)
    large = max_exact + (jnp.log(nf / max_exact) / math.log(REL_MAX_DIST / max_exact)
                         * (buckets - max_exact)).astype(jnp.int32)
    large = jnp.minimum(large, buckets - 1)
    return jnp.where(n < max_exact, n, large)


def _toeplitz(w, rows, cols):
    nh, lw = w.shape
    flat = jnp.tile(w, (1, rows))[:, :rows * (lw - 1)]
    return flat.reshape(nh, rows, lw - 1)[:, :, :cols]


def diff_attention(qn, kn, zqkv, v_col_block, rel_bias, lq1, lk1, lq2, lk2, sub_g, bsz, seq, layer):
    t = qn.shape[0]
    buckets, heads = rel_bias.shape
    dv = sub_g.shape[0]
    dqk = dv // 2
    tq = _tile(seq, 512)
    assert tq >= REL_MAX_DIST, "far key blocks must all fall in the last relative-position bucket"
    nq = seq // tq
    lam_init = 0.8 - 0.6 * math.exp(-0.3 * layer)
    table = jnp.transpose(rel_bias.astype(F32))
    fvals = table[:, _t5_bucket(jnp.arange(2 * tq + 1, dtype=jnp.int32), buckets)]
    fvals = (fvals - fvals[:, 2 * tq:]) * LOG2E
    jj = np.arange(2 * tq)
    n_diag = np.where(jj <= tq, 0, 2 * tq - jj)
    n_near = np.where(jj < tq, tq - jj, 3 * tq - jj)
    bias = jnp.stack([_toeplitz(fvals[:, n_diag], tq, tq), _toeplitz(fvals[:, n_near], tq, tq)], axis=1)
    row = lambda b, h, i: (0, 0)
    return pl.pallas_call(
        functools.partial(_attn_kernel, tq=tq, dqk=dqk, lam_init=lam_init),
        grid=(bsz, heads, nq),
        in_specs=[pl.BlockSpec((tq, dv), lambda b, h, i: (b * nq + i, h)),
                  pl.BlockSpec((seq, dv), lambda b, h, i: (b, h)),
                  pl.BlockSpec((seq, dv), lambda b, h, i: (b, v_col_block + h)),
                  pl.BlockSpec((None, 2, tq, tq), lambda b, h, i: (h, 0, 0, 0)),
                  pl.BlockSpec((1, dqk), row), pl.BlockSpec((1, dqk), row),
                  pl.BlockSpec((1, dqk), row), pl.BlockSpec((1, dqk), row),
                  pl.BlockSpec((1, dv), row)],
        out_specs=pl.BlockSpec((tq, dv), lambda b, h, i: (b * nq + i, h)),
        out_shape=jax.ShapeDtypeStruct((t, heads * dv), BF16),
        scratch_shapes=[pltpu.VMEM((tq, dv), F32), pltpu.VMEM((tq, dv), F32),
                        pltpu.VMEM((tq, V7X_LANES), F32), pltpu.VMEM((tq, V7X_LANES), F32),
                        pltpu.VMEM((tq, V7X_LANES), F32), pltpu.VMEM((tq, V7X_LANES), F32)],
        compiler_params=_cparams(("arbitrary", "arbitrary", "arbitrary")),
        name="diff_attention",
    )(qn, kn, zqkv, bias, lq1.reshape(1, dqk), lk1.reshape(1, dqk), lq2.reshape(1, dqk), lk2.reshape(1, dqk),
      sub_g.reshape(1, dv))


ROUTER_LANES = 128


def _norm_mod(x, g_ref, sc_ref, sh_ref):
    ms = jnp.mean(x * x, axis=-1, keepdims=True)
    return (x * lax.rsqrt(ms + EPS) * g_ref[...]) * (1.0 + sc_ref[...]) + sh_ref[...]


def _router_kernel(x_ref, g_ref, sc_ref, sh_ref, wr_ref, wlo_ref, br_ref, route_ref, *, ngroups, per_group):
    h = _norm_mod(x_ref[...], g_ref, sc_ref, sh_ref)
    hi = h.astype(BF16)
    lo = (h - hi.astype(F32)).astype(BF16)
    both = jnp.dot(hi, wr_ref[...], preferred_element_type=F32)
    nl = ROUTER_LANES
    logits = both[:, :nl] + both[:, nl:] + jnp.dot(lo, wlo_ref[...], preferred_element_type=F32) + br_ref[...]
    lane = lax.broadcasted_iota(jnp.int32, logits.shape, 1).astype(F32)
    big = float(nl)
    glog = jnp.where(lane < ngroups, logits, NEG_INF)
    gmax = jnp.max(glog, axis=-1, keepdims=True)
    gsum = jnp.sum(jnp.exp(glog - gmax), axis=-1, keepdims=True)
    gp = 1.0 / gsum
    gidx = jnp.min(jnp.where(glog == gmax, lane, big), axis=-1, keepdims=True)
    lo_lane = ngroups + gidx * per_group
    emask = (lane >= lo_lane) & (lane < lo_lane + per_group)
    elog = jnp.where(emask, logits, NEG_INF)
    emax = jnp.max(elog, axis=-1, keepdims=True)
    eexp = jnp.where(emask, jnp.exp(elog - emax), -1.0)
    i0 = jnp.min(jnp.where(eexp == 1.0, lane, big), axis=-1, keepdims=True)
    rest = jnp.where(lane == i0, -1.0, eexp)
    p1 = jnp.max(rest, axis=-1, keepdims=True)
    i1 = jnp.min(jnp.where(rest == p1, lane, big), axis=-1, keepdims=True)
    denom = 1.0 + p1
    w0 = gp * (1.0 / denom)
    w1 = gp * (p1 / denom)
    e0 = i0 - ngroups
    e1 = i1 - ngroups
    route_ref[...] = jnp.where(lane == 0, e0, jnp.where(lane == 1, e1, jnp.where(lane == 2, w0,
                               jnp.where(lane == 3, w1, 0.0))))


def moe_router(x, g, sc, sh, wg, bg, we, be, seq):
    t, d = x.shape
    bsz = sc.shape[0]
    ngroups = wg.shape[-1]
    per_group = we.shape[-1]
    nexp = ngroups * per_group
    nl = ROUTER_LANES
    assert ngroups + nexp <= nl
    wr = jnp.concatenate([wg, jnp.transpose(we, (1, 0, 2)).reshape(d, nexp)], axis=-1)
    wr = jnp.zeros((d, nl), F32).at[:, :ngroups + nexp].set(wr)
    w_hi = wr.astype(BF16)
    w_lo = (wr - w_hi.astype(F32)).astype(BF16)
    br = jnp.zeros((1, nl), F32).at[0, :ngroups + nexp].set(jnp.concatenate([bg, be.reshape(nexp)]))
    tm = _tile(seq, 256)
    per = seq // tm
    return pl.pallas_call(
        functools.partial(_router_kernel, ngroups=ngroups, per_group=per_group),
        grid=(t // tm,),
        in_specs=[pl.BlockSpec((tm, d), lambda i: (i, 0)),
                  pl.BlockSpec((1, d), lambda i: (0, 0)),
                  pl.BlockSpec((None, 1, d), lambda i: (i // per, 0, 0)),
                  pl.BlockSpec((None, 1, d), lambda i: (i // per, 0, 0)),
                  pl.BlockSpec((d, 2 * nl), lambda i: (0, 0)),
                  pl.BlockSpec((d, nl), lambda i: (0, 0)),
                  pl.BlockSpec((1, nl), lambda i: (0, 0))],
        out_specs=pl.BlockSpec((tm, nl), lambda i: (i, 0)),
        out_shape=jax.ShapeDtypeStruct((t, nl), F32),
        compiler_params=_cparams(("arbitrary",)),
        name="moe_router",
    )(x, g.reshape(1, d), sc.reshape(bsz, 1, d), sh.reshape(bsz, 1, d),
      jnp.concatenate([w_hi, w_lo], axis=-1), w_hi, br)


HI16 = 0xFFFF0000


def _dispatch_kernel(pos_ref, zf_ref, x_ref, g_ref, sc_ref, sh_ref, xs_hbm, buf, zbuf, sem, zsem,
                     *, tm, tme, ntok, ntile, topk, half):
    i = pl.program_id(0)
    nsteps = pl.num_programs(0)
    slot = lax.rem(i, 2)

    @pl.when(i == 0)
    def _():
        zbuf[...] = jnp.zeros_like(zbuf)

        def zcopy(tl):
            return pltpu.make_async_copy(zbuf, xs_hbm.at[pl.ds(tl * tme, tme)], zsem)

        def zstart(tl, _):
            @pl.when(zf_ref[tl] > 0)
            def _():
                zcopy(tl).start()
            return 0

        def zwait(tl, _):
            @pl.when(zf_ref[tl] > 0)
            def _():
                zcopy(tl).wait()
            return 0

        lax.fori_loop(0, ntile, zstart, 0)
        lax.fori_loop(0, ntile, zwait, 0)

    def pack_rows(cidx, _):
        rows = pl.ds(pl.multiple_of(cidx * V7X_SUBLANES, V7X_SUBLANES), V7X_SUBLANES)
        h = _norm_mod(x_ref[rows, :], g_ref, sc_ref, sh_ref)
        bits = pltpu.bitcast(h.astype(BF16).astype(F32), jnp.uint32)
        buf[slot, rows, :] = (bits[:, half:] & jnp.uint32(HI16)) | (bits[:, :half] >> 16)
        return 0

    lax.fori_loop(0, tm // V7X_SUBLANES, pack_rows, 0, unroll=4)

    base = i * tm

    def issue(r, _):
        for kk in range(topk):
            pltpu.make_async_copy(buf.at[slot, pl.ds(r, 1)],
                                  xs_hbm.at[pl.ds(pos_ref[kk * ntok + base + r], 1)], sem.at[slot]).start()
        return 0

    lax.fori_loop(0, tm, issue, 0, unroll=4)

    def drain(s):
        for _ in range(topk):
            pltpu.make_async_copy(buf.at[s], xs_hbm.at[pl.ds(0, tm)], sem.at[s]).wait()

    @pl.when(i > 0)
    def _():
        drain(1 - slot)

    @pl.when(i == nsteps - 1)
    def _():
        drain(slot)


def _expert_kernel(texp_ref, tvalid_ref, xidx_ref, xs_ref, wg_ref, wu_ref, wd_ref, y_ref, *, half):
    i = pl.program_id(0)

    @pl.when(tvalid_ref[i] > 0)
    def _():
        xp = xs_ref[...]
        x_lo = pltpu.bitcast(xp << 16, F32).astype(BF16)
        x_hi = pltpu.bitcast(xp & jnp.uint32(HI16), F32).astype(BF16)
        hg = (jnp.dot(x_lo, wg_ref[0:half, :], preferred_element_type=F32)
              + jnp.dot(x_hi, wg_ref[half:, :], preferred_element_type=F32))
        hu = (jnp.dot(x_lo, wu_ref[0:half, :], preferred_element_type=F32)
              + jnp.dot(x_hi, wu_ref[half:, :], preferred_element_type=F32))
        act = (hg * _sigmoid(hg)) * hu
        y_ref[...] = jnp.dot(act.astype(BF16), wd_ref[...], preferred_element_type=F32)

    @pl.when(tvalid_ref[i] == 0)
    def _():
        y_ref[...] = jnp.zeros_like(y_ref)


def _combine_kernel(pos_ref, x_ref, route_ref, g_ref, y_hbm, o_ref, ybuf, sem, *, tm, ntok, topk):
    i = pl.program_id(0)
    nsteps = pl.num_programs(0)
    slot = lax.rem(i, 2)

    def gather(step, s):
        base = step * tm

        def issue(r, _):
            for kk in range(topk):
                pltpu.make_async_copy(y_hbm.at[pl.ds(pos_ref[kk * ntok + base + r], 1)],
                                      ybuf.at[s, kk, pl.ds(r, 1)], sem.at[s]).start()
            return 0

        lax.fori_loop(0, tm, issue, 0, unroll=4)

    @pl.when(i == 0)
    def _():
        gather(0, 0)

    @pl.when(i + 1 < nsteps)
    def _():
        gather(i + 1, 1 - slot)

    for kk in range(topk):
        pltpu.make_async_copy(y_hbm.at[pl.ds(0, tm)], ybuf.at[slot, kk], sem.at[slot]).wait()

    def mix_rows(cidx, _):
        rows = pl.ds(pl.multiple_of(cidx * V7X_SUBLANES, V7X_SUBLANES), V7X_SUBLANES)
        w = route_ref[rows, :]
        y = w[:, topk:topk + 1] * ybuf[slot, 0, rows, :]
        for kk in range(1, topk):
            y = y + w[:, topk + kk:topk + kk + 1] * ybuf[slot, kk, rows, :]
        o_ref[rows, :] = x_ref[rows, :] + g_ref[...] * y
        return 0

    lax.fori_loop(0, tm // V7X_SUBLANES, mix_rows, 0, unroll=4)


def moe_layer(x, route, g, sc, sh, gate, w_gate, w_up, w_down, layer, seq):
    t, d = x.shape
    bsz = sc.shape[0]
    _, nexp, _, f = w_gate.shape
    topk = MOE_TOPK
    half = d // 2
    npair = t * topk
    tme = _tile(npair // nexp, 256) if npair // nexp >= 8 else 8
    ntile = npair // tme + nexp
    tm = _tile(seq, 256)
    per = seq // tm

    flat_e = jnp.transpose(route[:, 0:topk]).astype(jnp.int32).reshape(npair)
    onehot = (flat_e[:, None] == jnp.arange(nexp, dtype=jnp.int32)[None, :]).astype(jnp.int32)
    csum = jnp.cumsum(onehot, axis=0)
    counts = csum[-1]
    tiles_per = (counts + tme - 1) // tme
    tile_end = jnp.cumsum(tiles_per)
    tile_start = tile_end - tiles_per
    pos = jnp.sum(onehot * (tile_start[None, :] * tme + csum - 1), axis=1).astype(jnp.int32)
    n_used = tile_end[-1]
    tile_id = jnp.arange(ntile, dtype=jnp.int32)
    active = tile_id < n_used
    texp = jnp.minimum(jnp.sum((tile_id[:, None] >= tile_end[None, :]).astype(jnp.int32), axis=1), nexp - 1)
    tvalid = jnp.where(active, jnp.clip(counts[texp] - (tile_id - tile_start[texp]) * tme, 0, tme), 0)
    tvalid = tvalid.astype(jnp.int32)
    texp = jnp.where(active, texp, texp[jnp.maximum(n_used - 1, 0)]).astype(jnp.int32)
    xidx = jnp.minimum(tile_id, jnp.maximum(n_used - 1, 0)).astype(jnp.int32)
    zflag = (tvalid < tme).astype(jnp.int32)

    mod_specs = [pl.BlockSpec((1, d), lambda i, *_: (0, 0)),
                 pl.BlockSpec((None, 1, d), lambda i, *_: (i // per, 0, 0)),
                 pl.BlockSpec((None, 1, d), lambda i, *_: (i // per, 0, 0))]
    xs = pl.pallas_call(
        functools.partial(_dispatch_kernel, tm=tm, tme=tme, ntok=t, ntile=ntile, topk=topk, half=half),
        grid_spec=pltpu.PrefetchScalarGridSpec(
            num_scalar_prefetch=2,
            grid=(t // tm,),
            in_specs=[pl.BlockSpec((tm, d), lambda i, *_: (i, 0))] + mod_specs,
            out_specs=pl.BlockSpec(memory_space=pl.ANY),
            scratch_shapes=[pltpu.VMEM((2, tm, half), jnp.uint32), pltpu.VMEM((tme, half), jnp.uint32),
                            pltpu.SemaphoreType.DMA((2,)), pltpu.SemaphoreType.DMA(())]),
        out_shape=jax.ShapeDtypeStruct((ntile * tme, half), jnp.uint32),
        compiler_params=_cparams(("arbitrary",)),
        name="moe_dispatch",
    )(pos, zflag, x, g.reshape(1, d), sc.reshape(bsz, 1, d), sh.reshape(bsz, 1, d))

    y = pl.pallas_call(
        functools.partial(_expert_kernel, half=half),
        grid_spec=pltpu.PrefetchScalarGridSpec(
            num_scalar_prefetch=3,
            grid=(ntile,),
            in_specs=[pl.BlockSpec((tme, half), lambda i, te, tv, xi: (xi[i], 0)),
                      pl.BlockSpec((None, None, d, f), lambda i, te, tv, xi: (layer, te[i], 0, 0)),
                      pl.BlockSpec((None, None, d, f), lambda i, te, tv, xi: (layer, te[i], 0, 0)),
                      pl.BlockSpec((None, None, f, d), lambda i, te, tv, xi: (layer, te[i], 0, 0))],
            out_specs=pl.BlockSpec((tme, d), lambda i, te, tv, xi: (i, 0))),
        out_shape=jax.ShapeDtypeStruct((ntile * tme, d), F32),
        compiler_params=_cparams(("arbitrary",)),
        name="moe_experts",
    )(texp, tvalid, xidx, xs, w_gate, w_up, w_down)

    return pl.pallas_call(
        functools.partial(_combine_kernel, tm=tm, ntok=t, topk=topk),
        grid_spec=pltpu.PrefetchScalarGridSpec(
            num_scalar_prefetch=1,
            grid=(t // tm,),
            in_specs=[pl.BlockSpec((tm, d), lambda i, *_: (i, 0)),
                      pl.BlockSpec((tm, ROUTER_LANES), lambda i, *_: (i, 0)),
                      pl.BlockSpec((None, 1, d), lambda i, *_: (i // per, 0, 0)),
                      pl.BlockSpec(memory_space=pl.ANY)],
            out_specs=pl.BlockSpec((tm, d), lambda i, *_: (i, 0)),
            scratch_shapes=[pltpu.VMEM((2, topk, tm, d), F32), pltpu.SemaphoreType.DMA((2,))]),
        out_shape=jax.ShapeDtypeStruct((t, d), F32),
        compiler_params=_cparams(("arbitrary",)),
        name="moe_combine",
    )(pos, x, route, gate.reshape(bsz, 1, d), y)


def kernel(x, c, norm1_g, norm2_g, ada_w, ada_b, ab_w_in, ab_w_out, lru_conv_w, lru_conv_b, lru_wa, lru_ba, lru_wx, lru_bx, lru_lambda, s5_lambda_re, s5_lambda_im, s5_log_dt, s5_b_re, s5_b_im, s5_c_re, s5_c_im, s5_d, s5_glu_w, s5_glu_b, cd_w_in, cd_w_out, sg_ln_g, sg_ln_b, sg_w, sg_b, da_q_norm, da_k_norm, da_lq1, da_lk1, da_lq2, da_lk2, da_sub_g, rel_bias, moe_wg, moe_bg, moe_we, moe_be, moe_w_gate, moe_w_up, moe_w_down):
    bsz, seq, d = x.shape
    depth = norm1_g.shape[0]
    t = bsz * seq
    xt = x.reshape(t, d)
    mod = ada_modulation(c, ada_w, ada_b)
    wg_all, wu_all, wd_all = moe_w_gate.astype(BF16), moe_w_up.astype(BF16), moe_w_down.astype(BF16)

    for layer in range(depth):
        sh1, sc1, g1, sh2, sc2, g2 = [mod[layer, :, i * d:(i + 1) * d] for i in range(6)]
        hmix = norm_modulate(xt, norm1_g[layer], sc1, sh1, seq)
        j = layer // 2
        if layer % 2 == 0:
            lw = lru_conv_w.shape[-1]
            w_in = ab_w_in[j].astype(BF16)
            w_out = ab_w_out[j].astype(BF16)
            z_lru = matmul([hmix], w_in, BF16, col_off=0, ncols=2 * lw)
            z_s5 = matmul([hmix], w_in, BF16, col_off=2 * lw)
            y_a = lru_mixer(z_lru, lru_conv_w[j], lru_conv_b[j], lru_wa[j], lru_ba[j], lru_wx[j], lru_bx[j],
                            lru_lambda[j], bsz, seq)
            y_b = s5_mixer(z_s5, s5_lambda_re[j], s5_lambda_im[j], s5_log_dt[j], s5_b_re[j], s5_b_im[j],
                           s5_c_re[j], s5_c_im[j], s5_d[j], s5_glu_w[j], s5_glu_b[j], bsz, seq)
            xt = matmul([y_a, y_b], w_out, F32, res=xt, gate=g1, seq=seq)
        else:
            sgw = sg_ln_g.shape[-1]
            dqk = da_q_norm.shape[-1]
            dv = da_sub_g.shape[-1]
            heads = rel_bias.shape[1]
            daw = heads * dv
            w_in = cd_w_in[j].astype(BF16)
            w_out = cd_w_out[j].astype(BF16)
            z_sg = matmul([hmix], w_in, BF16, col_off=0, ncols=2 * sgw)
            z_qkv = matmul([hmix], w_in, BF16, col_off=2 * sgw)
            y_c = sgu_mixer(z_sg, sg_ln_g[j], sg_ln_b[j], sg_w[j], sg_b[j])
            q_gain = jnp.tile(da_q_norm[j] * (dqk ** -0.5 * LOG2E), daw // dqk).reshape(1, daw)
            k_gain = jnp.tile(da_k_norm[j], daw // dqk).reshape(1, daw)
            qn = qk_norm(z_qkv, 0, daw, q_gain, dqk)
            kn = qk_norm(z_qkv, 1, daw, k_gain, dqk)
            y_d = diff_attention(qn, kn, z_qkv, 2 * daw // dv, rel_bias, da_lq1[j], da_lk1[j], da_lq2[j], da_lk2[j],
                                 da_sub_g[j], bsz, seq, layer)
            xt = matmul([y_c, y_d], w_out, F32, res=xt, gate=g1, seq=seq)
        route = moe_router(xt, norm2_g[layer], sc2, sh2, moe_wg[layer], moe_bg[layer], moe_we[layer], moe_be[layer],
                           seq)
        xt = moe_layer(xt, route, norm2_g[layer], sc2, sh2, g2, wg_all, wu_all, wd_all, layer, seq)
    return xt.reshape(bsz, seq, d)
```

```python
import functools
import math

import jax
import jax.numpy as jnp
import numpy as np
from jax import lax
from jax.experimental import pallas as pl
from jax.experimental.pallas import tpu as pltpu

F32 = jnp.float32
BF16 = jnp.bfloat16

EPS = 1e-6
LRU_C = 8.0
REL_MAX_DIST = 128
MOE_TOPK = 2
NEG_INF = -1e30
LOG2E = math.log2(math.e)

V7X_LANES = 128
V7X_SUBLANES = 8
V7X_VMEM_LIMIT_BYTES = 56 * 1024 * 1024


def _cparams(semantics):
    return pltpu.CompilerParams(dimension_semantics=semantics, vmem_limit_bytes=V7X_VMEM_LIMIT_BYTES)


def _sigmoid(x):
    return 0.5 * jnp.tanh(0.5 * x) + 0.5


def _gelu(x):
    return 0.5 * x * (1.0 + jnp.tanh(math.sqrt(2.0 / math.pi) * (x + 0.044715 * (x * x * x))))


def _tile(n, want):
    t = min(n, want)
    while n % t:
        t -= 1
    return t


def _ada_kernel(c_ref, w_ref, b_ref, o_ref):
    c = c_ref[...]
    cond = c * _sigmoid(c)
    o_ref[...] = jnp.dot(cond.astype(BF16), w_ref[...].astype(BF16), preferred_element_type=F32) + b_ref[...]


def ada_modulation(c, ada_w, ada_b):
    bsz, d = c.shape
    depth, _, n = ada_w.shape
    rows = 16
    cp = jnp.zeros((rows, d), F32).at[:bsz].set(c)
    tn = _tile(n, 512)
    out = pl.pallas_call(
        _ada_kernel,
        grid=(depth, n // tn),
        in_specs=[pl.BlockSpec((rows, d), lambda l, j: (0, 0)),
                  pl.BlockSpec((None, d, tn), lambda l, j: (l, 0, j)),
                  pl.BlockSpec((None, 1, tn), lambda l, j: (l, 0, j))],
        out_specs=pl.BlockSpec((None, rows, tn), lambda l, j: (l, 0, j)),
        out_shape=jax.ShapeDtypeStruct((depth, rows, n), F32),
        compiler_params=_cparams(("arbitrary", "arbitrary")),
        name="ada_modulation",
    )(cp, ada_w, ada_b.reshape(depth, 1, n))
    return out[:, :bsz]


def _norm_mod_kernel(x_ref, g_ref, sc_ref, sh_ref, o_ref):
    x = x_ref[...]
    ms = jnp.mean(x * x, axis=-1, keepdims=True)
    y = x * lax.rsqrt(ms + EPS) * g_ref[...]
    o_ref[...] = (y * (1.0 + sc_ref[...]) + sh_ref[...]).astype(o_ref.dtype)


def norm_modulate(x, g, sc, sh, seq, out_dtype=BF16):
    t, d = x.shape
    bsz = sc.shape[0]
    tm = _tile(seq, 256)
    per = seq // tm
    return pl.pallas_call(
        _norm_mod_kernel,
        grid=(t // tm,),
        in_specs=[pl.BlockSpec((tm, d), lambda i: (i, 0)),
                  pl.BlockSpec((1, d), lambda i: (0, 0)),
                  pl.BlockSpec((None, 1, d), lambda i: (i // per, 0, 0)),
                  pl.BlockSpec((None, 1, d), lambda i: (i // per, 0, 0))],
        out_specs=pl.BlockSpec((tm, d), lambda i: (i, 0)),
        out_shape=jax.ShapeDtypeStruct((t, d), out_dtype),
        compiler_params=_cparams(("arbitrary",)),
        name="norm_modulate",
    )(x, g.reshape(1, d), sc.reshape(bsz, 1, d), sh.reshape(bsz, 1, d))


def _matmul_kernel(*refs, ksplit, has_res):
    n_lhs = len(ksplit)
    a_refs = refs[:n_lhs]
    w_ref = refs[n_lhs]
    o_ref = refs[-1]
    acc = None
    k0 = 0
    for a_ref, kk in zip(a_refs, ksplit):
        part = jnp.dot(a_ref[...], w_ref[k0:k0 + kk, :], preferred_element_type=F32)
        acc = part if acc is None else acc + part
        k0 += kk
    if has_res:
        res_ref, gate_ref = refs[n_lhs + 1], refs[n_lhs + 2]
        acc = res_ref[...] + gate_ref[...] * acc
    o_ref[...] = acc.astype(o_ref.dtype)


def matmul(lhs, w, out_dtype, col_off=0, ncols=None, res=None, gate=None, seq=None):
    m = lhs[0].shape[0]
    ktot = w.shape[0]
    ksplit = tuple(a.shape[1] for a in lhs)
    assert sum(ksplit) == ktot
    n = w.shape[1] - col_off if ncols is None else ncols
    tm = _tile(m if seq is None else seq, 1024)
    tn = _tile(math.gcd(n, col_off) if col_off else n, 512)
    joff = col_off // tn
    in_specs = [pl.BlockSpec((tm, kk), lambda i, j: (i, 0)) for kk in ksplit]
    in_specs += [pl.BlockSpec((ktot, tn), lambda i, j: (0, j + joff))]
    args = list(lhs) + [w]
    if res is not None:
        per = seq // tm
        bsz = gate.shape[0]
        in_specs += [pl.BlockSpec((tm, tn), lambda i, j: (i, j)),
                     pl.BlockSpec((None, 1, tn), lambda i, j: (i // per, 0, j))]
        args += [res, gate.reshape(bsz, 1, n)]
    return pl.pallas_call(
        functools.partial(_matmul_kernel, ksplit=ksplit, has_res=res is not None),
        grid=(m // tm, n // tn),
        in_specs=in_specs,
        out_specs=pl.BlockSpec((tm, tn), lambda i, j: (i, j)),
        out_shape=jax.ShapeDtypeStruct((m, n), out_dtype),
        compiler_params=_cparams(("arbitrary", "arbitrary")),
        name="matmul",
    )(*args)


def _interleave_rows(src_ref, dst_ref, nslab, nsub, sub_len):
    for sl in range(nslab):
        lanes = slice(sl * V7X_LANES, (sl + 1) * V7X_LANES)
        for r in range(nsub):
            dst_ref[sl, pl.ds(r, sub_len, stride=nsub), :] = src_ref[r * sub_len:(r + 1) * sub_len, lanes].astype(F32)


def _deinterleave_rows(src_ref, dst_ref, nslab, nsub, sub_len):
    for sl in range(nslab):
        lanes = slice(sl * V7X_LANES, (sl + 1) * V7X_LANES)
        for r in range(nsub):
            dst_ref[r * sub_len:(r + 1) * sub_len, lanes] = src_ref[sl, pl.ds(r, sub_len, stride=nsub), :].astype(
                dst_ref.dtype)


def _sublane_scan(a, b, row):
    for d in (1, 2, 4):
        keep = row >= d
        sa = jnp.where(keep, pltpu.roll(a, d, 0), 1.0)
        sb = jnp.where(keep, pltpu.roll(b, d, 0), 0.0)
        b = b + a * sb
        a = a * sa
    return a, b


def _bcast_last(x):
    return jnp.broadcast_to(x[V7X_SUBLANES - 1:V7X_SUBLANES, :], x.shape)


def _lru_kernel(x_ref, gate_ref, cw_ref, cb_ref, wax_ref, bax_ref, lam_ref, o_ref,
                xp_ref, gp_ref, yp_ref, tail_ref, a_ref, b_ref, carry_ref, *, heads, hd, tc, kconv):
    s = pl.program_id(1)
    sub = V7X_SUBLANES
    sub_len = tc // sub
    nslab = heads * hd // V7X_LANES
    spl = hd // V7X_LANES
    halo = (kconv - 1) * sub
    row = lax.broadcasted_iota(jnp.int32, (sub, V7X_LANES), 0)
    rowh = lax.broadcasted_iota(jnp.int32, (sub, hd), 0)

    @pl.when(s == 0)
    def _():
        tail_ref[...] = jnp.zeros_like(tail_ref)
        carry_ref[...] = jnp.zeros_like(carry_ref)

    for sl in range(nslab):
        lanes = slice(sl * V7X_LANES, (sl + 1) * V7X_LANES)
        for r in range(sub):
            xp_ref[sl, pl.ds(halo + r, sub_len, stride=sub), :] = x_ref[r * sub_len:(r + 1) * sub_len, lanes].astype(F32)
        for e in range(kconv - 1):
            cur = xp_ref[sl, halo + (sub_len - (kconv - 1) + e) * sub:halo + (sub_len - (kconv - 1) + e + 1) * sub, :]
            prev = tail_ref[sl, e * sub:(e + 1) * sub, :]
            xp_ref[sl, e * sub:(e + 1) * sub, :] = jnp.where(row == 0, pltpu.roll(prev, 1, 0), pltpu.roll(cur, 1, 0))
            tail_ref[sl, e * sub:(e + 1) * sub, :] = cur
    _interleave_rows(gate_ref, gp_ref, nslab, sub, sub_len)

    for h in range(heads):
        cols = slice(h * hd, (h + 1) * hd)
        parts = []
        for q in range(spl):
            sl = h * spl + q
            lanes = slice(sl * V7X_LANES, (sl + 1) * V7X_LANES)
            acc = cb_ref[:, lanes] + cw_ref[0:1, lanes] * xp_ref[sl, 0:tc, :]
            for k in range(1, kconv):
                acc = acc + cw_ref[k:k + 1, lanes] * xp_ref[sl, k * sub:k * sub + tc, :]
            parts.append(acc)
        xc = jnp.concatenate(parts, axis=-1) if spl > 1 else parts[0]
        pre = jnp.dot(xc.astype(BF16), wax_ref[h], preferred_element_type=F32) + bax_ref[h]
        r = _sigmoid(pre[:, :hd])
        gi = _sigmoid(pre[:, hd:])
        nl = -lam_ref[:, cols]
        sp = jnp.maximum(nl, 0.0) + jnp.log1p(jnp.exp(-jnp.abs(nl)))
        log_a = (-LRU_C) * r * sp
        a_ref[...] = jnp.exp(log_a)
        b_ref[...] = jnp.sqrt(1.0 - jnp.exp(2.0 * log_a)) * (gi * xc)

        def pass1(t, carry):
            hprev, pprev = carry
            rows = pl.ds(pl.multiple_of(t * sub, sub), sub)
            a = a_ref[rows, :]
            hh = a * hprev + b_ref[rows, :]
            pp = a * pprev
            b_ref[rows, :] = hh
            a_ref[rows, :] = pp
            return hh, pp

        zero = jnp.zeros((sub, hd), F32)
        hfin, pfin = lax.fori_loop(0, sub_len, pass1, (zero, zero + 1.0), unroll=4)
        cin = carry_ref[:, cols]
        ptot, hloc = _sublane_scan(pfin, hfin, rowh)
        hend = hloc + ptot * cin
        carry_ref[:, cols] = _bcast_last(hend)
        entry = jnp.where(rowh == 0, cin, pltpu.roll(hend, 1, 0))

        def pass2(t, _):
            rows = pl.ds(pl.multiple_of(t * sub, sub), sub)
            hh = b_ref[rows, :] + a_ref[rows, :] * entry
            for q in range(spl):
                sl = h * spl + q
                yp_ref[sl, rows, :] = _gelu(gp_ref[sl, rows, :]) * hh[:, q * V7X_LANES:(q + 1) * V7X_LANES]
            return 0

        lax.fori_loop(0, sub_len, pass2, 0, unroll=4)

    _deinterleave_rows(yp_ref, o_ref, nslab, sub, sub_len)


def lru_mixer(z, conv_w, conv_b, wa, ba, wx, bx, lam, bsz, seq):
    t = z.shape[0]
    heads, hd, _ = wa.shape
    w = heads * hd
    kconv = conv_w.shape[0]
    tc = _tile(seq, 256)
    ns = seq // tc
    nslab = w // V7X_LANES
    halo = (kconv - 1) * V7X_SUBLANES
    wax = jnp.concatenate([wa, wx], axis=-1).astype(BF16)
    bax = jnp.concatenate([ba.reshape(heads, 1, hd), bx.reshape(heads, 1, hd)], axis=-1)
    return pl.pallas_call(
        functools.partial(_lru_kernel, heads=heads, hd=hd, tc=tc, kconv=kconv),
        grid=(bsz, ns),
        in_specs=[pl.BlockSpec((tc, w), lambda b, s: (b * ns + s, 0)),
                  pl.BlockSpec((tc, w), lambda b, s: (b * ns + s, 1)),
                  pl.BlockSpec((kconv, w), lambda b, s: (0, 0)),
                  pl.BlockSpec((1, w), lambda b, s: (0, 0)),
                  pl.BlockSpec((heads, hd, 2 * hd), lambda b, s: (0, 0, 0)),
                  pl.BlockSpec((heads, 1, 2 * hd), lambda b, s: (0, 0, 0)),
                  pl.BlockSpec((1, w), lambda b, s: (0, 0))],
        out_specs=pl.BlockSpec((tc, w), lambda b, s: (b * ns + s, 0)),
        out_shape=jax.ShapeDtypeStruct((t, w), BF16),
        scratch_shapes=[pltpu.VMEM((nslab, halo + tc, V7X_LANES), F32),
                        pltpu.VMEM((nslab, tc, V7X_LANES), F32),
                        pltpu.VMEM((nslab, tc, V7X_LANES), F32),
                        pltpu.VMEM((nslab, halo, V7X_LANES), F32),
                        pltpu.VMEM((tc, hd), F32),
                        pltpu.VMEM((tc, hd), F32),
                        pltpu.VMEM((V7X_SUBLANES, w), F32)],
        compiler_params=_cparams(("arbitrary", "arbitrary")),
        name="lru_mixer",
    )(z, z, conv_w, conv_b.reshape(1, w), wax, bax, lam.reshape(1, w))


S5_GROUPS_PER_BLOCK = 8


def _cmul(ar, ai, br, bi):
    return ar * br - ai * bi, ar * bi + ai * br


def _s5_kernel(u_ref, lre_ref, lim_ref, ldt_ref, bre_ref, bim_ref, cre_ref, cim_ref, d_ref, gw_ref, gb_ref,
               o_ref, wb_ref, a_ref, am_ref, p_ref, carry_ref, up_ref, re_ref, im_ref, y_ref, yp_ref,
               *, tc, nblk, cin, nst, lane_chunk):
    s = pl.program_id(1)
    gn = nblk * nst
    sub = V7X_SUBLANES
    sub_len = tc // sub
    nslab = nblk * cin // V7X_LANES

    @pl.when(s == 0)
    def _():
        lr = lre_ref[...]
        li = lim_ref[...]
        dt = jnp.exp(ldt_ref[...])
        mag = jnp.exp(lr * dt)
        ar = mag * jnp.cos(li * dt)
        ai = mag * jnp.sin(li * dt)
        den = lr * lr + li * li
        zr = ar - 1.0
        cr = (zr * lr + ai * li) / den
        ci = (ai * lr - zr * li) / den
        for j in range(nblk):
            cols = slice(j * nst, (j + 1) * nst)
            br = bre_ref[j]
            bi = bim_ref[j]
            wb_ref[j, :, 0:nst] = (cr[:, cols] * br - ci[:, cols] * bi).astype(BF16)
            wb_ref[j, :, nst:2 * nst] = (cr[:, cols] * bi + ci[:, cols] * br).astype(BF16)
        a_ref[0] = jnp.broadcast_to(ar, (sub, gn))
        a_ref[1] = jnp.broadcast_to(ai, (sub, gn))
        qr, qi = ar, ai
        for _ in range(sub_len - 1):
            qr, qi = _cmul(qr, qi, ar, ai)
        row = lax.broadcasted_iota(jnp.int32, (sub, gn), 0)
        pr, pi = qr, qi
        accr = jnp.zeros((sub, gn), F32)
        acci = jnp.zeros((sub, gn), F32)
        powers = {}
        for r in range(sub):
            powers[r + 1] = (pr, pi)
            accr = jnp.where(row == r, pr, accr)
            acci = jnp.where(row == r, pi, acci)
            pr, pi = _cmul(pr, pi, qr, qi)
        p_ref[0] = accr
        p_ref[1] = acci
        for idx, dd in enumerate((1, 2, 4)):
            wr, wi = powers[dd]
            am_ref[2 * idx] = jnp.where(row >= dd, wr, 0.0)
            am_ref[2 * idx + 1] = jnp.where(row >= dd, wi, 0.0)
        carry_ref[...] = jnp.zeros_like(carry_ref)

    _interleave_rows(u_ref, up_ref, nslab, sub, sub_len)
    for j in range(nblk):
        bu = jnp.dot(up_ref[j].astype(BF16), wb_ref[j], preferred_element_type=F32)
        re_ref[:, j * nst:(j + 1) * nst] = bu[:, :nst]
        im_ref[:, j * nst:(j + 1) * nst] = bu[:, nst:]

    row_c = lax.broadcasted_iota(jnp.int32, (sub, lane_chunk), 0)
    for c in range(gn // lane_chunk):
        lsl = slice(c * lane_chunk, (c + 1) * lane_chunk)

        def pass1(t, carry, lsl=lsl):
            pr, pi = carry
            rows = pl.ds(pl.multiple_of(t * sub, sub), sub)
            ar = a_ref[0, :, lsl]
            ai = a_ref[1, :, lsl]
            hr = re_ref[rows, lsl] + (ar * pr - ai * pi)
            hi = im_ref[rows, lsl] + (ar * pi + ai * pr)
            re_ref[rows, lsl] = hr
            im_ref[rows, lsl] = hi
            return hr, hi

        zero = jnp.zeros((sub, lane_chunk), F32)
        fr, fi = lax.fori_loop(0, sub_len, pass1, (zero, zero), unroll=2)
        for idx, dd in enumerate((1, 2, 4)):
            mr = am_ref[2 * idx, :, lsl]
            mi = am_ref[2 * idx + 1, :, lsl]
            sr = pltpu.roll(fr, dd, 0)
            si = pltpu.roll(fi, dd, 0)
            fr, fi = fr + (mr * sr - mi * si), fi + (mr * si + mi * sr)
        cr_ = carry_ref[0, :, lsl]
        ci_ = carry_ref[1, :, lsl]
        pr_ = p_ref[0, :, lsl]
        pi_ = p_ref[1, :, lsl]
        er = fr + (pr_ * cr_ - pi_ * ci_)
        ei = fi + (pr_ * ci_ + pi_ * cr_)
        carry_ref[0, :, lsl] = _bcast_last(er)
        carry_ref[1, :, lsl] = _bcast_last(ei)
        sr0 = jnp.where(row_c == 0, cr_, pltpu.roll(er, 1, 0))
        si0 = jnp.where(row_c == 0, ci_, pltpu.roll(ei, 1, 0))

        def pass2(t, carry, lsl=lsl):
            qr, qi = carry
            rows = pl.ds(pl.multiple_of(t * sub, sub), sub)
            ar = a_ref[0, :, lsl]
            ai = a_ref[1, :, lsl]
            qr, qi = ar * qr - ai * qi, ar * qi + ai * qr
            re_ref[rows, lsl] = re_ref[rows, lsl] + qr
            im_ref[rows, lsl] = im_ref[rows, lsl] + qi
            return qr, qi

        lax.fori_loop(0, sub_len, pass2, (sr0, si0), unroll=2)

    for j in range(nblk):
        cols = slice(j * nst, (j + 1) * nst)
        yj = (jnp.dot(re_ref[:, cols].astype(BF16), cre_ref[j], preferred_element_type=F32)
              - jnp.dot(im_ref[:, cols].astype(BF16), cim_ref[j], preferred_element_type=F32))
        ucols = slice(j * cin, (j + 1) * cin)
        y_ref[:, ucols] = yj + d_ref[:, ucols] * up_ref[j]
    g = _gelu(y_ref[...])
    gate = _sigmoid(jnp.dot(g.astype(BF16), gw_ref[...], preferred_element_type=F32) + gb_ref[...])
    fin = g * gate
    for sl in range(nslab):
        yp_ref[sl] = fin[:, sl * V7X_LANES:(sl + 1) * V7X_LANES]
    _deinterleave_rows(yp_ref, o_ref, nslab, sub, sub_len)


def s5_mixer(u, lam_re, lam_im, log_dt, b_re, b_im, c_re, c_im, d, glu_w, glu_b, bsz, seq):
    t, ws = u.shape
    groups, nstate, gch = b_re.shape
    gpb = S5_GROUPS_PER_BLOCK
    nblk = groups // gpb
    cin = gpb * gch
    assert cin == V7X_LANES
    nst = gpb * nstate
    gn = groups * nstate
    tc = _tile(seq, 256)
    ns = seq // tc
    lane_chunk = _tile(gn, 1024)
    nslab = ws // V7X_LANES
    eye = jnp.eye(gpb, dtype=F32)

    def bdiag_in(b):
        bb = b.reshape(nblk, gpb, nstate, gch).transpose(0, 1, 3, 2)
        return (bb[:, :, :, None, :] * eye[None, :, None, :, None]).reshape(nblk, cin, nst)

    def bdiag_out(c):
        cc = c.reshape(nblk, gpb, gch, nstate).transpose(0, 1, 3, 2)
        return (cc[:, :, :, None, :] * eye[None, :, None, :, None]).reshape(nblk, nst, cin)

    const2 = lambda b, s: (0, 0)
    const3 = lambda b, s: (0, 0, 0)
    return pl.pallas_call(
        functools.partial(_s5_kernel, tc=tc, nblk=nblk, cin=cin, nst=nst, lane_chunk=lane_chunk),
        grid=(bsz, ns),
        in_specs=[pl.BlockSpec((tc, ws), lambda b, s: (b * ns + s, 0)),
                  pl.BlockSpec((1, gn), const2), pl.BlockSpec((1, gn), const2), pl.BlockSpec((1, gn), const2),
                  pl.BlockSpec((nblk, cin, nst), const3), pl.BlockSpec((nblk, cin, nst), const3),
                  pl.BlockSpec((nblk, nst, cin), const3), pl.BlockSpec((nblk, nst, cin), const3),
                  pl.BlockSpec((1, ws), const2),
                  pl.BlockSpec((ws, ws), const2),
                  pl.BlockSpec((1, ws), const2)],
        out_specs=pl.BlockSpec((tc, ws), lambda b, s: (b * ns + s, 0)),
        out_shape=jax.ShapeDtypeStruct((t, ws), BF16),
        scratch_shapes=[pltpu.VMEM((nblk, cin, 2 * nst), BF16),
                        pltpu.VMEM((2, V7X_SUBLANES, gn), F32),
                        pltpu.VMEM((6, V7X_SUBLANES, gn), F32),
                        pltpu.VMEM((2, V7X_SUBLANES, gn), F32),
                        pltpu.VMEM((2, V7X_SUBLANES, gn), F32),
                        pltpu.VMEM((nslab, tc, V7X_LANES), F32),
                        pltpu.VMEM((tc, gn), F32),
                        pltpu.VMEM((tc, gn), F32),
                        pltpu.VMEM((tc, ws), F32),
                        pltpu.VMEM((nslab, tc, V7X_LANES), F32)],
        compiler_params=_cparams(("arbitrary", "arbitrary")),
        name="s5_mixer",
    )(u, lam_re.reshape(1, gn), lam_im.reshape(1, gn),
      jnp.broadcast_to(log_dt[:, None], (groups, nstate)).reshape(1, gn),
      bdiag_in(b_re), bdiag_in(b_im), bdiag_out(c_re).astype(BF16), bdiag_out(c_im).astype(BF16),
      d.reshape(1, ws), glu_w.astype(BF16), glu_b.reshape(1, ws))


def _sgu_kernel(u_ref, v_ref, g_ref, b_ref, w_ref, bs_ref, o_ref, *, heads, hd, chunk, nchunk):
    v = _gelu(v_ref[...].astype(F32))
    mu = jnp.mean(v, axis=-1, keepdims=True)
    vc = v - mu
    var = jnp.mean(vc * vc, axis=-1, keepdims=True)
    vn = (vc * lax.rsqrt(var + EPS) * g_ref[...] + b_ref[...]).astype(BF16)
    r = lax.broadcasted_iota(jnp.int32, (chunk, chunk), 0)
    c = lax.broadcasted_iota(jnp.int32, (chunk, chunk), 1)
    tril = r >= c
    for h in range(heads):
        wh = jnp.where(tril, w_ref[h], 0.0).astype(BF16)
        cols = slice(h * hd, (h + 1) * hd)
        for n in range(nchunk):
            rows = slice(n * chunk, (n + 1) * chunk)
            gsp = jnp.dot(wh, vn[rows, cols], preferred_element_type=F32) + bs_ref[h]
            o_ref[rows, cols] = (_gelu(u_ref[rows, cols].astype(F32)) * gsp).astype(o_ref.dtype)


def sgu_mixer(z, ln_g, ln_b, w_s, b_s):
    t = z.shape[0]
    heads, chunk, _ = w_s.shape
    w = ln_g.shape[0]
    hd = w // heads
    nchunk = 2 if (t // chunk) % 2 == 0 else 1
    tm = nchunk * chunk
    bs = jnp.broadcast_to(b_s[:, :, None], (heads, chunk, hd))
    return pl.pallas_call(
        functools.partial(_sgu_kernel, heads=heads, hd=hd, chunk=chunk, nchunk=nchunk),
        grid=(t // tm,),
        in_specs=[pl.BlockSpec((tm, w), lambda i: (i, 0)),
                  pl.BlockSpec((tm, w), lambda i: (i, 1)),
                  pl.BlockSpec((1, w), lambda i: (0, 0)),
                  pl.BlockSpec((1, w), lambda i: (0, 0)),
                  pl.BlockSpec((heads, chunk, chunk), lambda i: (0, 0, 0)),
                  pl.BlockSpec((heads, chunk, hd), lambda i: (0, 0, 0))],
        out_specs=pl.BlockSpec((tm, w), lambda i: (i, 0)),
        out_shape=jax.ShapeDtypeStruct((t, w), BF16),
        compiler_params=_cparams(("arbitrary",)),
        name="sgu_mixer",
    )(z, z, ln_g.reshape(1, w), ln_b.reshape(1, w), w_s, bs)


def _qk_norm_kernel(x_ref, g_ref, o_ref, *, nseg, seg):
    x = x_ref[...].astype(F32)
    for i in range(nseg):
        cols = slice(i * seg, (i + 1) * seg)
        xs = x[:, cols]
        ms = jnp.mean(xs * xs, axis=-1, keepdims=True)
        o_ref[:, cols] = (xs * lax.rsqrt(ms + EPS) * g_ref[:, cols]).astype(o_ref.dtype)


def qk_norm(z, col_block, width, gain_row, seg):
    t = z.shape[0]
    tm = _tile(t, 512)
    return pl.pallas_call(
        functools.partial(_qk_norm_kernel, nseg=width // seg, seg=seg),
        grid=(t // tm,),
        in_specs=[pl.BlockSpec((tm, width), lambda i: (i, col_block)),
                  pl.BlockSpec((1, width), lambda i: (0, 0))],
        out_specs=pl.BlockSpec((tm, width), lambda i: (i, 0)),
        out_shape=jax.ShapeDtypeStruct((t, width), BF16),
        compiler_params=_cparams(("arbitrary",)),
        name="qk_norm",
    )(z, gain_row)


def _attn_kernel(q_ref, k_ref, v_ref, bias_ref, lq1_ref, lk1_ref, lq2_ref, lk2_ref, sg_ref, o_ref,
                 acc0_ref, acc1_ref, m0_ref, m1_ref, l0_ref, l1_ref, *, tq, dqk, lam_init):
    qi = pl.program_id(2)
    accs, ms, ls = (acc0_ref, acc1_ref), (m0_ref, m1_ref), (l0_ref, l1_ref)
    for mp in range(2):
        ms[mp][...] = jnp.full_like(ms[mp], NEG_INF)
        ls[mp][...] = jnp.zeros_like(ls[mp])
        accs[mp][...] = jnp.zeros_like(accs[mp])
    r = lax.broadcasted_iota(jnp.int32, (tq, tq), 0)
    c = lax.broadcasted_iota(jnp.int32, (tq, tq), 1)
    causal = r >= c
    krep = tq // V7X_LANES
    vrep = accs[0].shape[1] // V7X_LANES

    def block(kb, bias_idx, masked):
        k0 = pl.multiple_of(kb * tq, tq)
        v = v_ref[pl.ds(k0, tq), :]
        for mp in range(2):
            q = q_ref[:, mp * dqk:(mp + 1) * dqk]
            k = k_ref[pl.ds(k0, tq), mp * dqk:(mp + 1) * dqk]
            sc = lax.dot_general(q, k, (((1,), (1,)), ((), ())), preferred_element_type=F32)
            if bias_idx is not None:
                sc = sc + bias_ref[bias_idx]
            if masked:
                sc = jnp.where(causal, sc, NEG_INF)
            m_old = ms[mp][...]
            m_new = jnp.maximum(m_old, jnp.max(sc, axis=-1, keepdims=True))
            alpha = jnp.exp2(m_old - m_new)
            p = jnp.exp2(sc - jnp.tile(m_new, (1, krep)))
            ls[mp][...] = alpha * ls[mp][...] + jnp.sum(p, axis=-1, keepdims=True)
            accs[mp][...] = (jnp.tile(alpha, (1, vrep)) * accs[mp][...]
                             + jnp.dot(p.astype(BF16), v, preferred_element_type=F32))
            ms[mp][...] = m_new

    def far(kb, _):
        block(kb, None, False)
        return 0

    lax.fori_loop(0, jnp.maximum(qi - 1, 0), far, 0)

    @pl.when(qi > 0)
    def _():
        block(qi - 1, 1, False)

    block(qi, 0, True)

    lam = (jnp.exp(jnp.sum(lq1_ref[...] * lk1_ref[...], axis=-1, keepdims=True))
           - jnp.exp(jnp.sum(lq2_ref[...] * lk2_ref[...], axis=-1, keepdims=True)) + lam_init)
    o = (accs[0][...] * jnp.tile(1.0 / ls[0][...], (1, vrep))
         - lam * (accs[1][...] * jnp.tile(1.0 / ls[1][...], (1, vrep))))
    ms_o = jnp.mean(o * o, axis=-1, keepdims=True)
    o_ref[...] = (o * lax.rsqrt(ms_o + EPS) * sg_ref[...] * (1.0 - lam_init)).astype(o_ref.dtype)


def _t5_bucket(n, buckets):
    max_exact = buckets // 2
    nf = jnp.maximum(n, 1).astype(F32)
    large = max_exact + (jnp.log(nf / max_exact) / math.log(REL_MAX_DIST / max_exact)
                         * (buckets - max_exact)).astype(jnp.int32)
    large = jnp.minimum(large, buckets - 1)
    return jnp.where(n < max_exact, n, large)


def _toeplitz(w, rows, cols):
    nh, lw = w.shape
    flat = jnp.tile(w, (1, rows))[:, :rows * (lw - 1)]
    return flat.reshape(nh, rows, lw - 1)[:, :, :cols]


def diff_attention(qn, kn, zqkv, v_col_block, rel_bias, lq1, lk1, lq2, lk2, sub_g, bsz, seq, layer):
    t = qn.shape[0]
    buckets, heads = rel_bias.shape
    dv = sub_g.shape[0]
    dqk = dv // 2
    tq = _tile(seq, 512)
    assert tq >= REL_MAX_DIST, "far key blocks must all fall in the last relative-position bucket"
    nq = seq // tq
    lam_init = 0.8 - 0.6 * math.exp(-0.3 * layer)
    table = jnp.transpose(rel_bias.astype(F32))
    fvals = table[:, _t5_bucket(jnp.arange(2 * tq + 1, dtype=jnp.int32), buckets)]
    fvals = (fvals - fvals[:, 2 * tq:]) * LOG2E
    jj = np.arange(2 * tq)
    n_diag = np.where(jj <= tq, 0, 2 * tq - jj)
    n_near = np.where(jj < tq, tq - jj, 3 * tq - jj)
    bias = jnp.stack([_toeplitz(fvals[:, n_diag], tq, tq), _toeplitz(fvals[:, n_near], tq, tq)], axis=1)
    row = lambda b, h, i: (0, 0)
    return pl.pallas_call(
        functools.partial(_attn_kernel, tq=tq, dqk=dqk, lam_init=lam_init),
        grid=(bsz, heads, nq),
        in_specs=[pl.BlockSpec((tq, dv), lambda b, h, i: (b * nq + i, h)),
                  pl.BlockSpec((seq, dv), lambda b, h, i: (b, h)),
                  pl.BlockSpec((seq, dv), lambda b, h, i: (b, v_col_block + h)),
                  pl.BlockSpec((None, 2, tq, tq), lambda b, h, i: (h, 0, 0, 0)),
                  pl.BlockSpec((1, dqk), row), pl.BlockSpec((1, dqk), row),
                  pl.BlockSpec((1, dqk), row), pl.BlockSpec((1, dqk), row),
                  pl.BlockSpec((1, dv), row)],
        out_specs=pl.BlockSpec((tq, dv), lambda b, h, i: (b * nq + i, h)),
        out_shape=jax.ShapeDtypeStruct((t, heads * dv), BF16),
        scratch_shapes=[pltpu.VMEM((tq, dv), F32), pltpu.VMEM((tq, dv), F32),
                        pltpu.VMEM((tq, V7X_LANES), F32), pltpu.VMEM((tq, V7X_LANES), F32),
                        pltpu.VMEM((tq, V7X_LANES), F32), pltpu.VMEM((tq, V7X_LANES), F32)],
        compiler_params=_cparams(("arbitrary", "arbitrary", "arbitrary")),
        name="diff_attention",
    )(qn, kn, zqkv, bias, lq1.reshape(1, dqk), lk1.reshape(1, dqk), lq2.reshape(1, dqk), lk2.reshape(1, dqk),
      sub_g.reshape(1, dv))


ROUTER_LANES = 128


def _norm_mod(x, g_ref, sc_ref, sh_ref):
    ms = jnp.mean(x * x, axis=-1, keepdims=True)
    return (x * lax.rsqrt(ms + EPS) * g_ref[...]) * (1.0 + sc_ref[...]) + sh_ref[...]


def _router_kernel(x_ref, g_ref, sc_ref, sh_ref, wr_ref, wlo_ref, br_ref, route_ref, *, ngroups, per_group):
    h = _norm_mod(x_ref[...], g_ref, sc_ref, sh_ref)
    hi = h.astype(BF16)
    lo = (h - hi.astype(F32)).astype(BF16)
    both = jnp.dot(hi, wr_ref[...], preferred_element_type=F32)
    nl = ROUTER_LANES
    logits = both[:, :nl] + both[:, nl:] + jnp.dot(lo, wlo_ref[...], preferred_element_type=F32) + br_ref[...]
    lane = lax.broadcasted_iota(jnp.int32, logits.shape, 1).astype(F32)
    big = float(nl)
    glog = jnp.where(lane < ngroups, logits, NEG_INF)
    gmax = jnp.max(glog, axis=-1, keepdims=True)
    gsum = jnp.sum(jnp.exp(glog - gmax), axis=-1, keepdims=True)
    gp = 1.0 / gsum
    gidx = jnp.min(jnp.where(glog == gmax, lane, big), axis=-1, keepdims=True)
    lo_lane = ngroups + gidx * per_group
    emask = (lane >= lo_lane) & (lane < lo_lane + per_group)
    elog = jnp.where(emask, logits, NEG_INF)
    emax = jnp.max(elog, axis=-1, keepdims=True)
    eexp = jnp.where(emask, jnp.exp(elog - emax), -1.0)
    i0 = jnp.min(jnp.where(eexp == 1.0, lane, big), axis=-1, keepdims=True)
    rest = jnp.where(lane == i0, -1.0, eexp)
    p1 = jnp.max(rest, axis=-1, keepdims=True)
    i1 = jnp.min(jnp.where(rest == p1, lane, big), axis=-1, keepdims=True)
    denom = 1.0 + p1
    w0 = gp * (1.0 / denom)
    w1 = gp * (p1 / denom)
    e0 = i0 - ngroups
    e1 = i1 - ngroups
    route_ref[...] = jnp.where(lane == 0, e0, jnp.where(lane == 1, e1, jnp.where(lane == 2, w0,
                               jnp.where(lane == 3, w1, 0.0))))


def moe_router(x, g, sc, sh, wg, bg, we, be, seq):
    t, d = x.shape
    bsz = sc.shape[0]
    ngroups = wg.shape[-1]
    per_group = we.shape[-1]
    nexp = ngroups * per_group
    nl = ROUTER_LANES
    assert ngroups + nexp <= nl
    wr = jnp.concatenate([wg, jnp.transpose(we, (1, 0, 2)).reshape(d, nexp)], axis=-1)
    wr = jnp.zeros((d, nl), F32).at[:, :ngroups + nexp].set(wr)
    w_hi = wr.astype(BF16)
    w_lo = (wr - w_hi.astype(F32)).astype(BF16)
    br = jnp.zeros((1, nl), F32).at[0, :ngroups + nexp].set(jnp.concatenate([bg, be.reshape(nexp)]))
    tm = _tile(seq, 256)
    per = seq // tm
    return pl.pallas_call(
        functools.partial(_router_kernel, ngroups=ngroups, per_group=per_group),
        grid=(t // tm,),
        in_specs=[pl.BlockSpec((tm, d), lambda i: (i, 0)),
                  pl.BlockSpec((1, d), lambda i: (0, 0)),
                  pl.BlockSpec((None, 1, d), lambda i: (i // per, 0, 0)),
                  pl.BlockSpec((None, 1, d), lambda i: (i // per, 0, 0)),
                  pl.BlockSpec((d, 2 * nl), lambda i: (0, 0)),
                  pl.BlockSpec((d, nl), lambda i: (0, 0)),
                  pl.BlockSpec((1, nl), lambda i: (0, 0))],
        out_specs=pl.BlockSpec((tm, nl), lambda i: (i, 0)),
        out_shape=jax.ShapeDtypeStruct((t, nl), F32),
        compiler_params=_cparams(("arbitrary",)),
        name="moe_router",
    )(x, g.reshape(1, d), sc.reshape(bsz, 1, d), sh.reshape(bsz, 1, d),
      jnp.concatenate([w_hi, w_lo], axis=-1), w_hi, br)


HI16 = 0xFFFF0000


def _dispatch_kernel(pos_ref, zf_ref, x_ref, g_ref, sc_ref, sh_ref, xs_hbm, buf, zbuf, sem, zsem,
                     *, tm, tme, ntok, ntile, topk, half):
    i = pl.program_id(0)
    nsteps = pl.num_programs(0)
    slot = lax.rem(i, 2)

    @pl.when(i == 0)
    def _():
        zbuf[...] = jnp.zeros_like(zbuf)

        def zcopy(tl):
            return pltpu.make_async_copy(zbuf, xs_hbm.at[pl.ds(tl * tme, tme)], zsem)

        def zstart(tl, _):
            @pl.when(zf_ref[tl] > 0)
            def _():
                zcopy(tl).start()
            return 0

        def zwait(tl, _):
            @pl.when(zf_ref[tl] > 0)
            def _():
                zcopy(tl).wait()
            return 0

        lax.fori_loop(0, ntile, zstart, 0)
        lax.fori_loop(0, ntile, zwait, 0)

    def pack_rows(cidx, _):
        rows = pl.ds(pl.multiple_of(cidx * V7X_SUBLANES, V7X_SUBLANES), V7X_SUBLANES)
        h = _norm_mod(x_ref[rows, :], g_ref, sc_ref, sh_ref)
        bits = pltpu.bitcast(h.astype(BF16).astype(F32), jnp.uint32)
        buf[slot, rows, :] = (bits[:, half:] & jnp.uint32(HI16)) | (bits[:, :half] >> 16)
        return 0

    lax.fori_loop(0, tm // V7X_SUBLANES, pack_rows, 0, unroll=4)

    base = i * tm

    def issue(r, _):
        for kk in range(topk):
            pltpu.make_async_copy(buf.at[slot, pl.ds(r, 1)],
                                  xs_hbm.at[pl.ds(pos_ref[kk * ntok + base + r], 1)], sem.at[slot]).start()
        return 0

    lax.fori_loop(0, tm, issue, 0, unroll=4)

    def drain(s):
        for _ in range(topk):
            pltpu.make_async_copy(buf.at[s], xs_hbm.at[pl.ds(0, tm)], sem.at[s]).wait()

    @pl.when(i > 0)
    def _():
        drain(1 - slot)

    @pl.when(i == nsteps - 1)
    def _():
        drain(slot)


def _expert_kernel(texp_ref, tvalid_ref, xidx_ref, xs_ref, wg_ref, wu_ref, wd_ref, y_ref, *, half):
    i = pl.program_id(0)

    @pl.when(tvalid_ref[i] > 0)
    def _():
        xp = xs_ref[...]
        x_lo = pltpu.bitcast(xp << 16, F32).astype(BF16)
        x_hi = pltpu.bitcast(xp & jnp.uint32(HI16), F32).astype(BF16)
        hg = (jnp.dot(x_lo, wg_ref[0:half, :], preferred_element_type=F32)
              + jnp.dot(x_hi, wg_ref[half:, :], preferred_element_type=F32))
        hu = (jnp.dot(x_lo, wu_ref[0:half, :], preferred_element_type=F32)
              + jnp.dot(x_hi, wu_ref[half:, :], preferred_element_type=F32))
        act = (hg * _sigmoid(hg)) * hu
        y_ref[...] = jnp.dot(act.astype(BF16), wd_ref[...], preferred_element_type=F32)

    @pl.when(tvalid_ref[i] == 0)
    def _():
        y_ref[...] = jnp.zeros_like(y_ref)


def _combine_kernel(pos_ref, x_ref, route_ref, g_ref, y_hbm, o_ref, ybuf, sem, *, tm, ntok, topk):
    i = pl.program_id(0)
    nsteps = pl.num_programs(0)
    slot = lax.rem(i, 2)

    def gather(step, s):
        base = step * tm

        def issue(r, _):
            for kk in range(topk):
                pltpu.make_async_copy(y_hbm.at[pl.ds(pos_ref[kk * ntok + base + r], 1)],
                                      ybuf.at[s, kk, pl.ds(r, 1)], sem.at[s]).start()
            return 0

        lax.fori_loop(0, tm, issue, 0, unroll=4)

    @pl.when(i == 0)
    def _():
        gather(0, 0)

    @pl.when(i + 1 < nsteps)
    def _():
        gather(i + 1, 1 - slot)

    for kk in range(topk):
        pltpu.make_async_copy(y_hbm.at[pl.ds(0, tm)], ybuf.at[slot, kk], sem.at[slot]).wait()

    def mix_rows(cidx, _):
        rows = pl.ds(pl.multiple_of(cidx * V7X_SUBLANES, V7X_SUBLANES), V7X_SUBLANES)
        w = route_ref[rows, :]
        y = w[:, topk:topk + 1] * ybuf[slot, 0, rows, :]
        for kk in range(1, topk):
            y = y + w[:, topk + kk:topk + kk + 1] * ybuf[slot, kk, rows, :]
        o_ref[rows, :] = x_ref[rows, :] + g_ref[...] * y
        return 0

    lax.fori_loop(0, tm // V7X_SUBLANES, mix_rows, 0, unroll=4)


def moe_layer(x, route, g, sc, sh, gate, w_gate, w_up, w_down, layer, seq):
    t, d = x.shape
    bsz = sc.shape[0]
    _, nexp, _, f = w_gate.shape
    topk = MOE_TOPK
    half = d // 2
    npair = t * topk
    tme = _tile(npair // nexp, 256) if npair // nexp >= 8 else 8
    ntile = npair // tme + nexp
    tm = _tile(seq, 256)
    per = seq // tm

    flat_e = jnp.transpose(route[:, 0:topk]).astype(jnp.int32).reshape(npair)
    onehot = (flat_e[:, None] == jnp.arange(nexp, dtype=jnp.int32)[None, :]).astype(jnp.int32)
    csum = jnp.cumsum(onehot, axis=0)
    counts = csum[-1]
    tiles_per = (counts + tme - 1) // tme
    tile_end = jnp.cumsum(tiles_per)
    tile_start = tile_end - tiles_per
    pos = jnp.sum(onehot * (tile_start[None, :] * tme + csum - 1), axis=1).astype(jnp.int32)
    n_used = tile_end[-1]
    tile_id = jnp.arange(ntile, dtype=jnp.int32)
    active = tile_id < n_used
    texp = jnp.minimum(jnp.sum((tile_id[:, None] >= tile_end[None, :]).astype(jnp.int32), axis=1), nexp - 1)
    tvalid = jnp.where(active, jnp.clip(counts[texp] - (tile_id - tile_start[texp]) * tme, 0, tme), 0)
    tvalid = tvalid.astype(jnp.int32)
    texp = jnp.where(active, texp, texp[jnp.maximum(n_used - 1, 0)]).astype(jnp.int32)
    xidx = jnp.minimum(tile_id, jnp.maximum(n_used - 1, 0)).astype(jnp.int32)
    zflag = (tvalid < tme).astype(jnp.int32)

    mod_specs = [pl.BlockSpec((1, d), lambda i, *_: (0, 0)),
                 pl.BlockSpec((None, 1, d), lambda i, *_: (i // per, 0, 0)),
                 pl.BlockSpec((None, 1, d), lambda i, *_: (i // per, 0, 0))]
    xs = pl.pallas_call(
        functools.partial(_dispatch_kernel, tm=tm, tme=tme, ntok=t, ntile=ntile, topk=topk, half=half),
        grid_spec=pltpu.PrefetchScalarGridSpec(
            num_scalar_prefetch=2,
            grid=(t // tm,),
            in_specs=[pl.BlockSpec((tm, d), lambda i, *_: (i, 0))] + mod_specs,
            out_specs=pl.BlockSpec(memory_space=pl.ANY),
            scratch_shapes=[pltpu.VMEM((2, tm, half), jnp.uint32), pltpu.VMEM((tme, half), jnp.uint32),
                            pltpu.SemaphoreType.DMA((2,)), pltpu.SemaphoreType.DMA(())]),
        out_shape=jax.ShapeDtypeStruct((ntile * tme, half), jnp.uint32),
        compiler_params=_cparams(("arbitrary",)),
        name="moe_dispatch",
    )(pos, zflag, x, g.reshape(1, d), sc.reshape(bsz, 1, d), sh.reshape(bsz, 1, d))

    y = pl.pallas_call(
        functools.partial(_expert_kernel, half=half),
        grid_spec=pltpu.PrefetchScalarGridSpec(
            num_scalar_prefetch=3,
            grid=(ntile,),
            in_specs=[pl.BlockSpec((tme, half), lambda i, te, tv, xi: (xi[i], 0)),
                      pl.BlockSpec((None, None, d, f), lambda i, te, tv, xi: (layer, te[i], 0, 0)),
                      pl.BlockSpec((None, None, d, f), lambda i, te, tv, xi: (layer, te[i], 0, 0)),
                      pl.BlockSpec((None, None, f, d), lambda i, te, tv, xi: (layer, te[i], 0, 0))],
            out_specs=pl.BlockSpec((tme, d), lambda i, te, tv, xi: (i, 0))),
        out_shape=jax.ShapeDtypeStruct((ntile * tme, d), F32),
        compiler_params=_cparams(("arbitrary",)),
        name="moe_experts",
    )(texp, tvalid, xidx, xs, w_gate, w_up, w_down)

    return pl.pallas_call(
        functools.partial(_combine_kernel, tm=tm, ntok=t, topk=topk),
        grid_spec=pltpu.PrefetchScalarGridSpec(
            num_scalar_prefetch=1,
            grid=(t // tm,),
            in_specs=[pl.BlockSpec((tm, d), lambda i, *_: (i, 0)),
                      pl.BlockSpec((tm, ROUTER_LANES), lambda i, *_: (i, 0)),
                      pl.BlockSpec((None, 1, d), lambda i, *_: (i // per, 0, 0)),
                      pl.BlockSpec(memory_space=pl.ANY)],
            out_specs=pl.BlockSpec((tm, d), lambda i, *_: (i, 0)),
            scratch_shapes=[pltpu.VMEM((2, topk, tm, d), F32), pltpu.SemaphoreType.DMA((2,))]),
        out_shape=jax.ShapeDtypeStruct((t, d), F32),
        compiler_params=_cparams(("arbitrary",)),
        name="moe_combine",
    )(pos, x, route, gate.reshape(bsz, 1, d), y)


def kernel(x, c, norm1_g, norm2_g, ada_w, ada_b, ab_w_in, ab_w_out, lru_conv_w, lru_conv_b, lru_wa, lru_ba, lru_wx, lru_bx, lru_lambda, s5_lambda_re, s5_lambda_im, s5_log_dt, s5_b_re, s5_b_im, s5_c_re, s5_c_im, s5_d, s5_glu_w, s5_glu_b, cd_w_in, cd_w_out, sg_ln_g, sg_ln_b, sg_w, sg_b, da_q_norm, da_k_norm, da_lq1, da_lk1, da_lq2, da_lk2, da_sub_g, rel_bias, moe_wg, moe_bg, moe_we, moe_be, moe_w_gate, moe_w_up, moe_w_down):
    bsz, seq, d = x.shape
    depth = norm1_g.shape[0]
    t = bsz * seq
    xt = x.reshape(t, d)
    mod = ada_modulation(c, ada_w, ada_b)
    wg_all, wu_all, wd_all = moe_w_gate.astype(BF16), moe_w_up.astype(BF16), moe_w_down.astype(BF16)

    for layer in range(depth):
        sh1, sc1, g1, sh2, sc2, g2 = [mod[layer, :, i * d:(i + 1) * d] for i in range(6)]
        hmix = norm_modulate(xt, norm1_g[layer], sc1, sh1, seq)
        j = layer // 2
        if layer % 2 == 0:
            lw = lru_conv_w.shape[-1]
            w_in = ab_w_in[j].astype(BF16)
            w_out = ab_w_out[j].astype(BF16)
            z_lru = matmul([hmix], w_in, BF16, col_off=0, ncols=2 * lw)
            z_s5 = matmul([hmix], w_in, BF16, col_off=2 * lw)
            y_a = lru_mixer(z_lru, lru_conv_w[j], lru_conv_b[j], lru_wa[j], lru_ba[j], lru_wx[j], lru_bx[j],
                            lru_lambda[j], bsz, seq)
            y_b = s5_mixer(z_s5, s5_lambda_re[j], s5_lambda_im[j], s5_log_dt[j], s5_b_re[j], s5_b_im[j],
                           s5_c_re[j], s5_c_im[j], s5_d[j], s5_glu_w[j], s5_glu_b[j], bsz, seq)
            xt = matmul([y_a, y_b], w_out, F32, res=xt, gate=g1, seq=seq)
        else:
            sgw = sg_ln_g.shape[-1]
            dqk = da_q_norm.shape[-1]
            dv = da_sub_g.shape[-1]
            heads = rel_bias.shape[1]
            daw = heads * dv
            w_in = cd_w_in[j].astype(BF16)
            w_out = cd_w_out[j].astype(BF16)
            z_sg = matmul([hmix], w_in, BF16, col_off=0, ncols=2 * sgw)
            z_qkv = matmul([hmix], w_in, BF16, col_off=2 * sgw)
            y_c = sgu_mixer(z_sg, sg_ln_g[j], sg_ln_b[j], sg_w[j], sg_b[j])
            q_gain = jnp.tile(da_q_norm[j] * (dqk ** -0.5 * LOG2E), daw // dqk).reshape(1, daw)
            k_gain = jnp.tile(da_k_norm[j], daw // dqk).reshape(1, daw)
            qn = qk_norm(z_qkv, 0, daw, q_gain, dqk)
            kn = qk_norm(z_qkv, 1, daw, k_gain, dqk)
            y_d = diff_attention(qn, kn, z_qkv, 2 * daw // dv, rel_bias, da_lq1[j], da_lk1[j], da_lq2[j], da_lk2[j],
                                 da_sub_g[j], bsz, seq, layer)
            xt = matmul([y_c, y_d], w_out, F32, res=xt, gate=g1, seq=seq)
        route = moe_router(xt, norm2_g[layer], sc2, sh2, moe_wg[layer], moe_bg[layer], moe_we[layer], moe_be[layer],
                           seq)
        xt = moe_layer(xt, route, norm2_g[layer], sc2, sh2, g2, wg_all, wu_all, wd_all, layer, seq)
    return xt.reshape(bsz, seq, d)
```

```python
import functools
import math

import jax
import jax.numpy as jnp
import numpy as np
from jax import lax
from jax.experimental import pallas as pl
from jax.experimental.pallas import tpu as pltpu

F32 = jnp.float32
BF16 = jnp.bfloat16

EPS = 1e-6
LRU_C = 8.0
REL_MAX_DIST = 128
MOE_TOPK = 2
NEG_INF = -1e30
LOG2E = math.log2(math.e)

V7X_LANES = 128
V7X_SUBLANES = 8
V7X_VMEM_LIMIT_BYTES = 56 * 1024 * 1024


def _cparams(semantics):
    return pltpu.CompilerParams(dimension_semantics=semantics, vmem_limit_bytes=V7X_VMEM_LIMIT_BYTES)


def _sigmoid(x):
    return 0.5 * jnp.tanh(0.5 * x) + 0.5


def _gelu(x):
    return 0.5 * x * (1.0 + jnp.tanh(math.sqrt(2.0 / math.pi) * (x + 0.044715 * (x * x * x))))


def _tile(n, want):
    t = min(n, want)
    while n % t:
        t -= 1
    return t


def _ada_kernel(c_ref, w_ref, b_ref, o_ref):
    c = c_ref[...]
    cond = c * _sigmoid(c)
    o_ref[...] = jnp.dot(cond.astype(BF16), w_ref[...].astype(BF16), preferred_element_type=F32) + b_ref[...]


def ada_modulation(c, ada_w, ada_b):
    bsz, d = c.shape
    depth, _, n = ada_w.shape
    rows = 16
    cp = jnp.zeros((rows, d), F32).at[:bsz].set(c)
    tn = _tile(n, 512)
    out = pl.pallas_call(
        _ada_kernel,
        grid=(depth, n // tn),
        in_specs=[pl.BlockSpec((rows, d), lambda l, j: (0, 0)),
                  pl.BlockSpec((None, d, tn), lambda l, j: (l, 0, j)),
                  pl.BlockSpec((None, 1, tn), lambda l, j: (l, 0, j))],
        out_specs=pl.BlockSpec((None, rows, tn), lambda l, j: (l, 0, j)),
        out_shape=jax.ShapeDtypeStruct((depth, rows, n), F32),
        compiler_params=_cparams(("arbitrary", "arbitrary")),
        name="ada_modulation",
    )(cp, ada_w, ada_b.reshape(depth, 1, n))
    return out[:, :bsz]


def _norm_mod_kernel(x_ref, g_ref, sc_ref, sh_ref, o_ref):
    x = x_ref[...]
    ms = jnp.mean(x * x, axis=-1, keepdims=True)
    y = x * lax.rsqrt(ms + EPS) * g_ref[...]
    o_ref[...] = (y * (1.0 + sc_ref[...]) + sh_ref[...]).astype(o_ref.dtype)


def norm_modulate(x, g, sc, sh, seq, out_dtype=BF16):
    t, d = x.shape
    bsz = sc.shape[0]
    tm = _tile(seq, 256)
    per = seq // tm
    return pl.pallas_call(
        _norm_mod_kernel,
        grid=(t // tm,),
        in_specs=[pl.BlockSpec((tm, d), lambda i: (i, 0)),
                  pl.BlockSpec((1, d), lambda i: (0, 0)),
                  pl.BlockSpec((None, 1, d), lambda i: (i // per, 0, 0)),
                  pl.BlockSpec((None, 1, d), lambda i: (i // per, 0, 0))],
        out_specs=pl.BlockSpec((tm, d), lambda i: (i, 0)),
        out_shape=jax.ShapeDtypeStruct((t, d), out_dtype),
        compiler_params=_cparams(("arbitrary",)),
        name="norm_modulate",
    )(x, g.reshape(1, d), sc.reshape(bsz, 1, d), sh.reshape(bsz, 1, d))


def _matmul_kernel(*refs, ksplit, has_res):
    n_lhs = len(ksplit)
    a_refs = refs[:n_lhs]
    w_ref = refs[n_lhs]
    o_ref = refs[-1]
    acc = None
    k0 = 0
    for a_ref, kk in zip(a_refs, ksplit):
        part = jnp.dot(a_ref[...], w_ref[k0:k0 + kk, :], preferred_element_type=F32)
        acc = part if acc is None else acc + part
        k0 += kk
    if has_res:
        res_ref, gate_ref = refs[n_lhs + 1], refs[n_lhs + 2]
        acc = res_ref[...] + gate_ref[...] * acc
    o_ref[...] = acc.astype(o_ref.dtype)


def matmul(lhs, w, out_dtype, col_off=0, ncols=None, res=None, gate=None, seq=None):
    m = lhs[0].shape[0]
    ktot = w.shape[0]
    ksplit = tuple(a.shape[1] for a in lhs)
    assert sum(ksplit) == ktot
    n = w.shape[1] - col_off if ncols is None else ncols
    tm = _tile(m if seq is None else seq, 1024)
    tn = _tile(math.gcd(n, col_off) if col_off else n, 512)
    joff = col_off // tn
    in_specs = [pl.BlockSpec((tm, kk), lambda i, j: (i, 0)) for kk in ksplit]
    in_specs += [pl.BlockSpec((ktot, tn), lambda i, j: (0, j + joff))]
    args = list(lhs) + [w]
    if res is not None:
        per = seq // tm
        bsz = gate.shape[0]
        in_specs += [pl.BlockSpec((tm, tn), lambda i, j: (i, j)),
                     pl.BlockSpec((None, 1, tn), lambda i, j: (i // per, 0, j))]
        args += [res, gate.reshape(bsz, 1, n)]
    return pl.pallas_call(
        functools.partial(_matmul_kernel, ksplit=ksplit, has_res=res is not None),
        grid=(m // tm, n // tn),
        in_specs=in_specs,
        out_specs=pl.BlockSpec((tm, tn), lambda i, j: (i, j)),
        out_shape=jax.ShapeDtypeStruct((m, n), out_dtype),
        compiler_params=_cparams(("arbitrary", "arbitrary")),
        name="matmul",
    )(*args)


def _interleave_rows(src_ref, dst_ref, nslab, nsub, sub_len):
    for sl in range(nslab):
        lanes = slice(sl * V7X_LANES, (sl + 1) * V7X_LANES)
        for r in range(nsub):
            dst_ref[sl, pl.ds(r, sub_len, stride=nsub), :] = src_ref[r * sub_len:(r + 1) * sub_len, lanes].astype(F32)


def _deinterleave_rows(src_ref, dst_ref, nslab, nsub, sub_len):
    for sl in range(nslab):
        lanes = slice(sl * V7X_LANES, (sl + 1) * V7X_LANES)
        for r in range(nsub):
            dst_ref[r * sub_len:(r + 1) * sub_len, lanes] = src_ref[sl, pl.ds(r, sub_len, stride=nsub), :].astype(
                dst_ref.dtype)


def _sublane_scan(a, b, row):
    for d in (1, 2, 4):
        keep = row >= d
        sa = jnp.where(keep, pltpu.roll(a, d, 0), 1.0)
        sb = jnp.where(keep, pltpu.roll(b, d, 0), 0.0)
        b = b + a * sb
        a = a * sa
    return a, b


def _bcast_last(x):
    return jnp.broadcast_to(x[V7X_SUBLANES - 1:V7X_SUBLANES, :], x.shape)


def _lru_kernel(x_ref, gate_ref, cw_ref, cb_ref, wax_ref, bax_ref, lam_ref, o_ref,
                xp_ref, gp_ref, yp_ref, tail_ref, a_ref, b_ref, carry_ref, *, heads, hd, tc, kconv):
    s = pl.program_id(1)
    sub = V7X_SUBLANES
    sub_len = tc // sub
    nslab = heads * hd // V7X_LANES
    spl = hd // V7X_LANES
    halo = (kconv - 1) * sub
    row = lax.broadcasted_iota(jnp.int32, (sub, V7X_LANES), 0)
    rowh = lax.broadcasted_iota(jnp.int32, (sub, hd), 0)

    @pl.when(s == 0)
    def _():
        tail_ref[...] = jnp.zeros_like(tail_ref)
        carry_ref[...] = jnp.zeros_like(carry_ref)

    for sl in range(nslab):
        lanes = slice(sl * V7X_LANES, (sl + 1) * V7X_LANES)
        for r in range(sub):
            xp_ref[sl, pl.ds(halo + r, sub_len, stride=sub), :] = x_ref[r * sub_len:(r + 1) * sub_len, lanes].astype(F32)
        for e in range(kconv - 1):
            cur = xp_ref[sl, halo + (sub_len - (kconv - 1) + e) * sub:halo + (sub_len - (kconv - 1) + e + 1) * sub, :]
            prev = tail_ref[sl, e * sub:(e + 1) * sub, :]
            xp_ref[sl, e * sub:(e + 1) * sub, :] = jnp.where(row == 0, pltpu.roll(prev, 1, 0), pltpu.roll(cur, 1, 0))
            tail_ref[sl, e * sub:(e + 1) * sub, :] = cur
    _interleave_rows(gate_ref, gp_ref, nslab, sub, sub_len)

    for h in range(heads):
        cols = slice(h * hd, (h + 1) * hd)
        parts = []
        for q in range(spl):
            sl = h * spl + q
            lanes = slice(sl * V7X_LANES, (sl + 1) * V7X_LANES)
            acc = cb_ref[:, lanes] + cw_ref[0:1, lanes] * xp_ref[sl, 0:tc, :]
            for k in range(1, kconv):
                acc = acc + cw_ref[k:k + 1, lanes] * xp_ref[sl, k * sub:k * sub + tc, :]
            parts.append(acc)
        xc = jnp.concatenate(parts, axis=-1) if spl > 1 else parts[0]
        pre = jnp.dot(xc.astype(BF16), wax_ref[h], preferred_element_type=F32) + bax_ref[h]
        r = _sigmoid(pre[:, :hd])
        gi = _sigmoid(pre[:, hd:])
        nl = -lam_ref[:, cols]
        sp = jnp.maximum(nl, 0.0) + jnp.log1p(jnp.exp(-jnp.abs(nl)))
        log_a = (-LRU_C) * r * sp
        a_ref[...] = jnp.exp(log_a)
        b_ref[...] = jnp.sqrt(1.0 - jnp.exp(2.0 * log_a)) * (gi * xc)

        def pass1(t, carry):
            hprev, pprev = carry
            rows = pl.ds(pl.multiple_of(t * sub, sub), sub)
            a = a_ref[rows, :]
            hh = a * hprev + b_ref[rows, :]
            pp = a * pprev
            b_ref[rows, :] = hh
            a_ref[rows, :] = pp
            return hh, pp

        zero = jnp.zeros((sub, hd), F32)
        hfin, pfin = lax.fori_loop(0, sub_len, pass1, (zero, zero + 1.0), unroll=4)
        cin = carry_ref[:, cols]
        ptot, hloc = _sublane_scan(pfin, hfin, rowh)
        hend = hloc + ptot * cin
        carry_ref[:, cols] = _bcast_last(hend)
        entry = jnp.where(rowh == 0, cin, pltpu.roll(hend, 1, 0))

        def pass2(t, _):
            rows = pl.ds(pl.multiple_of(t * sub, sub), sub)
            hh = b_ref[rows, :] + a_ref[rows, :] * entry
            for q in range(spl):
                sl = h * spl + q
                yp_ref[sl, rows, :] = _gelu(gp_ref[sl, rows, :]) * hh[:, q * V7X_LANES:(q + 1) * V7X_LANES]
            return 0

        lax.fori_loop(0, sub_len, pass2, 0, unroll=4)

    _deinterleave_rows(yp_ref, o_ref, nslab, sub, sub_len)


def lru_mixer(z, conv_w, conv_b, wa, ba, wx, bx, lam, bsz, seq):
    t = z.shape[0]
    heads, hd, _ = wa.shape
    w = heads * hd
    kconv = conv_w.shape[0]
    tc = _tile(seq, 256)
    ns = seq // tc
    nslab = w // V7X_LANES
    halo = (kconv - 1) * V7X_SUBLANES
    wax = jnp.concatenate([wa, wx], axis=-1).astype(BF16)
    bax = jnp.concatenate([ba.reshape(heads, 1, hd), bx.reshape(heads, 1, hd)], axis=-1)
    return pl.pallas_call(
        functools.partial(_lru_kernel, heads=heads, hd=hd, tc=tc, kconv=kconv),
        grid=(bsz, ns),
        in_specs=[pl.BlockSpec((tc, w), lambda b, s: (b * ns + s, 0)),
                  pl.BlockSpec((tc, w), lambda b, s: (b * ns + s, 1)),
                  pl.BlockSpec((kconv, w), lambda b, s: (0, 0)),
                  pl.BlockSpec((1, w), lambda b, s: (0, 0)),
                  pl.BlockSpec((heads, hd, 2 * hd), lambda b, s: (0, 0, 0)),
                  pl.BlockSpec((heads, 1, 2 * hd), lambda b, s: (0, 0, 0)),
                  pl.BlockSpec((1, w), lambda b, s: (0, 0))],
        out_specs=pl.BlockSpec((tc, w), lambda b, s: (b * ns + s, 0)),
        out_shape=jax.ShapeDtypeStruct((t, w), BF16),
        scratch_shapes=[pltpu.VMEM((nslab, halo + tc, V7X_LANES), F32),
                        pltpu.VMEM((nslab, tc, V7X_LANES), F32),
                        pltpu.VMEM((nslab, tc, V7X_LANES), F32),
                        pltpu.VMEM((nslab, halo, V7X_LANES), F32),
                        pltpu.VMEM((tc, hd), F32),
                        pltpu.VMEM((tc, hd), F32),
                        pltpu.VMEM((V7X_SUBLANES, w), F32)],
        compiler_params=_cparams(("arbitrary", "arbitrary")),
        name="lru_mixer",
    )(z, z, conv_w, conv_b.reshape(1, w), wax, bax, lam.reshape(1, w))


S5_GROUPS_PER_BLOCK = 8


def _cmul(ar, ai, br, bi):
    return ar * br - ai * bi, ar * bi + ai * br


def _s5_kernel(u_ref, lre_ref, lim_ref, ldt_ref, bre_ref, bim_ref, cre_ref, cim_ref, d_ref, gw_ref, gb_ref,
               o_ref, wb_ref, a_ref, am_ref, p_ref, carry_ref, up_ref, re_ref, im_ref, y_ref, yp_ref,
               *, tc, nblk, cin, nst, lane_chunk):
    s = pl.program_id(1)
    gn = nblk * nst
    sub = V7X_SUBLANES
    sub_len = tc // sub
    nslab = nblk * cin // V7X_LANES

    @pl.when(s == 0)
    def _():
        lr = lre_ref[...]
        li = lim_ref[...]
        dt = jnp.exp(ldt_ref[...])
        mag = jnp.exp(lr * dt)
        ar = mag * jnp.cos(li * dt)
        ai = mag * jnp.sin(li * dt)
        den = lr * lr + li * li
        zr = ar - 1.0
        cr = (zr * lr + ai * li) / den
        ci = (ai * lr - zr * li) / den
        for j in range(nblk):
            cols = slice(j * nst, (j + 1) * nst)
            br = bre_ref[j]
            bi = bim_ref[j]
            wb_ref[j, :, 0:nst] = (cr[:, cols] * br - ci[:, cols] * bi).astype(BF16)
            wb_ref[j, :, nst:2 * nst] = (cr[:, cols] * bi + ci[:, cols] * br).astype(BF16)
        a_ref[0] = jnp.broadcast_to(ar, (sub, gn))
        a_ref[1] = jnp.broadcast_to(ai, (sub, gn))
        qr, qi = ar, ai
        for _ in range(sub_len - 1):
            qr, qi = _cmul(qr, qi, ar, ai)
        row = lax.broadcasted_iota(jnp.int32, (sub, gn), 0)
        pr, pi = qr, qi
        accr = jnp.zeros((sub, gn), F32)
        acci = jnp.zeros((sub, gn), F32)
        powers = {}
        for r in range(sub):
            powers[r + 1] = (pr, pi)
            accr = jnp.where(row == r, pr, accr)
            acci = jnp.where(row == r, pi, acci)
            pr, pi = _cmul(pr, pi, qr, qi)
        p_ref[0] = accr
        p_ref[1] = acci
        for idx, dd in enumerate((1, 2, 4)):
            wr, wi = powers[dd]
            am_ref[2 * idx] = jnp.where(row >= dd, wr, 0.0)
            am_ref[2 * idx + 1] = jnp.where(row >= dd, wi, 0.0)
        carry_ref[...] = jnp.zeros_like(carry_ref)

    _interleave_rows(u_ref, up_ref, nslab, sub, sub_len)
    for j in range(nblk):
        bu = jnp.dot(up_ref[j].astype(BF16), wb_ref[j], preferred_element_type=F32)
        re_ref[:, j * nst:(j + 1) * nst] = bu[:, :nst]
        im_ref[:, j * nst:(j + 1) * nst] = bu[:, nst:]

    row_c = lax.broadcasted_iota(jnp.int32, (sub, lane_chunk), 0)
    for c in range(gn // lane_chunk):
        lsl = slice(c * lane_chunk, (c + 1) * lane_chunk)

        def pass1(t, carry, lsl=lsl):
            pr, pi = carry
            rows = pl.ds(pl.multiple_of(t * sub, sub), sub)
            ar = a_ref[0, :, lsl]
            ai = a_ref[1, :, lsl]
            hr = re_ref[rows, lsl] + (ar * pr - ai * pi)
            hi = im_ref[rows, lsl] + (ar * pi + ai * pr)
            re_ref[rows, lsl] = hr
            im_ref[rows, lsl] = hi
            return hr, hi

        zero = jnp.zeros((sub, lane_chunk), F32)
        fr, fi = lax.fori_loop(0, sub_len, pass1, (zero, zero), unroll=2)
        for idx, dd in enumerate((1, 2, 4)):
            mr = am_ref[2 * idx, :, lsl]
            mi = am_ref[2 * idx + 1, :, lsl]
            sr = pltpu.roll(fr, dd, 0)
            si = pltpu.roll(fi, dd, 0)
            fr, fi = fr + (mr * sr - mi * si), fi + (mr * si + mi * sr)
        cr_ = carry_ref[0, :, lsl]
        ci_ = carry_ref[1, :, lsl]
        pr_ = p_ref[0, :, lsl]
        pi_ = p_ref[1, :, lsl]
        er = fr + (pr_ * cr_ - pi_ * ci_)
        ei = fi + (pr_ * ci_ + pi_ * cr_)
        carry_ref[0, :, lsl] = _bcast_last(er)
        carry_ref[1, :, lsl] = _bcast_last(ei)
        sr0 = jnp.where(row_c == 0, cr_, pltpu.roll(er, 1, 0))
        si0 = jnp.where(row_c == 0, ci_, pltpu.roll(ei, 1, 0))

        def pass2(t, carry, lsl=lsl):
            qr, qi = carry
            rows = pl.ds(pl.multiple_of(t * sub, sub), sub)
            ar = a_ref[0, :, lsl]
            ai = a_ref[1, :, lsl]
            qr, qi = ar * qr - ai * qi, ar * qi + ai * qr
            re_ref[rows, lsl] = re_ref[rows, lsl] + qr
            im_ref[rows, lsl] = im_ref[rows, lsl] + qi
            return qr, qi

        lax.fori_loop(0, sub_len, pass2, (sr0, si0), unroll=2)

    for j in range(nblk):
        cols = slice(j * nst, (j + 1) * nst)
        yj = (jnp.dot(re_ref[:, cols].astype(BF16), cre_ref[j], preferred_element_type=F32)
              - jnp.dot(im_ref[:, cols].astype(BF16), cim_ref[j], preferred_element_type=F32))
        ucols = slice(j * cin, (j + 1) * cin)
        y_ref[:, ucols] = yj + d_ref[:, ucols] * up_ref[j]
    g = _gelu(y_ref[...])
    gate = _sigmoid(jnp.dot(g.astype(BF16), gw_ref[...], preferred_element_type=F32) + gb_ref[...])
    fin = g * gate
    for sl in range(nslab):
        yp_ref[sl] = fin[:, sl * V7X_LANES:(sl + 1) * V7X_LANES]
    _deinterleave_rows(yp_ref, o_ref, nslab, sub, sub_len)


def s5_mixer(u, lam_re, lam_im, log_dt, b_re, b_im, c_re, c_im, d, glu_w, glu_b, bsz, seq):
    t, ws = u.shape
    groups, nstate, gch = b_re.shape
    gpb = S5_GROUPS_PER_BLOCK
    nblk = groups // gpb
    cin = gpb * gch
    assert cin == V7X_LANES
    nst = gpb * nstate
    gn = groups * nstate
    tc = _tile(seq, 256)
    ns = seq // tc
    lane_chunk = _tile(gn, 1024)
    nslab = ws // V7X_LANES
    eye = jnp.eye(gpb, dtype=F32)

    def bdiag_in(b):
        bb = b.reshape(nblk, gpb, nstate, gch).transpose(0, 1, 3, 2)
        return (bb[:, :, :, None, :] * eye[None, :, None, :, None]).reshape(nblk, cin, nst)

    def bdiag_out(c):
        cc = c.reshape(nblk, gpb, gch, nstate).transpose(0, 1, 3, 2)
        return (cc[:, :, :, None, :] * eye[None, :, None, :, None]).reshape(nblk, nst, cin)

    const2 = lambda b, s: (0, 0)
    const3 = lambda b, s: (0, 0, 0)
    return pl.pallas_call(
        functools.partial(_s5_kernel, tc=tc, nblk=nblk, cin=cin, nst=nst, lane_chunk=lane_chunk),
        grid=(bsz, ns),
        in_specs=[pl.BlockSpec((tc, ws), lambda b, s: (b * ns + s, 0)),
                  pl.BlockSpec((1, gn), const2), pl.BlockSpec((1, gn), const2), pl.BlockSpec((1, gn), const2),
                  pl.BlockSpec((nblk, cin, nst), const3), pl.BlockSpec((nblk, cin, nst), const3),
                  pl.BlockSpec((nblk, nst, cin), const3), pl.BlockSpec((nblk, nst, cin), const3),
                  pl.BlockSpec((1, ws), const2),
                  pl.BlockSpec((ws, ws), const2),
                  pl.BlockSpec((1, ws), const2)],
        out_specs=pl.BlockSpec((tc, ws), lambda b, s: (b * ns + s, 0)),
        out_shape=jax.ShapeDtypeStruct((t, ws), BF16),
        scratch_shapes=[pltpu.VMEM((nblk, cin, 2 * nst), BF16),
                        pltpu.VMEM((2, V7X_SUBLANES, gn), F32),
                        pltpu.VMEM((6, V7X_SUBLANES, gn), F32),
                        pltpu.VMEM((2, V7X_SUBLANES, gn), F32),
                        pltpu.VMEM((2, V7X_SUBLANES, gn), F32),
                        pltpu.VMEM((nslab, tc, V7X_LANES), F32),
                        pltpu.VMEM((tc, gn), F32),
                        pltpu.VMEM((tc, gn), F32),
                        pltpu.VMEM((tc, ws), F32),
                        pltpu.VMEM((nslab, tc, V7X_LANES), F32)],
        compiler_params=_cparams(("arbitrary", "arbitrary")),
        name="s5_mixer",
    )(u, lam_re.reshape(1, gn), lam_im.reshape(1, gn),
      jnp.broadcast_to(log_dt[:, None], (groups, nstate)).reshape(1, gn),
      bdiag_in(b_re), bdiag_in(b_im), bdiag_out(c_re).astype(BF16), bdiag_out(c_im).astype(BF16),
      d.reshape(1, ws), glu_w.astype(BF16), glu_b.reshape(1, ws))


def _sgu_kernel(u_ref, v_ref, g_ref, b_ref, w_ref, bs_ref, o_ref, *, heads, hd, chunk, nchunk):
    v = _gelu(v_ref[...].astype(F32))
    mu = jnp.mean(v, axis=-1, keepdims=True)
    vc = v - mu
    var = jnp.mean(vc * vc, axis=-1, keepdims=True)
    vn = (vc * lax.rsqrt(var + EPS) * g_ref[...] + b_ref[...]).astype(BF16)
    r = lax.broadcasted_iota(jnp.int32, (chunk, chunk), 0)
    c = lax.broadcasted_iota(jnp.int32, (chunk, chunk), 1)
    tril = r >= c
    for h in range(heads):
        wh = jnp.where(tril, w_ref[h], 0.0).astype(BF16)
        cols = slice(h * hd, (h + 1) * hd)
        for n in range(nchunk):
            rows = slice(n * chunk, (n + 1) * chunk)
            gsp = jnp.dot(wh, vn[rows, cols], preferred_element_type=F32) + bs_ref[h]
            o_ref[rows, cols] = (_gelu(u_ref[rows, cols].astype(F32)) * gsp).astype(o_ref.dtype)


def sgu_mixer(z, ln_g, ln_b, w_s, b_s):
    t = z.shape[0]
    heads, chunk, _ = w_s.shape
    w = ln_g.shape[0]
    hd = w // heads
    nchunk = 2 if (t // chunk) % 2 == 0 else 1
    tm = nchunk * chunk
    bs = jnp.broadcast_to(b_s[:, :, None], (heads, chunk, hd))
    return pl.pallas_call(
        functools.partial(_sgu_kernel, heads=heads, hd=hd, chunk=chunk, nchunk=nchunk),
        grid=(t // tm,),
        in_specs=[pl.BlockSpec((tm, w), lambda i: (i, 0)),
                  pl.BlockSpec((tm, w), lambda i: (i, 1)),
                  pl.BlockSpec((1, w), lambda i: (0, 0)),
                  pl.BlockSpec((1, w), lambda i: (0, 0)),
                  pl.BlockSpec((heads, chunk, chunk), lambda i: (0, 0, 0)),
                  pl.BlockSpec((heads, chunk, hd), lambda i: (0, 0, 0))],
        out_specs=pl.BlockSpec((tm, w), lambda i: (i, 0)),
        out_shape=jax.ShapeDtypeStruct((t, w), BF16),
        compiler_params=_cparams(("arbitrary",)),
        name="sgu_mixer",
    )(z, z, ln_g.reshape(1, w), ln_b.reshape(1, w), w_s, bs)


def _qk_norm_kernel(x_ref, g_ref, o_ref, *, nseg, seg):
    x = x_ref[...].astype(F32)
    for i in range(nseg):
        cols = slice(i * seg, (i + 1) * seg)
        xs = x[:, cols]
        ms = jnp.mean(xs * xs, axis=-1, keepdims=True)
        o_ref[:, cols] = (xs * lax.rsqrt(ms + EPS) * g_ref[:, cols]).astype(o_ref.dtype)


def qk_norm(z, col_block, width, gain_row, seg):
    t = z.shape[0]
    tm = _tile(t, 512)
    return pl.pallas_call(
        functools.partial(_qk_norm_kernel, nseg=width // seg, seg=seg),
        grid=(t // tm,),
        in_specs=[pl.BlockSpec((tm, width), lambda i: (i, col_block)),
                  pl.BlockSpec((1, width), lambda i: (0, 0))],
        out_specs=pl.BlockSpec((tm, width), lambda i: (i, 0)),
        out_shape=jax.ShapeDtypeStruct((t, width), BF16),
        compiler_params=_cparams(("arbitrary",)),
        name="qk_norm",
    )(z, gain_row)


def _attn_kernel(q_ref, k_ref, v_ref, bias_ref, lq1_ref, lk1_ref, lq2_ref, lk2_ref, sg_ref, o_ref,
                 acc0_ref, acc1_ref, m0_ref, m1_ref, l0_ref, l1_ref, *, tq, dqk, lam_init):
    qi = pl.program_id(2)
    accs, ms, ls = (acc0_ref, acc1_ref), (m0_ref, m1_ref), (l0_ref, l1_ref)
    for mp in range(2):
        ms[mp][...] = jnp.full_like(ms[mp], NEG_INF)
        ls[mp][...] = jnp.zeros_like(ls[mp])
        accs[mp][...] = jnp.zeros_like(accs[mp])
    r = lax.broadcasted_iota(jnp.int32, (tq, tq), 0)
    c = lax.broadcasted_iota(jnp.int32, (tq, tq), 1)
    causal = r >= c
    krep = tq // V7X_LANES
    vrep = accs[0].shape[1] // V7X_LANES

    def block(kb, bias_idx, masked):
        k0 = pl.multiple_of(kb * tq, tq)
        v = v_ref[pl.ds(k0, tq), :]
        for mp in range(2):
            q = q_ref[:, mp * dqk:(mp + 1) * dqk]
            k = k_ref[pl.ds(k0, tq), mp * dqk:(mp + 1) * dqk]
            sc = lax.dot_general(q, k, (((1,), (1,)), ((), ())), preferred_element_type=F32)
            if bias_idx is not None:
                sc = sc + bias_ref[bias_idx]
            if masked:
                sc = jnp.where(causal, sc, NEG_INF)
            m_old = ms[mp][...]
            m_new = jnp.maximum(m_old, jnp.max(sc, axis=-1, keepdims=True))
            alpha = jnp.exp2(m_old - m_new)
            p = jnp.exp2(sc - jnp.tile(m_new, (1, krep)))
            ls[mp][...] = alpha * ls[mp][...] + jnp.sum(p, axis=-1, keepdims=True)
            accs[mp][...] = (jnp.tile(alpha, (1, vrep)) * accs[mp][...]
                             + jnp.dot(p.astype(BF16), v, preferred_element_type=F32))
            ms[mp][...] = m_new

    def far(kb, _):
        block(kb, None, False)
        return 0

    lax.fori_loop(0, jnp.maximum(qi - 1, 0), far, 0)

    @pl.when(qi > 0)
    def _():
        block(qi - 1, 1, False)

    block(qi, 0, True)

    lam = (jnp.exp(jnp.sum(lq1_ref[...] * lk1_ref[...], axis=-1, keepdims=True))
           - jnp.exp(jnp.sum(lq2_ref[...] * lk2_ref[...], axis=-1, keepdims=True)) + lam_init)
    o = (accs[0][...] * jnp.tile(1.0 / ls[0][...], (1, vrep))
         - lam * (accs[1][...] * jnp.tile(1.0 / ls[1][...], (1, vrep))))
    ms_o = jnp.mean(o * o, axis=-1, keepdims=True)
    o_ref[...] = (o * lax.rsqrt(ms_o + EPS) * sg_ref[...] * (1.0 - lam_init)).astype(o_ref.dtype)


def _t5_bucket(n, buckets):
    max_exact = buckets // 2
    nf = jnp.maximum(n, 1).astype(F32)
    large = max_exact + (jnp.log(nf / max_exact) / math.log(REL_MAX_DIST / max_exact)
                         * (buckets - max_exact)).astype(jnp.int32)
    large = jnp.minimum(large, buckets - 1)
    return jnp.where(n < max_exact, n, large)


def _toeplitz(w, rows, cols):
    nh, lw = w.shape
    flat = jnp.tile(w, (1, rows))[:, :rows * (lw - 1)]
    return flat.reshape(nh, rows, lw - 1)[:, :, :cols]


def diff_attention(qn, kn, zqkv, v_col_block, rel_bias, lq1, lk1, lq2, lk2, sub_g, bsz, seq, layer):
    t = qn.shape[0]
    buckets, heads = rel_bias.shape
    dv = sub_g.shape[0]
    dqk = dv // 2
    tq = _tile(seq, 512)
    assert tq >= REL_MAX_DIST, "far key blocks must all fall in the last relative-position bucket"
    nq = seq // tq
    lam_init = 0.8 - 0.6 * math.exp(-0.3 * layer)
    table = jnp.transpose(rel_bias.astype(F32))
    fvals = table[:, _t5_bucket(jnp.arange(2 * tq + 1, dtype=jnp.int32), buckets)]
    fvals = (fvals - fvals[:, 2 * tq:]) * LOG2E
    jj = np.arange(2 * tq)
    n_diag = np.where(jj <= tq, 0, 2 * tq - jj)
    n_near = np.where(jj < tq, tq - jj, 3 * tq - jj)
    bias = jnp.stack([_toeplitz(fvals[:, n_diag], tq, tq), _toeplitz(fvals[:, n_near], tq, tq)], axis=1)
    row = lambda b, h, i: (0, 0)
    return pl.pallas_call(
        functools.partial(_attn_kernel, tq=tq, dqk=dqk, lam_init=lam_init),
        grid=(bsz, heads, nq),
        in_specs=[pl.BlockSpec((tq, dv), lambda b, h, i: (b * nq + i, h)),
                  pl.BlockSpec((seq, dv), lambda b, h, i: (b, h)),
                  pl.BlockSpec((seq, dv), lambda b, h, i: (b, v_col_block + h)),
                  pl.BlockSpec((None, 2, tq, tq), lambda b, h, i: (h, 0, 0, 0)),
                  pl.BlockSpec((1, dqk), row), pl.BlockSpec((1, dqk), row),
                  pl.BlockSpec((1, dqk), row), pl.BlockSpec((1, dqk), row),
                  pl.BlockSpec((1, dv), row)],
        out_specs=pl.BlockSpec((tq, dv), lambda b, h, i: (b * nq + i, h)),
        out_shape=jax.ShapeDtypeStruct((t, heads * dv), BF16),
        scratch_shapes=[pltpu.VMEM((tq, dv), F32), pltpu.VMEM((tq, dv), F32),
                        pltpu.VMEM((tq, V7X_LANES), F32), pltpu.VMEM((tq, V7X_LANES), F32),
                        pltpu.VMEM((tq, V7X_LANES), F32), pltpu.VMEM((tq, V7X_LANES), F32)],
        compiler_params=_cparams(("arbitrary", "arbitrary", "arbitrary")),
        name="diff_attention",
    )(qn, kn, zqkv, bias, lq1.reshape(1, dqk), lk1.reshape(1, dqk), lq2.reshape(1, dqk), lk2.reshape(1, dqk),
      sub_g.reshape(1, dv))


ROUTER_LANES = 128


def _norm_mod(x, g_ref, sc_ref, sh_ref):
    ms = jnp.mean(x * x, axis=-1, keepdims=True)
    return (x * lax.rsqrt(ms + EPS) * g_ref[...]) * (1.0 + sc_ref[...]) + sh_ref[...]


def _router_kernel(x_ref, g_ref, sc_ref, sh_ref, wr_ref, wlo_ref, br_ref, route_ref, *, ngroups, per_group):
    h = _norm_mod(x_ref[...], g_ref, sc_ref, sh_ref)
    hi = h.astype(BF16)
    lo = (h - hi.astype(F32)).astype(BF16)
    both = jnp.dot(hi, wr_ref[...], preferred_element_type=F32)
    nl = ROUTER_LANES
    logits = both[:, :nl] + both[:, nl:] + jnp.dot(lo, wlo_ref[...], preferred_element_type=F32) + br_ref[...]
    lane = lax.broadcasted_iota(jnp.int32, logits.shape, 1).astype(F32)
    big = float(nl)
    glog = jnp.where(lane < ngroups, logits, NEG_INF)
    gmax = jnp.max(glog, axis=-1, keepdims=True)
    gsum = jnp.sum(jnp.exp(glog - gmax), axis=-1, keepdims=True)
    gp = 1.0 / gsum
    gidx = jnp.min(jnp.where(glog == gmax, lane, big), axis=-1, keepdims=True)
    lo_lane = ngroups + gidx * per_group
    emask = (lane >= lo_lane) & (lane < lo_lane + per_group)
    elog = jnp.where(emask, logits, NEG_INF)
    emax = jnp.max(elog, axis=-1, keepdims=True)
    eexp = jnp.where(emask, jnp.exp(elog - emax), -1.0)
    i0 = jnp.min(jnp.where(eexp == 1.0, lane, big), axis=-1, keepdims=True)
    rest = jnp.where(lane == i0, -1.0, eexp)
    p1 = jnp.max(rest, axis=-1, keepdims=True)
    i1 = jnp.min(jnp.where(rest == p1, lane, big), axis=-1, keepdims=True)
    denom = 1.0 + p1
    w0 = gp * (1.0 / denom)
    w1 = gp * (p1 / denom)
    e0 = i0 - ngroups
    e1 = i1 - ngroups
    route_ref[...] = jnp.where(lane == 0, e0, jnp.where(lane == 1, e1, jnp.where(lane == 2, w0,
                               jnp.where(lane == 3, w1, 0.0))))


def moe_router(x, g, sc, sh, wg, bg, we, be, seq):
    t, d = x.shape
    bsz = sc.shape[0]
    ngroups = wg.shape[-1]
    per_group = we.shape[-1]
    nexp = ngroups * per_group
    nl = ROUTER_LANES
    assert ngroups + nexp <= nl
    wr = jnp.concatenate([wg, jnp.transpose(we, (1, 0, 2)).reshape(d, nexp)], axis=-1)
    wr = jnp.zeros((d, nl), F32).at[:, :ngroups + nexp].set(wr)
    w_hi = wr.astype(BF16)
    w_lo = (wr - w_hi.astype(F32)).astype(BF16)
    br = jnp.zeros((1, nl), F32).at[0, :ngroups + nexp].set(jnp.concatenate([bg, be.reshape(nexp)]))
    tm = _tile(seq, 256)
    per = seq // tm
    return pl.pallas_call(
        functools.partial(_router_kernel, ngroups=ngroups, per_group=per_group),
        grid=(t // tm,),
        in_specs=[pl.BlockSpec((tm, d), lambda i: (i, 0)),
                  pl.BlockSpec((1, d), lambda i: (0, 0)),
                  pl.BlockSpec((None, 1, d), lambda i: (i // per, 0, 0)),
                  pl.BlockSpec((None, 1, d), lambda i: (i // per, 0, 0)),
                  pl.BlockSpec((d, 2 * nl), lambda i: (0, 0)),
                  pl.BlockSpec((d, nl), lambda i: (0, 0)),
                  pl.BlockSpec((1, nl), lambda i: (0, 0))],
        out_specs=pl.BlockSpec((tm, nl), lambda i: (i, 0)),
        out_shape=jax.ShapeDtypeStruct((t, nl), F32),
        compiler_params=_cparams(("arbitrary",)),
        name="moe_router",
    )(x, g.reshape(1, d), sc.reshape(bsz, 1, d), sh.reshape(bsz, 1, d),
      jnp.concatenate([w_hi, w_lo], axis=-1), w_hi, br)


HI16 = 0xFFFF0000


def _dispatch_kernel(pos_ref, zf_ref, x_ref, g_ref, sc_ref, sh_ref, xs_hbm, buf, zbuf, sem, zsem,
                     *, tm, tme, ntok, ntile, topk, half):
    i = pl.program_id(0)
    nsteps = pl.num_programs(0)
    slot = lax.rem(i, 2)

    @pl.when(i == 0)
    def _():
        zbuf[...] = jnp.zeros_like(zbuf)

        def zcopy(tl):
            return pltpu.make_async_copy(zbuf, xs_hbm.at[pl.ds(tl * tme, tme)], zsem)

        def zstart(tl, _):
            @pl.when(zf_ref[tl] > 0)
            def _():
                zcopy(tl).start()
            return 0

        def zwait(tl, _):
            @pl.when(zf_ref[tl] > 0)
            def _():
                zcopy(tl).wait()
            return 0

        lax.fori_loop(0, ntile, zstart, 0)
        lax.fori_loop(0, ntile, zwait, 0)

    def pack_rows(cidx, _):
        rows = pl.ds(pl.multiple_of(cidx * V7X_SUBLANES, V7X_SUBLANES), V7X_SUBLANES)
        h = _norm_mod(x_ref[rows, :], g_ref, sc_ref, sh_ref)
        bits = pltpu.bitcast(h.astype(BF16).astype(F32), jnp.uint32)
        buf[slot, rows, :] = (bits[:, half:] & jnp.uint32(HI16)) | (bits[:, :half] >> 16)
        return 0

    lax.fori_loop(0, tm // V7X_SUBLANES, pack_rows, 0, unroll=4)

    base = i * tm

    def issue(r, _):
        for kk in range(topk):
            pltpu.make_async_copy(buf.at[slot, pl.ds(r, 1)],
                                  xs_hbm.at[pl.ds(pos_ref[kk * ntok + base + r], 1)], sem.at[slot]).start()
        return 0

    lax.fori_loop(0, tm, issue, 0, unroll=4)

    def drain(s):
        for _ in range(topk):
            pltpu.make_async_copy(buf.at[s], xs_hbm.at[pl.ds(0, tm)], sem.at[s]).wait()

    @pl.when(i > 0)
    def _():
        drain(1 - slot)

    @pl.when(i == nsteps - 1)
    def _():
        drain(slot)


W_CHUNKS = 4
W_STAGES = 4


def _expert_kernel(texp_ref, tvalid_ref, xidx_ref, slot_ref, first_ref, nxt_ref, clo_ref, chi_ref,
                   xs_ref, wg_hbm, wu_hbm, wd_hbm, y_ref, wg_buf, wu_buf, wd_buf, st_a, st_d, sem,
                   *, half, layer, d, f):
    i = pl.program_id(0)
    nch = 3 * W_CHUNKS
    ra = d // W_CHUNKS
    rd = f // W_CHUNKS

    def chunk_dma(e, which, part, b):
        if which == 2:
            return pltpu.make_async_copy(wd_hbm.at[layer, e, pl.ds(pl.multiple_of(part * rd, rd), rd), :],
                                         st_d.at[b], sem.at[b])
        src = wg_hbm if which == 0 else wu_hbm
        return pltpu.make_async_copy(src.at[layer, e, pl.ds(pl.multiple_of(part * ra, ra), ra), :],
                                     st_a.at[b], sem.at[b])

    def for_chunk(c, fn):
        for which in range(3):
            @pl.when((c >= which * W_CHUNKS) & (c < (which + 1) * W_CHUNKS))
            def _(which=which):
                fn(which, c - which * W_CHUNKS)

    def start(e, c):
        b = lax.rem(c, W_STAGES)
        for_chunk(c, lambda which, part: chunk_dma(e, which, part, b).start())

    def finish(e, c, s):
        b = lax.rem(c, W_STAGES)

        def fn(which, part):
            chunk_dma(e, which, part, b).wait()
            if which == 2:
                wd_buf[s, pl.ds(pl.multiple_of(part * rd, rd), rd), :] = st_d[b].astype(BF16)
            else:
                dst = wg_buf if which == 0 else wu_buf
                dst[s, pl.ds(pl.multiple_of(part * ra, ra), ra), :] = st_a[b].astype(BF16)

        for_chunk(c, fn)

    slot = slot_ref[i]
    nxt = nxt_ref[i]

    @pl.when(i == 0)
    def _():
        def load_first(c, _):
            start(texp_ref[0], c)
            finish(texp_ref[0], c, 0)
            return 0
        lax.fori_loop(0, nch, load_first, 0)

    @pl.when((nxt >= 0) & (first_ref[i] > 0))
    def _():
        for c in range(W_STAGES):
            start(nxt, jnp.int32(c))

    @pl.when(tvalid_ref[i] > 0)
    def _():
        xp = xs_ref[...]
        x_lo = pltpu.bitcast(xp << 16, F32).astype(BF16)
        x_hi = pltpu.bitcast(xp & jnp.uint32(HI16), F32).astype(BF16)
        hg = (jnp.dot(x_lo, wg_buf[slot, 0:half, :], preferred_element_type=F32)
              + jnp.dot(x_hi, wg_buf[slot, half:, :], preferred_element_type=F32))
        hu = (jnp.dot(x_lo, wu_buf[slot, 0:half, :], preferred_element_type=F32)
              + jnp.dot(x_hi, wu_buf[slot, half:, :], preferred_element_type=F32))
        act = (hg * _sigmoid(hg)) * hu
        yv = jnp.dot(act.astype(BF16), wd_buf[slot], preferred_element_type=F32)
        bits = pltpu.bitcast(yv.astype(BF16).astype(F32), jnp.uint32)
        y_ref[...] = (bits[:, half:] & jnp.uint32(HI16)) | (bits[:, :half] >> 16)

    @pl.when(tvalid_ref[i] == 0)
    def _():
        y_ref[...] = jnp.zeros_like(y_ref)

    @pl.when(nxt >= 0)
    def _():
        def advance(c, _):
            finish(nxt, c, 1 - slot)

            @pl.when(c + W_STAGES < nch)
            def _():
                start(nxt, c + W_STAGES)
            return 0
        lax.fori_loop(clo_ref[i], chi_ref[i], advance, 0)


def _combine_kernel(pos_ref, x_ref, route_ref, g_ref, y_hbm, o_ref, ybuf, sem, *, tm, ntok, topk, half):
    i = pl.program_id(0)
    nsteps = pl.num_programs(0)
    slot = lax.rem(i, 2)

    def gather(step, s):
        base = step * tm

        def issue(r, _):
            for kk in range(topk):
                pltpu.make_async_copy(y_hbm.at[pl.ds(pos_ref[kk * ntok + base + r], 1)],
                                      ybuf.at[s, kk, pl.ds(r, 1)], sem.at[s]).start()
            return 0

        lax.fori_loop(0, tm, issue, 0, unroll=4)

    @pl.when(i == 0)
    def _():
        gather(0, 0)

    @pl.when(i + 1 < nsteps)
    def _():
        gather(i + 1, 1 - slot)

    for kk in range(topk):
        pltpu.make_async_copy(y_hbm.at[pl.ds(0, tm)], ybuf.at[slot, kk], sem.at[slot]).wait()

    def mix_rows(cidx, _):
        rows = pl.ds(pl.multiple_of(cidx * V7X_SUBLANES, V7X_SUBLANES), V7X_SUBLANES)
        w = route_ref[rows, :]
        ylo = yhi = None
        for kk in range(topk):
            yp = ybuf[slot, kk, rows, :]
            wk = w[:, topk + kk:topk + kk + 1]
            lo = wk * pltpu.bitcast(yp << 16, F32)
            hi = wk * pltpu.bitcast(yp & jnp.uint32(HI16), F32)
            ylo = lo if ylo is None else ylo + lo
            yhi = hi if yhi is None else yhi + hi
        o_ref[rows, 0:half] = x_ref[rows, 0:half] + g_ref[:, 0:half] * ylo
        o_ref[rows, half:] = x_ref[rows, half:] + g_ref[:, half:] * yhi
        return 0

    lax.fori_loop(0, tm // V7X_SUBLANES, mix_rows, 0, unroll=4)


def moe_layer(x, route, g, sc, sh, gate, w_gate, w_up, w_down, layer, seq):
    t, d = x.shape
    bsz = sc.shape[0]
    _, nexp, _, f = w_gate.shape
    topk = MOE_TOPK
    half = d // 2
    npair = t * topk
    tme = _tile(npair // nexp, 256) if npair // nexp >= 8 else 8
    ntile = npair // tme + nexp
    tm = _tile(seq, 256)
    per = seq // tm

    flat_e = jnp.transpose(route[:, 0:topk]).astype(jnp.int32).reshape(npair)
    onehot = (flat_e[:, None] == jnp.arange(nexp, dtype=jnp.int32)[None, :]).astype(jnp.int32)
    csum = jnp.cumsum(onehot, axis=0)
    counts = csum[-1]
    tiles_per = (counts + tme - 1) // tme
    tile_end = jnp.cumsum(tiles_per)
    tile_start = tile_end - tiles_per
    pos = jnp.sum(onehot * (tile_start[None, :] * tme + csum - 1), axis=1).astype(jnp.int32)
    n_used = tile_end[-1]
    tile_id = jnp.arange(ntile, dtype=jnp.int32)
    active = tile_id < n_used
    texp = jnp.minimum(jnp.sum((tile_id[:, None] >= tile_end[None, :]).astype(jnp.int32), axis=1), nexp - 1)
    tvalid = jnp.where(active, jnp.clip(counts[texp] - (tile_id - tile_start[texp]) * tme, 0, tme), 0)
    tvalid = tvalid.astype(jnp.int32)
    texp = jnp.where(active, texp, texp[jnp.maximum(n_used - 1, 0)]).astype(jnp.int32)
    xidx = jnp.minimum(tile_id, jnp.maximum(n_used - 1, 0)).astype(jnp.int32)
    zflag = (tvalid < tme).astype(jnp.int32)
    nchunk = 3 * W_CHUNKS
    used = (tiles_per > 0).astype(jnp.int32)
    ordinal = jnp.cumsum(used) - 1
    eids = jnp.arange(nexp, dtype=jnp.int32)
    nxt_of = jnp.min(jnp.where((eids[None, :] > eids[:, None]) & (used[None, :] > 0), eids[None, :], nexp), axis=1)
    nxt_of = jnp.where(nxt_of >= nexp, -1, nxt_of)
    kk = tile_id - tile_start[texp]
    nn = jnp.maximum(tiles_per[texp], 1)
    wslot = jnp.where(active, ordinal[texp] % 2, 0).astype(jnp.int32)
    wfirst = (active & (kk == 0)).astype(jnp.int32)
    wnxt = jnp.where(active, nxt_of[texp], -1).astype(jnp.int32)
    clo = jnp.where(active, kk * nchunk // nn, 0).astype(jnp.int32)
    chi = jnp.where(active, (kk + 1) * nchunk // nn, 0).astype(jnp.int32)

    mod_specs = [pl.BlockSpec((1, d), lambda i, *_: (0, 0)),
                 pl.BlockSpec((None, 1, d), lambda i, *_: (i // per, 0, 0)),
                 pl.BlockSpec((None, 1, d), lambda i, *_: (i // per, 0, 0))]
    xs = pl.pallas_call(
        functools.partial(_dispatch_kernel, tm=tm, tme=tme, ntok=t, ntile=ntile, topk=topk, half=half),
        grid_spec=pltpu.PrefetchScalarGridSpec(
            num_scalar_prefetch=2,
            grid=(t // tm,),
            in_specs=[pl.BlockSpec((tm, d), lambda i, *_: (i, 0))] + mod_specs,
            out_specs=pl.BlockSpec(memory_space=pl.ANY),
            scratch_shapes=[pltpu.VMEM((2, tm, half), jnp.uint32), pltpu.VMEM((tme, half), jnp.uint32),
                            pltpu.SemaphoreType.DMA((2,)), pltpu.SemaphoreType.DMA(())]),
        out_shape=jax.ShapeDtypeStruct((ntile * tme, half), jnp.uint32),
        compiler_params=_cparams(("arbitrary",)),
        name="moe_dispatch",
    )(pos, zflag, x, g.reshape(1, d), sc.reshape(bsz, 1, d), sh.reshape(bsz, 1, d))

    y = pl.pallas_call(
        functools.partial(_expert_kernel, half=half, layer=layer, d=d, f=f),
        grid_spec=pltpu.PrefetchScalarGridSpec(
            num_scalar_prefetch=8,
            grid=(ntile,),
            in_specs=[pl.BlockSpec((tme, half), lambda i, te, tv, xi, *_: (xi[i], 0)),
                      pl.BlockSpec(memory_space=pl.ANY),
                      pl.BlockSpec(memory_space=pl.ANY),
                      pl.BlockSpec(memory_space=pl.ANY)],
            out_specs=pl.BlockSpec((tme, half), lambda i, *_: (i, 0)),
            scratch_shapes=[pltpu.VMEM((2, d, f), BF16), pltpu.VMEM((2, d, f), BF16), pltpu.VMEM((2, f, d), BF16),
                            pltpu.VMEM((W_STAGES, d // W_CHUNKS, f), F32),
                            pltpu.VMEM((W_STAGES, f // W_CHUNKS, d), F32),
                            pltpu.SemaphoreType.DMA((W_STAGES,))]),
        out_shape=jax.ShapeDtypeStruct((ntile * tme, half), jnp.uint32),
        compiler_params=_cparams(("arbitrary",)),
        name="moe_experts",
    )(texp, tvalid, xidx, wslot, wfirst, wnxt, clo, chi, xs, w_gate, w_up, w_down)

    return pl.pallas_call(
        functools.partial(_combine_kernel, tm=tm, ntok=t, topk=topk, half=half),
        grid_spec=pltpu.PrefetchScalarGridSpec(
            num_scalar_prefetch=1,
            grid=(t // tm,),
            in_specs=[pl.BlockSpec((tm, d), lambda i, *_: (i, 0)),
                      pl.BlockSpec((tm, ROUTER_LANES), lambda i, *_: (i, 0)),
                      pl.BlockSpec((None, 1, d), lambda i, *_: (i // per, 0, 0)),
                      pl.BlockSpec(memory_space=pl.ANY)],
            out_specs=pl.BlockSpec((tm, d), lambda i, *_: (i, 0)),
            scratch_shapes=[pltpu.VMEM((2, topk, tm, half), jnp.uint32), pltpu.SemaphoreType.DMA((2,))]),
        out_shape=jax.ShapeDtypeStruct((t, d), F32),
        compiler_params=_cparams(("arbitrary",)),
        name="moe_combine",
    )(pos, x, route, gate.reshape(bsz, 1, d), y)


def kernel(x, c, norm1_g, norm2_g, ada_w, ada_b, ab_w_in, ab_w_out, lru_conv_w, lru_conv_b, lru_wa, lru_ba, lru_wx, lru_bx, lru_lambda, s5_lambda_re, s5_lambda_im, s5_log_dt, s5_b_re, s5_b_im, s5_c_re, s5_c_im, s5_d, s5_glu_w, s5_glu_b, cd_w_in, cd_w_out, sg_ln_g, sg_ln_b, sg_w, sg_b, da_q_norm, da_k_norm, da_lq1, da_lk1, da_lq2, da_lk2, da_sub_g, rel_bias, moe_wg, moe_bg, moe_we, moe_be, moe_w_gate, moe_w_up, moe_w_down):
    bsz, seq, d = x.shape
    depth = norm1_g.shape[0]
    t = bsz * seq
    xt = x.reshape(t, d)
    mod = ada_modulation(c, ada_w, ada_b)

    for layer in range(depth):
        sh1, sc1, g1, sh2, sc2, g2 = [mod[layer, :, i * d:(i + 1) * d] for i in range(6)]
        hmix = norm_modulate(xt, norm1_g[layer], sc1, sh1, seq)
        j = layer // 2
        if layer % 2 == 0:
            lw = lru_conv_w.shape[-1]
            w_in = ab_w_in[j].astype(BF16)
            w_out = ab_w_out[j].astype(BF16)
            z_lru = matmul([hmix], w_in, BF16, col_off=0, ncols=2 * lw)
            z_s5 = matmul([hmix], w_in, BF16, col_off=2 * lw)
            y_a = lru_mixer(z_lru, lru_conv_w[j], lru_conv_b[j], lru_wa[j], lru_ba[j], lru_wx[j], lru_bx[j],
                            lru_lambda[j], bsz, seq)
            y_b = s5_mixer(z_s5, s5_lambda_re[j], s5_lambda_im[j], s5_log_dt[j], s5_b_re[j], s5_b_im[j],
                           s5_c_re[j], s5_c_im[j], s5_d[j], s5_glu_w[j], s5_glu_b[j], bsz, seq)
            xt = matmul([y_a, y_b], w_out, F32, res=xt, gate=g1, seq=seq)
        else:
            sgw = sg_ln_g.shape[-1]
            dqk = da_q_norm.shape[-1]
            dv = da_sub_g.shape[-1]
            heads = rel_bias.shape[1]
            daw = heads * dv
            w_in = cd_w_in[j].astype(BF16)
            w_out = cd_w_out[j].astype(BF16)
            z_sg = matmul([hmix], w_in, BF16, col_off=0, ncols=2 * sgw)
            z_qkv = matmul([hmix], w_in, BF16, col_off=2 * sgw)
            y_c = sgu_mixer(z_sg, sg_ln_g[j], sg_ln_b[j], sg_w[j], sg_b[j])
            q_gain = jnp.tile(da_q_norm[j] * (dqk ** -0.5 * LOG2E), daw // dqk).reshape(1, daw)
            k_gain = jnp.tile(da_k_norm[j], daw // dqk).reshape(1, daw)
            qn = qk_norm(z_qkv, 0, daw, q_gain, dqk)
            kn = qk_norm(z_qkv, 1, daw, k_gain, dqk)
            y_d = diff_attention(qn, kn, z_qkv, 2 * daw // dv, rel_bias, da_lq1[j], da_lk1[j], da_lq2[j], da_lk2[j],
                                 da_sub_g[j], bsz, seq, layer)
            xt = matmul([y_c, y_d], w_out, F32, res=xt, gate=g1, seq=seq)
        route = moe_router(xt, norm2_g[layer], sc2, sh2, moe_wg[layer], moe_bg[layer], moe_we[layer], moe_be[layer],
                           seq)
        xt = moe_layer(xt, route, norm2_g[layer], sc2, sh2, g2, moe_w_gate, moe_w_up, moe_w_down, layer, seq)
    return xt.reshape(bsz, seq, d)
```

```python
import functools
import math

import jax
import jax.numpy as jnp
import numpy as np
from jax import lax
from jax.experimental import pallas as pl
from jax.experimental.pallas import tpu as pltpu

F32 = jnp.float32
BF16 = jnp.bfloat16

EPS = 1e-6
LRU_C = 8.0
REL_MAX_DIST = 128
MOE_TOPK = 2
NEG_INF = -1e30
LOG2E = math.log2(math.e)

V7X_LANES = 128
V7X_SUBLANES = 8
V7X_VMEM_LIMIT_BYTES = 56 * 1024 * 1024


def _cparams(semantics):
    return pltpu.CompilerParams(dimension_semantics=semantics, vmem_limit_bytes=V7X_VMEM_LIMIT_BYTES)


def _sigmoid(x):
    return 0.5 * jnp.tanh(0.5 * x) + 0.5


def _gelu(x):
    return 0.5 * x * (1.0 + jnp.tanh(math.sqrt(2.0 / math.pi) * (x + 0.044715 * (x * x * x))))


def _tile(n, want):
    t = min(n, want)
    while n % t:
        t -= 1
    return t


def _ada_kernel(c_ref, w_ref, b_ref, o_ref):
    c = c_ref[...]
    cond = c * _sigmoid(c)
    o_ref[...] = jnp.dot(cond.astype(BF16), w_ref[...].astype(BF16), preferred_element_type=F32) + b_ref[...]


def ada_modulation(c, ada_w, ada_b):
    bsz, d = c.shape
    depth, _, n = ada_w.shape
    rows = 16
    cp = jnp.zeros((rows, d), F32).at[:bsz].set(c)
    tn = _tile(n, 512)
    out = pl.pallas_call(
        _ada_kernel,
        grid=(depth, n // tn),
        in_specs=[pl.BlockSpec((rows, d), lambda l, j: (0, 0)),
                  pl.BlockSpec((None, d, tn), lambda l, j: (l, 0, j)),
                  pl.BlockSpec((None, 1, tn), lambda l, j: (l, 0, j))],
        out_specs=pl.BlockSpec((None, rows, tn), lambda l, j: (l, 0, j)),
        out_shape=jax.ShapeDtypeStruct((depth, rows, n), F32),
        compiler_params=_cparams(("arbitrary", "arbitrary")),
        name="ada_modulation",
    )(cp, ada_w, ada_b.reshape(depth, 1, n))
    return out[:, :bsz]


def _norm_mod_kernel(x_ref, g_ref, sc_ref, sh_ref, o_ref):
    x = x_ref[...]
    ms = jnp.mean(x * x, axis=-1, keepdims=True)
    y = x * lax.rsqrt(ms + EPS) * g_ref[...]
    o_ref[...] = (y * (1.0 + sc_ref[...]) + sh_ref[...]).astype(o_ref.dtype)


def norm_modulate(x, g, sc, sh, seq, out_dtype=BF16):
    t, d = x.shape
    bsz = sc.shape[0]
    tm = _tile(seq, 256)
    per = seq // tm
    return pl.pallas_call(
        _norm_mod_kernel,
        grid=(t // tm,),
        in_specs=[pl.BlockSpec((tm, d), lambda i: (i, 0)),
                  pl.BlockSpec((1, d), lambda i: (0, 0)),
                  pl.BlockSpec((None, 1, d), lambda i: (i // per, 0, 0)),
                  pl.BlockSpec((None, 1, d), lambda i: (i // per, 0, 0))],
        out_specs=pl.BlockSpec((tm, d), lambda i: (i, 0)),
        out_shape=jax.ShapeDtypeStruct((t, d), out_dtype),
        compiler_params=_cparams(("arbitrary",)),
        name="norm_modulate",
    )(x, g.reshape(1, d), sc.reshape(bsz, 1, d), sh.reshape(bsz, 1, d))


def _matmul_kernel(*refs, ksplit, has_res):
    n_lhs = len(ksplit)
    a_refs = refs[:n_lhs]
    w_ref = refs[n_lhs]
    o_ref = refs[-1]
    acc = None
    k0 = 0
    for a_ref, kk in zip(a_refs, ksplit):
        part = jnp.dot(a_ref[...], w_ref[k0:k0 + kk, :], preferred_element_type=F32)
        acc = part if acc is None else acc + part
        k0 += kk
    if has_res:
        res_ref, gate_ref = refs[n_lhs + 1], refs[n_lhs + 2]
        acc = res_ref[...] + gate_ref[...] * acc
    o_ref[...] = acc.astype(o_ref.dtype)


def matmul(lhs, w, out_dtype, col_off=0, ncols=None, res=None, gate=None, seq=None):
    m = lhs[0].shape[0]
    ktot = w.shape[0]
    ksplit = tuple(a.shape[1] for a in lhs)
    assert sum(ksplit) == ktot
    n = w.shape[1] - col_off if ncols is None else ncols
    tm = _tile(m if seq is None else seq, 1024)
    wide = res is None and jnp.dtype(out_dtype).itemsize == 2
    tn = _tile(math.gcd(n, col_off) if col_off else n, 1024 if wide else 512)
    joff = col_off // tn
    in_specs = [pl.BlockSpec((tm, kk), lambda i, j: (i, 0)) for kk in ksplit]
    in_specs += [pl.BlockSpec((ktot, tn), lambda i, j: (0, j + joff))]
    args = list(lhs) + [w]
    if res is not None:
        per = seq // tm
        bsz = gate.shape[0]
        in_specs += [pl.BlockSpec((tm, tn), lambda i, j: (i, j)),
                     pl.BlockSpec((None, 1, tn), lambda i, j: (i // per, 0, j))]
        args += [res, gate.reshape(bsz, 1, n)]
    return pl.pallas_call(
        functools.partial(_matmul_kernel, ksplit=ksplit, has_res=res is not None),
        grid=(m // tm, n // tn),
        in_specs=in_specs,
        out_specs=pl.BlockSpec((tm, tn), lambda i, j: (i, j)),
        out_shape=jax.ShapeDtypeStruct((m, n), out_dtype),
        compiler_params=_cparams(("arbitrary", "arbitrary")),
        name="matmul",
    )(*args)


def _interleave_rows(src_ref, dst_ref, nslab, nsub, sub_len):
    for sl in range(nslab):
        lanes = slice(sl * V7X_LANES, (sl + 1) * V7X_LANES)
        for r in range(nsub):
            dst_ref[sl, pl.ds(r, sub_len, stride=nsub), :] = src_ref[r * sub_len:(r + 1) * sub_len, lanes].astype(F32)


def _deinterleave_rows(src_ref, dst_ref, nslab, nsub, sub_len):
    for sl in range(nslab):
        lanes = slice(sl * V7X_LANES, (sl + 1) * V7X_LANES)
        for r in range(nsub):
            dst_ref[r * sub_len:(r + 1) * sub_len, lanes] = src_ref[sl, pl.ds(r, sub_len, stride=nsub), :].astype(
                dst_ref.dtype)


def _sublane_scan(a, b, row):
    for d in (1, 2, 4):
        keep = row >= d
        sa = jnp.where(keep, pltpu.roll(a, d, 0), 1.0)
        sb = jnp.where(keep, pltpu.roll(b, d, 0), 0.0)
        b = b + a * sb
        a = a * sa
    return a, b


def _bcast_last(x):
    return jnp.broadcast_to(x[V7X_SUBLANES - 1:V7X_SUBLANES, :], x.shape)


def _lru_kernel(x_ref, gate_ref, cw_ref, cb_ref, wax_ref, bax_ref, lam_ref, o_ref,
                xp_ref, gp_ref, yp_ref, tail_ref, a_ref, b_ref, carry_ref, *, heads, hd, tc, kconv):
    s = pl.program_id(1)
    sub = V7X_SUBLANES
    sub_len = tc // sub
    nslab = heads * hd // V7X_LANES
    spl = hd // V7X_LANES
    halo = (kconv - 1) * sub
    row = lax.broadcasted_iota(jnp.int32, (sub, V7X_LANES), 0)
    rowh = lax.broadcasted_iota(jnp.int32, (sub, hd), 0)

    @pl.when(s == 0)
    def _():
        tail_ref[...] = jnp.zeros_like(tail_ref)
        carry_ref[...] = jnp.zeros_like(carry_ref)

    for sl in range(nslab):
        lanes = slice(sl * V7X_LANES, (sl + 1) * V7X_LANES)
        for r in range(sub):
            xp_ref[sl, pl.ds(halo + r, sub_len, stride=sub), :] = x_ref[r * sub_len:(r + 1) * sub_len, lanes].astype(F32)
        for e in range(kconv - 1):
            cur = xp_ref[sl, halo + (sub_len - (kconv - 1) + e) * sub:halo + (sub_len - (kconv - 1) + e + 1) * sub, :]
            prev = tail_ref[sl, e * sub:(e + 1) * sub, :]
            xp_ref[sl, e * sub:(e + 1) * sub, :] = jnp.where(row == 0, pltpu.roll(prev, 1, 0), pltpu.roll(cur, 1, 0))
            tail_ref[sl, e * sub:(e + 1) * sub, :] = cur
    _interleave_rows(gate_ref, gp_ref, nslab, sub, sub_len)

    for h in range(heads):
        cols = slice(h * hd, (h + 1) * hd)
        parts = []
        for q in range(spl):
            sl = h * spl + q
            lanes = slice(sl * V7X_LANES, (sl + 1) * V7X_LANES)
            acc = cb_ref[:, lanes] + cw_ref[0:1, lanes] * xp_ref[sl, 0:tc, :]
            for k in range(1, kconv):
                acc = acc + cw_ref[k:k + 1, lanes] * xp_ref[sl, k * sub:k * sub + tc, :]
            parts.append(acc)
        xc = jnp.concatenate(parts, axis=-1) if spl > 1 else parts[0]
        pre = jnp.dot(xc.astype(BF16), wax_ref[h], preferred_element_type=F32) + bax_ref[h]
        r = _sigmoid(pre[:, :hd])
        gi = _sigmoid(pre[:, hd:])
        nl = -lam_ref[:, cols]
        sp = jnp.maximum(nl, 0.0) + jnp.log1p(jnp.exp(-jnp.abs(nl)))
        log_a = (-LRU_C) * r * sp
        a = jnp.exp(log_a)
        a_ref[...] = a
        b_ref[...] = jnp.sqrt(1.0 - a * a) * (gi * xc)

        def pass1(t, carry):
            hprev, pprev = carry
            rows = pl.ds(pl.multiple_of(t * sub, sub), sub)
            a = a_ref[rows, :]
            hh = a * hprev + b_ref[rows, :]
            pp = a * pprev
            b_ref[rows, :] = hh
            a_ref[rows, :] = pp
            return hh, pp

        zero = jnp.zeros((sub, hd), F32)
        hfin, pfin = lax.fori_loop(0, sub_len, pass1, (zero, zero + 1.0), unroll=4)
        cin = carry_ref[:, cols]
        ptot, hloc = _sublane_scan(pfin, hfin, rowh)
        hend = hloc + ptot * cin
        carry_ref[:, cols] = _bcast_last(hend)
        entry = jnp.where(rowh == 0, cin, pltpu.roll(hend, 1, 0))

        def pass2(t, _):
            rows = pl.ds(pl.multiple_of(t * sub, sub), sub)
            hh = b_ref[rows, :] + a_ref[rows, :] * entry
            for q in range(spl):
                sl = h * spl + q
                yp_ref[sl, rows, :] = _gelu(gp_ref[sl, rows, :]) * hh[:, q * V7X_LANES:(q + 1) * V7X_LANES]
            return 0

        lax.fori_loop(0, sub_len, pass2, 0, unroll=4)

    _deinterleave_rows(yp_ref, o_ref, nslab, sub, sub_len)


def lru_mixer(z, conv_w, conv_b, wa, ba, wx, bx, lam, bsz, seq):
    t = z.shape[0]
    heads, hd, _ = wa.shape
    w = heads * hd
    kconv = conv_w.shape[0]
    tc = _tile(seq, 256)
    ns = seq // tc
    nslab = w // V7X_LANES
    halo = (kconv - 1) * V7X_SUBLANES
    wax = jnp.concatenate([wa, wx], axis=-1).astype(BF16)
    bax = jnp.concatenate([ba.reshape(heads, 1, hd), bx.reshape(heads, 1, hd)], axis=-1)
    return pl.pallas_call(
        functools.partial(_lru_kernel, heads=heads, hd=hd, tc=tc, kconv=kconv),
        grid=(bsz, ns),
        in_specs=[pl.BlockSpec((tc, w), lambda b, s: (b * ns + s, 0)),
                  pl.BlockSpec((tc, w), lambda b, s: (b * ns + s, 1)),
                  pl.BlockSpec((kconv, w), lambda b, s: (0, 0)),
                  pl.BlockSpec((1, w), lambda b, s: (0, 0)),
                  pl.BlockSpec((heads, hd, 2 * hd), lambda b, s: (0, 0, 0)),
                  pl.BlockSpec((heads, 1, 2 * hd), lambda b, s: (0, 0, 0)),
                  pl.BlockSpec((1, w), lambda b, s: (0, 0))],
        out_specs=pl.BlockSpec((tc, w), lambda b, s: (b * ns + s, 0)),
        out_shape=jax.ShapeDtypeStruct((t, w), BF16),
        scratch_shapes=[pltpu.VMEM((nslab, halo + tc, V7X_LANES), F32),
                        pltpu.VMEM((nslab, tc, V7X_LANES), F32),
                        pltpu.VMEM((nslab, tc, V7X_LANES), F32),
                        pltpu.VMEM((nslab, halo, V7X_LANES), F32),
                        pltpu.VMEM((tc, hd), F32),
                        pltpu.VMEM((tc, hd), F32),
                        pltpu.VMEM((V7X_SUBLANES, w), F32)],
        compiler_params=_cparams(("arbitrary", "arbitrary")),
        name="lru_mixer",
    )(z, z, conv_w, conv_b.reshape(1, w), wax, bax, lam.reshape(1, w))


S5_GROUPS_PER_BLOCK = 8


def _cmul(ar, ai, br, bi):
    return ar * br - ai * bi, ar * bi + ai * br


def _s5_kernel(u_ref, lre_ref, lim_ref, ldt_ref, bre_ref, bim_ref, cre_ref, cim_ref, d_ref, gw_ref, gb_ref,
               o_ref, wb_ref, a_ref, am_ref, p_ref, carry_ref, up_ref, re_ref, im_ref, y_ref, yp_ref,
               *, tc, nblk, cin, nst, lane_chunk):
    s = pl.program_id(1)
    gn = nblk * nst
    sub = V7X_SUBLANES
    sub_len = tc // sub
    nslab = nblk * cin // V7X_LANES

    @pl.when(s == 0)
    def _():
        lr = lre_ref[...]
        li = lim_ref[...]
        dt = jnp.exp(ldt_ref[...])
        mag = jnp.exp(lr * dt)
        ar = mag * jnp.cos(li * dt)
        ai = mag * jnp.sin(li * dt)
        den = lr * lr + li * li
        zr = ar - 1.0
        cr = (zr * lr + ai * li) / den
        ci = (ai * lr - zr * li) / den
        for j in range(nblk):
            cols = slice(j * nst, (j + 1) * nst)
            br = bre_ref[j]
            bi = bim_ref[j]
            wb_ref[j, :, 0:nst] = (cr[:, cols] * br - ci[:, cols] * bi).astype(BF16)
            wb_ref[j, :, nst:2 * nst] = (cr[:, cols] * bi + ci[:, cols] * br).astype(BF16)
        a_ref[0] = jnp.broadcast_to(ar, (sub, gn))
        a_ref[1] = jnp.broadcast_to(ai, (sub, gn))
        qr, qi = ar, ai
        for _ in range(sub_len - 1):
            qr, qi = _cmul(qr, qi, ar, ai)
        row = lax.broadcasted_iota(jnp.int32, (sub, gn), 0)
        pr, pi = qr, qi
        accr = jnp.zeros((sub, gn), F32)
        acci = jnp.zeros((sub, gn), F32)
        powers = {}
        for r in range(sub):
            powers[r + 1] = (pr, pi)
            accr = jnp.where(row == r, pr, accr)
            acci = jnp.where(row == r, pi, acci)
            pr, pi = _cmul(pr, pi, qr, qi)
        p_ref[0] = accr
        p_ref[1] = acci
        for idx, dd in enumerate((1, 2, 4)):
            wr, wi = powers[dd]
            am_ref[2 * idx] = jnp.where(row >= dd, wr, 0.0)
            am_ref[2 * idx + 1] = jnp.where(row >= dd, wi, 0.0)
        carry_ref[...] = jnp.zeros_like(carry_ref)

    _interleave_rows(u_ref, up_ref, nslab, sub, sub_len)
    for j in range(nblk):
        bu = jnp.dot(up_ref[j].astype(BF16), wb_ref[j], preferred_element_type=F32)
        re_ref[:, j * nst:(j + 1) * nst] = bu[:, :nst]
        im_ref[:, j * nst:(j + 1) * nst] = bu[:, nst:]

    row_c = lax.broadcasted_iota(jnp.int32, (sub, lane_chunk), 0)
    for c in range(gn // lane_chunk):
        lsl = slice(c * lane_chunk, (c + 1) * lane_chunk)

        def pass1(t, carry, lsl=lsl):
            pr, pi = carry
            rows = pl.ds(pl.multiple_of(t * sub, sub), sub)
            ar = a_ref[0, :, lsl]
            ai = a_ref[1, :, lsl]
            hr = re_ref[rows, lsl] + (ar * pr - ai * pi)
            hi = im_ref[rows, lsl] + (ar * pi + ai * pr)
            re_ref[rows, lsl] = hr
            im_ref[rows, lsl] = hi
            return hr, hi

        zero = jnp.zeros((sub, lane_chunk), F32)
        fr, fi = lax.fori_loop(0, sub_len, pass1, (zero, zero), unroll=2)
        for idx, dd in enumerate((1, 2, 4)):
            mr = am_ref[2 * idx, :, lsl]
            mi = am_ref[2 * idx + 1, :, lsl]
            sr = pltpu.roll(fr, dd, 0)
            si = pltpu.roll(fi, dd, 0)
            fr, fi = fr + (mr * sr - mi * si), fi + (mr * si + mi * sr)
        cr_ = carry_ref[0, :, lsl]
        ci_ = carry_ref[1, :, lsl]
        pr_ = p_ref[0, :, lsl]
        pi_ = p_ref[1, :, lsl]
        er = fr + (pr_ * cr_ - pi_ * ci_)
        ei = fi + (pr_ * ci_ + pi_ * cr_)
        carry_ref[0, :, lsl] = _bcast_last(er)
        carry_ref[1, :, lsl] = _bcast_last(ei)
        sr0 = jnp.where(row_c == 0, cr_, pltpu.roll(er, 1, 0))
        si0 = jnp.where(row_c == 0, ci_, pltpu.roll(ei, 1, 0))

        def pass2(t, carry, lsl=lsl):
            qr, qi = carry
            rows = pl.ds(pl.multiple_of(t * sub, sub), sub)
            ar = a_ref[0, :, lsl]
            ai = a_ref[1, :, lsl]
            qr, qi = ar * qr - ai * qi, ar * qi + ai * qr
            re_ref[rows, lsl] = re_ref[rows, lsl] + qr
            im_ref[rows, lsl] = im_ref[rows, lsl] + qi
            return qr, qi

        lax.fori_loop(0, sub_len, pass2, (sr0, si0), unroll=2)

    for j in range(nblk):
        cols = slice(j * nst, (j + 1) * nst)
        yj = (jnp.dot(re_ref[:, cols].astype(BF16), cre_ref[j], preferred_element_type=F32)
              - jnp.dot(im_ref[:, cols].astype(BF16), cim_ref[j], preferred_element_type=F32))
        ucols = slice(j * cin, (j + 1) * cin)
        y_ref[:, ucols] = yj + d_ref[:, ucols] * up_ref[j]
    g = _gelu(y_ref[...])
    gate = _sigmoid(jnp.dot(g.astype(BF16), gw_ref[...], preferred_element_type=F32) + gb_ref[...])
    fin = g * gate
    for sl in range(nslab):
        yp_ref[sl] = fin[:, sl * V7X_LANES:(sl + 1) * V7X_LANES]
    _deinterleave_rows(yp_ref, o_ref, nslab, sub, sub_len)


def s5_mixer(u, lam_re, lam_im, log_dt, b_re, b_im, c_re, c_im, d, glu_w, glu_b, bsz, seq):
    t, ws = u.shape
    groups, nstate, gch = b_re.shape
    gpb = S5_GROUPS_PER_BLOCK
    nblk = groups // gpb
    cin = gpb * gch
    assert cin == V7X_LANES
    nst = gpb * nstate
    gn = groups * nstate
    tc = _tile(seq, 256)
    ns = seq // tc
    lane_chunk = _tile(gn, 1024)
    nslab = ws // V7X_LANES
    eye = jnp.eye(gpb, dtype=F32)

    def bdiag_in(b):
        bb = b.reshape(nblk, gpb, nstate, gch).transpose(0, 1, 3, 2)
        return (bb[:, :, :, None, :] * eye[None, :, None, :, None]).reshape(nblk, cin, nst)

    def bdiag_out(c):
        cc = c.reshape(nblk, gpb, gch, nstate).transpose(0, 1, 3, 2)
        return (cc[:, :, :, None, :] * eye[None, :, None, :, None]).reshape(nblk, nst, cin)

    const2 = lambda b, s: (0, 0)
    const3 = lambda b, s: (0, 0, 0)
    return pl.pallas_call(
        functools.partial(_s5_kernel, tc=tc, nblk=nblk, cin=cin, nst=nst, lane_chunk=lane_chunk),
        grid=(bsz, ns),
        in_specs=[pl.BlockSpec((tc, ws), lambda b, s: (b * ns + s, 0)),
                  pl.BlockSpec((1, gn), const2), pl.BlockSpec((1, gn), const2), pl.BlockSpec((1, gn), const2),
                  pl.BlockSpec((nblk, cin, nst), const3), pl.BlockSpec((nblk, cin, nst), const3),
                  pl.BlockSpec((nblk, nst, cin), const3), pl.BlockSpec((nblk, nst, cin), const3),
                  pl.BlockSpec((1, ws), const2),
                  pl.BlockSpec((ws, ws), const2),
                  pl.BlockSpec((1, ws), const2)],
        out_specs=pl.BlockSpec((tc, ws), lambda b, s: (b * ns + s, 0)),
        out_shape=jax.ShapeDtypeStruct((t, ws), BF16),
        scratch_shapes=[pltpu.VMEM((nblk, cin, 2 * nst), BF16),
                        pltpu.VMEM((2, V7X_SUBLANES, gn), F32),
                        pltpu.VMEM((6, V7X_SUBLANES, gn), F32),
                        pltpu.VMEM((2, V7X_SUBLANES, gn), F32),
                        pltpu.VMEM((2, V7X_SUBLANES, gn), F32),
                        pltpu.VMEM((nslab, tc, V7X_LANES), F32),
                        pltpu.VMEM((tc, gn), F32),
                        pltpu.VMEM((tc, gn), F32),
                        pltpu.VMEM((tc, ws), F32),
                        pltpu.VMEM((nslab, tc, V7X_LANES), F32)],
        compiler_params=_cparams(("arbitrary", "arbitrary")),
        name="s5_mixer",
    )(u, lam_re.reshape(1, gn), lam_im.reshape(1, gn),
      jnp.broadcast_to(log_dt[:, None], (groups, nstate)).reshape(1, gn),
      bdiag_in(b_re), bdiag_in(b_im), bdiag_out(c_re).astype(BF16), bdiag_out(c_im).astype(BF16),
      d.reshape(1, ws), glu_w.astype(BF16), glu_b.reshape(1, ws))


def _sgu_kernel(u_ref, v_ref, g_ref, b_ref, w_ref, bs_ref, o_ref, *, heads, hd, chunk, nchunk):
    v = _gelu(v_ref[...].astype(F32))
    mu = jnp.mean(v, axis=-1, keepdims=True)
    vc = v - mu
    var = jnp.mean(vc * vc, axis=-1, keepdims=True)
    vn = (vc * lax.rsqrt(var + EPS) * g_ref[...] + b_ref[...]).astype(BF16)
    r = lax.broadcasted_iota(jnp.int32, (chunk, chunk), 0)
    c = lax.broadcasted_iota(jnp.int32, (chunk, chunk), 1)
    tril = r >= c
    for h in range(heads):
        wh = jnp.where(tril, w_ref[h], 0.0).astype(BF16)
        cols = slice(h * hd, (h + 1) * hd)
        for n in range(nchunk):
            rows = slice(n * chunk, (n + 1) * chunk)
            gsp = jnp.dot(wh, vn[rows, cols], preferred_element_type=F32) + bs_ref[h]
            o_ref[rows, cols] = (_gelu(u_ref[rows, cols].astype(F32)) * gsp).astype(o_ref.dtype)


def sgu_mixer(z, ln_g, ln_b, w_s, b_s):
    t = z.shape[0]
    heads, chunk, _ = w_s.shape
    w = ln_g.shape[0]
    hd = w // heads
    nchunk = 2 if (t // chunk) % 2 == 0 else 1
    tm = nchunk * chunk
    bs = jnp.broadcast_to(b_s[:, :, None], (heads, chunk, hd))
    return pl.pallas_call(
        functools.partial(_sgu_kernel, heads=heads, hd=hd, chunk=chunk, nchunk=nchunk),
        grid=(t // tm,),
        in_specs=[pl.BlockSpec((tm, w), lambda i: (i, 0)),
                  pl.BlockSpec((tm, w), lambda i: (i, 1)),
                  pl.BlockSpec((1, w), lambda i: (0, 0)),
                  pl.BlockSpec((1, w), lambda i: (0, 0)),
                  pl.BlockSpec((heads, chunk, chunk), lambda i: (0, 0, 0)),
                  pl.BlockSpec((heads, chunk, hd), lambda i: (0, 0, 0))],
        out_specs=pl.BlockSpec((tm, w), lambda i: (i, 0)),
        out_shape=jax.ShapeDtypeStruct((t, w), BF16),
        compiler_params=_cparams(("arbitrary",)),
        name="sgu_mixer",
    )(z, z, ln_g.reshape(1, w), ln_b.reshape(1, w), w_s, bs)


def _qk_norm_kernel(x_ref, g_ref, o_ref, *, nseg, seg):
    x = x_ref[...].astype(F32)
    for i in range(nseg):
        cols = slice(i * seg, (i + 1) * seg)
        xs = x[:, cols]
        ms = jnp.mean(xs * xs, axis=-1, keepdims=True)
        o_ref[:, cols] = (xs * lax.rsqrt(ms + EPS) * g_ref[:, cols]).astype(o_ref.dtype)


def qk_norm(z, col_block, width, gain_row, seg):
    t = z.shape[0]
    tm = _tile(t, 512)
    return pl.pallas_call(
        functools.partial(_qk_norm_kernel, nseg=width // seg, seg=seg),
        grid=(t // tm,),
        in_specs=[pl.BlockSpec((tm, width), lambda i: (i, col_block)),
                  pl.BlockSpec((1, width), lambda i: (0, 0))],
        out_specs=pl.BlockSpec((tm, width), lambda i: (i, 0)),
        out_shape=jax.ShapeDtypeStruct((t, width), BF16),
        compiler_params=_cparams(("arbitrary",)),
        name="qk_norm",
    )(z, gain_row)


ATTN_ROW_STRIP = 128


def _attn_kernel(q_ref, k_ref, v_ref, bias_ref, lq1_ref, lk1_ref, lq2_ref, lk2_ref, sg_ref, o_ref,
                 acc0_ref, acc1_ref, m0_ref, m1_ref, l0_ref, l1_ref, *, tq, dqk, lam_init):
    qi = pl.program_id(2)
    accs, ms, ls = (acc0_ref, acc1_ref), (m0_ref, m1_ref), (l0_ref, l1_ref)
    for mp in range(2):
        ms[mp][...] = jnp.full_like(ms[mp], NEG_INF)
        ls[mp][...] = jnp.zeros_like(ls[mp])
        accs[mp][...] = jnp.zeros_like(accs[mp])
    r = lax.broadcasted_iota(jnp.int32, (tq, tq), 0)
    c = lax.broadcasted_iota(jnp.int32, (tq, tq), 1)
    causal = r >= c
    krep = tq // V7X_LANES
    vrep = accs[0].shape[1] // V7X_LANES

    def block(kb, bias_idx, masked):
        k0 = pl.multiple_of(kb * tq, tq)
        v = v_ref[pl.ds(k0, tq), :]
        for r0 in range(0, tq, ATTN_ROW_STRIP):
            rs = slice(r0, r0 + ATTN_ROW_STRIP)
            for mp in range(2):
                q = q_ref[rs, mp * dqk:(mp + 1) * dqk]
                k = k_ref[pl.ds(k0, tq), mp * dqk:(mp + 1) * dqk]
                sc = lax.dot_general(q, k, (((1,), (1,)), ((), ())), preferred_element_type=F32)
                if bias_idx is not None:
                    sc = sc + bias_ref[bias_idx, rs, :]
                if masked:
                    sc = jnp.where(causal[rs, :], sc, NEG_INF)
                m_old = ms[mp][rs, :]
                m_new = jnp.maximum(m_old, jnp.max(sc, axis=-1, keepdims=True))
                alpha = jnp.exp2(m_old - m_new)
                p = jnp.exp2(sc - jnp.tile(m_new, (1, krep)))
                ls[mp][rs, :] = alpha * ls[mp][rs, :] + jnp.sum(p, axis=-1, keepdims=True)
                accs[mp][rs, :] = (jnp.tile(alpha, (1, vrep)) * accs[mp][rs, :]
                                   + jnp.dot(p.astype(BF16), v, preferred_element_type=F32))
                ms[mp][rs, :] = m_new

    nfar = jnp.maximum(qi - 1, 0)

    def far_pair(j, _):
        block(2 * j, None, False)
        block(2 * j + 1, None, False)
        return 0

    lax.fori_loop(0, nfar // 2, far_pair, 0)

    @pl.when(nfar % 2 == 1)
    def _():
        block(nfar - 1, None, False)

    @pl.when(qi > 0)
    def _():
        block(qi - 1, 1, False)
        block(qi, 0, True)

    @pl.when(qi == 0)
    def _():
        block(qi, 0, True)

    lam = (jnp.exp(jnp.sum(lq1_ref[...] * lk1_ref[...], axis=-1, keepdims=True))
           - jnp.exp(jnp.sum(lq2_ref[...] * lk2_ref[...], axis=-1, keepdims=True)) + lam_init)
    o = (accs[0][...] * jnp.tile(1.0 / ls[0][...], (1, vrep))
         - lam * (accs[1][...] * jnp.tile(1.0 / ls[1][...], (1, vrep))))
    ms_o = jnp.mean(o * o, axis=-1, keepdims=True)
    o_ref[...] = (o * lax.rsqrt(ms_o + EPS) * sg_ref[...] * (1.0 - lam_init)).astype(o_ref.dtype)


def _t5_bucket(n, buckets):
    max_exact = buckets // 2
    nf = jnp.maximum(n, 1).astype(F32)
    large = max_exact + (jnp.log(nf / max_exact) / math.log(REL_MAX_DIST / max_exact)
                         * (buckets - max_exact)).astype(jnp.int32)
    large = jnp.minimum(large, buckets - 1)
    return jnp.where(n < max_exact, n, large)


def _toeplitz(w, rows, cols):
    nh, lw = w.shape
    flat = jnp.tile(w, (1, rows))[:, :rows * (lw - 1)]
    return flat.reshape(nh, rows, lw - 1)[:, :, :cols]


def diff_attention(qn, kn, zqkv, v_col_block, rel_bias, lq1, lk1, lq2, lk2, sub_g, bsz, seq, layer):
    t = qn.shape[0]
    buckets, heads = rel_bias.shape
    dv = sub_g.shape[0]
    dqk = dv // 2
    tq = _tile(seq, 512)
    assert tq >= REL_MAX_DIST, "far key blocks must all fall in the last relative-position bucket"
    nq = seq // tq
    lam_init = 0.8 - 0.6 * math.exp(-0.3 * layer)
    table = jnp.transpose(rel_bias.astype(F32))
    fvals = table[:, _t5_bucket(jnp.arange(2 * tq + 1, dtype=jnp.int32), buckets)]
    fvals = (fvals - fvals[:, 2 * tq:]) * LOG2E
    jj = np.arange(2 * tq)
    n_diag = np.where(jj <= tq, 0, 2 * tq - jj)
    n_near = np.where(jj < tq, tq - jj, 3 * tq - jj)
    bias = jnp.stack([_toeplitz(fvals[:, n_diag], tq, tq), _toeplitz(fvals[:, n_near], tq, tq)], axis=1)
    row = lambda b, h, i: (0, 0)
    return pl.pallas_call(
        functools.partial(_attn_kernel, tq=tq, dqk=dqk, lam_init=lam_init),
        grid=(bsz, heads, nq),
        in_specs=[pl.BlockSpec((tq, dv), lambda b, h, i: (b * nq + i, h)),
                  pl.BlockSpec((seq, dv), lambda b, h, i: (b, h)),
                  pl.BlockSpec((seq, dv), lambda b, h, i: (b, v_col_block + h)),
                  pl.BlockSpec((None, 2, tq, tq), lambda b, h, i: (h, 0, 0, 0)),
                  pl.BlockSpec((1, dqk), row), pl.BlockSpec((1, dqk), row),
                  pl.BlockSpec((1, dqk), row), pl.BlockSpec((1, dqk), row),
                  pl.BlockSpec((1, dv), row)],
        out_specs=pl.BlockSpec((tq, dv), lambda b, h, i: (b * nq + i, h)),
        out_shape=jax.ShapeDtypeStruct((t, heads * dv), BF16),
        scratch_shapes=[pltpu.VMEM((tq, dv), F32), pltpu.VMEM((tq, dv), F32),
                        pltpu.VMEM((tq, V7X_LANES), F32), pltpu.VMEM((tq, V7X_LANES), F32),
                        pltpu.VMEM((tq, V7X_LANES), F32), pltpu.VMEM((tq, V7X_LANES), F32)],
        compiler_params=_cparams(("arbitrary", "arbitrary", "arbitrary")),
        name="diff_attention",
    )(qn, kn, zqkv, bias, lq1.reshape(1, dqk), lk1.reshape(1, dqk), lq2.reshape(1, dqk), lk2.reshape(1, dqk),
      sub_g.reshape(1, dv))


ROUTER_LANES = 128
HI16 = 0xFFFF0000


def _norm_mod(x, g_ref, sc_ref, sh_ref):
    ms = jnp.mean(x * x, axis=-1, keepdims=True)
    return (x * lax.rsqrt(ms + EPS) * g_ref[...]) * (1.0 + sc_ref[...]) + sh_ref[...]


def _router_kernel(x_ref, g_ref, sc_ref, sh_ref, wr_ref, wlo_ref, br_ref, route_ref, hpk_ref, *, ngroups, per_group):
    h = _norm_mod(x_ref[...], g_ref, sc_ref, sh_ref)
    half = h.shape[1] // 2
    bits = pltpu.bitcast(h.astype(BF16).astype(F32), jnp.uint32)
    hpk_ref[...] = (bits[:, half:] & jnp.uint32(HI16)) | (bits[:, :half] >> 16)
    hi = h.astype(BF16)
    lo = (h - hi.astype(F32)).astype(BF16)
    both = jnp.dot(hi, wr_ref[...], preferred_element_type=F32)
    nl = ROUTER_LANES
    logits = both[:, :nl] + both[:, nl:] + jnp.dot(lo, wlo_ref[...], preferred_element_type=F32) + br_ref[...]
    lane = lax.broadcasted_iota(jnp.int32, logits.shape, 1).astype(F32)
    big = float(nl)
    glog = jnp.where(lane < ngroups, logits, NEG_INF)
    gmax = jnp.max(glog, axis=-1, keepdims=True)
    gsum = jnp.sum(jnp.exp(glog - gmax), axis=-1, keepdims=True)
    gp = 1.0 / gsum
    gidx = jnp.min(jnp.where(glog == gmax, lane, big), axis=-1, keepdims=True)
    lo_lane = ngroups + gidx * per_group
    emask = (lane >= lo_lane) & (lane < lo_lane + per_group)
    elog = jnp.where(emask, logits, NEG_INF)
    emax = jnp.max(elog, axis=-1, keepdims=True)
    eexp = jnp.where(emask, jnp.exp(elog - emax), -1.0)
    i0 = jnp.min(jnp.where(eexp == 1.0, lane, big), axis=-1, keepdims=True)
    rest = jnp.where(lane == i0, -1.0, eexp)
    p1 = jnp.max(rest, axis=-1, keepdims=True)
    i1 = jnp.min(jnp.where(rest == p1, lane, big), axis=-1, keepdims=True)
    denom = 1.0 + p1
    w0 = gp * (1.0 / denom)
    w1 = gp * (p1 / denom)
    e0 = i0 - ngroups
    e1 = i1 - ngroups
    route_ref[...] = jnp.where(lane == 0, e0, jnp.where(lane == 1, e1, jnp.where(lane == 2, w0,
                               jnp.where(lane == 3, w1, 0.0))))


def moe_router(x, g, sc, sh, wg, bg, we, be, seq):
    t, d = x.shape
    bsz = sc.shape[0]
    ngroups = wg.shape[-1]
    per_group = we.shape[-1]
    nexp = ngroups * per_group
    nl = ROUTER_LANES
    assert ngroups + nexp <= nl
    wr = jnp.concatenate([wg, jnp.transpose(we, (1, 0, 2)).reshape(d, nexp)], axis=-1)
    wr = jnp.zeros((d, nl), F32).at[:, :ngroups + nexp].set(wr)
    w_hi = wr.astype(BF16)
    w_lo = (wr - w_hi.astype(F32)).astype(BF16)
    br = jnp.zeros((1, nl), F32).at[0, :ngroups + nexp].set(jnp.concatenate([bg, be.reshape(nexp)]))
    tm = _tile(seq, 256)
    per = seq // tm
    return pl.pallas_call(
        functools.partial(_router_kernel, ngroups=ngroups, per_group=per_group),
        grid=(t // tm,),
        in_specs=[pl.BlockSpec((tm, d), lambda i: (i, 0)),
                  pl.BlockSpec((1, d), lambda i: (0, 0)),
                  pl.BlockSpec((None, 1, d), lambda i: (i // per, 0, 0)),
                  pl.BlockSpec((None, 1, d), lambda i: (i // per, 0, 0)),
                  pl.BlockSpec((d, 2 * nl), lambda i: (0, 0)),
                  pl.BlockSpec((d, nl), lambda i: (0, 0)),
                  pl.BlockSpec((1, nl), lambda i: (0, 0))],
        out_specs=[pl.BlockSpec((tm, nl), lambda i: (i, 0)), pl.BlockSpec((tm, d // 2), lambda i: (i, 0))],
        out_shape=[jax.ShapeDtypeStruct((t, nl), F32), jax.ShapeDtypeStruct((t, d // 2), jnp.uint32)],
        compiler_params=_cparams(("arbitrary",)),
        name="moe_router",
    )(x, g.reshape(1, d), sc.reshape(bsz, 1, d), sh.reshape(bsz, 1, d),
      jnp.concatenate([w_hi, w_lo], axis=-1), w_hi, br)


def _dispatch_kernel(pos_ref, zf_ref, hpk_hbm, xs_hbm, zbuf, sem, zsem, *, tm, tme, ntok, ntile, topk):
    i = pl.program_id(0)
    nsteps = pl.num_programs(0)
    slot = lax.rem(i, 2)

    @pl.when(i == 0)
    def _():
        zbuf[...] = jnp.zeros_like(zbuf)

        def zcopy(tl):
            return pltpu.make_async_copy(zbuf, xs_hbm.at[pl.ds(tl * tme, tme)], zsem)

        def zstart(tl, _):
            @pl.when(zf_ref[tl] > 0)
            def _():
                zcopy(tl).start()
            return 0

        def zwait(tl, _):
            @pl.when(zf_ref[tl] > 0)
            def _():
                zcopy(tl).wait()
            return 0

        lax.fori_loop(0, ntile, zstart, 0)
        lax.fori_loop(0, ntile, zwait, 0)

    base = i * tm

    def issue(r, _):
        for kk in range(topk):
            pltpu.make_async_copy(hpk_hbm.at[pl.ds(base + r, 1)],
                                  xs_hbm.at[pl.ds(pos_ref[kk * ntok + base + r], 1)], sem.at[slot]).start()
        return 0

    lax.fori_loop(0, tm, issue, 0, unroll=4)

    def drain(s):
        for _ in range(topk):
            pltpu.make_async_copy(hpk_hbm.at[pl.ds(0, tm)], xs_hbm.at[pl.ds(0, tm)], sem.at[s]).wait()

    @pl.when(i > 0)
    def _():
        drain(1 - slot)

    @pl.when(i == nsteps - 1)
    def _():
        drain(slot)


W_CHUNKS = 4
W_STAGES = 4


def _expert_kernel(texp_ref, tvalid_ref, xidx_ref, slot_ref, first_ref, nxt_ref, clo_ref, chi_ref,
                   xs_ref, wg_hbm, wu_hbm, wd_hbm, y_ref, wg_buf, wu_buf, wd_buf, st_a, st_d, sem,
                   *, half, layer, d, f):
    i = pl.program_id(0)
    nch = 3 * W_CHUNKS
    ra = d // W_CHUNKS
    rd = f // W_CHUNKS

    def chunk_dma(e, which, part, b):
        if which == 2:
            return pltpu.make_async_copy(wd_hbm.at[layer, e, pl.ds(pl.multiple_of(part * rd, rd), rd), :],
                                         st_d.at[b], sem.at[b])
        src = wg_hbm if which == 0 else wu_hbm
        return pltpu.make_async_copy(src.at[layer, e, pl.ds(pl.multiple_of(part * ra, ra), ra), :],
                                     st_a.at[b], sem.at[b])

    def for_chunk(c, fn):
        for which in range(3):
            @pl.when((c >= which * W_CHUNKS) & (c < (which + 1) * W_CHUNKS))
            def _(which=which):
                fn(which, c - which * W_CHUNKS)

    def start(e, c):
        b = lax.rem(c, W_STAGES)
        for_chunk(c, lambda which, part: chunk_dma(e, which, part, b).start())

    def finish(e, c, s):
        b = lax.rem(c, W_STAGES)

        def fn(which, part):
            chunk_dma(e, which, part, b).wait()
            if which == 2:
                wd_buf[s, pl.ds(pl.multiple_of(part * rd, rd), rd), :] = st_d[b].astype(BF16)
            else:
                dst = wg_buf if which == 0 else wu_buf
                dst[s, pl.ds(pl.multiple_of(part * ra, ra), ra), :] = st_a[b].astype(BF16)

        for_chunk(c, fn)

    slot = slot_ref[i]
    nxt = nxt_ref[i]

    @pl.when(i == 0)
    def _():
        def load_first(c, _):
            start(texp_ref[0], c)
            finish(texp_ref[0], c, 0)
            return 0
        lax.fori_loop(0, nch, load_first, 0)

    @pl.when((nxt >= 0) & (first_ref[i] > 0))
    def _():
        for c in range(W_STAGES):
            start(nxt, jnp.int32(c))

    @pl.when(tvalid_ref[i] > 0)
    def _():
        xp = xs_ref[...]
        x_lo = pltpu.bitcast(xp << 16, F32).astype(BF16)
        x_hi = pltpu.bitcast(xp & jnp.uint32(HI16), F32).astype(BF16)
        hg = (jnp.dot(x_lo, wg_buf[slot, 0:half, :], preferred_element_type=F32)
              + jnp.dot(x_hi, wg_buf[slot, half:, :], preferred_element_type=F32))
        hu = (jnp.dot(x_lo, wu_buf[slot, 0:half, :], preferred_element_type=F32)
              + jnp.dot(x_hi, wu_buf[slot, half:, :], preferred_element_type=F32))
        act = (hg * _sigmoid(hg)) * hu
        yv = jnp.dot(act.astype(BF16), wd_buf[slot], preferred_element_type=F32)
        bits = pltpu.bitcast(yv.astype(BF16).astype(F32), jnp.uint32)
        y_ref[...] = (bits[:, half:] & jnp.uint32(HI16)) | (bits[:, :half] >> 16)

    @pl.when(tvalid_ref[i] == 0)
    def _():
        y_ref[...] = jnp.zeros_like(y_ref)

    @pl.when(nxt >= 0)
    def _():
        def advance(c, _):
            finish(nxt, c, 1 - slot)

            @pl.when(c + W_STAGES < nch)
            def _():
                start(nxt, c + W_STAGES)
            return 0
        lax.fori_loop(clo_ref[i], chi_ref[i], advance, 0)


def _combine_kernel(pos_ref, x_ref, route_ref, g_ref, y_hbm, o_ref, ybuf, sem, *, tm, ntok, topk, half):
    i = pl.program_id(0)
    nsteps = pl.num_programs(0)
    slot = lax.rem(i, 2)

    def gather(step, s):
        base = step * tm

        def issue(r, _):
            for kk in range(topk):
                pltpu.make_async_copy(y_hbm.at[pl.ds(pos_ref[kk * ntok + base + r], 1)],
                                      ybuf.at[s, kk, pl.ds(r, 1)], sem.at[s]).start()
            return 0

        lax.fori_loop(0, tm, issue, 0, unroll=4)

    @pl.when(i == 0)
    def _():
        gather(0, 0)

    @pl.when(i + 1 < nsteps)
    def _():
        gather(i + 1, 1 - slot)

    for kk in range(topk):
        pltpu.make_async_copy(y_hbm.at[pl.ds(0, tm)], ybuf.at[slot, kk], sem.at[slot]).wait()

    def mix_rows(cidx, _):
        rows = pl.ds(pl.multiple_of(cidx * V7X_SUBLANES, V7X_SUBLANES), V7X_SUBLANES)
        w = route_ref[rows, :]
        ylo = yhi = None
        for kk in range(topk):
            yp = ybuf[slot, kk, rows, :]
            wk = w[:, topk + kk:topk + kk + 1]
            lo = wk * pltpu.bitcast(yp << 16, F32)
            hi = wk * pltpu.bitcast(yp & jnp.uint32(HI16), F32)
            ylo = lo if ylo is None else ylo + lo
            yhi = hi if yhi is None else yhi + hi
        o_ref[rows, 0:half] = x_ref[rows, 0:half] + g_ref[:, 0:half] * ylo
        o_ref[rows, half:] = x_ref[rows, half:] + g_ref[:, half:] * yhi
        return 0

    lax.fori_loop(0, tm // V7X_SUBLANES, mix_rows, 0, unroll=4)


def moe_layer(x, route, hpk, gate, w_gate, w_up, w_down, layer, seq):
    t, d = x.shape
    bsz = gate.shape[0]
    _, nexp, _, f = w_gate.shape
    topk = MOE_TOPK
    half = d // 2
    npair = t * topk
    tme = _tile(npair // nexp, 256) if npair // nexp >= 8 else 8
    ntile = npair // tme + nexp
    tm = _tile(seq, 256)
    per = seq // tm

    flat_e = jnp.transpose(route[:, 0:topk]).astype(jnp.int32).reshape(npair)
    onehot = (flat_e[:, None] == jnp.arange(nexp, dtype=jnp.int32)[None, :]).astype(jnp.int32)
    csum = jnp.cumsum(onehot, axis=0)
    counts = csum[-1]
    tiles_per = (counts + tme - 1) // tme
    tile_end = jnp.cumsum(tiles_per)
    tile_start = tile_end - tiles_per
    pos = jnp.sum(onehot * (tile_start[None, :] * tme + csum - 1), axis=1).astype(jnp.int32)
    n_used = tile_end[-1]
    tile_id = jnp.arange(ntile, dtype=jnp.int32)
    active = tile_id < n_used
    texp = jnp.minimum(jnp.sum((tile_id[:, None] >= tile_end[None, :]).astype(jnp.int32), axis=1), nexp - 1)
    tvalid = jnp.where(active, jnp.clip(counts[texp] - (tile_id - tile_start[texp]) * tme, 0, tme), 0)
    tvalid = tvalid.astype(jnp.int32)
    texp = jnp.where(active, texp, texp[jnp.maximum(n_used - 1, 0)]).astype(jnp.int32)
    xidx = jnp.minimum(tile_id, jnp.maximum(n_used - 1, 0)).astype(jnp.int32)
    zflag = (tvalid < tme).astype(jnp.int32)
    nchunk = 3 * W_CHUNKS
    used = (tiles_per > 0).astype(jnp.int32)
    ordinal = jnp.cumsum(used) - 1
    eids = jnp.arange(nexp, dtype=jnp.int32)
    nxt_of = jnp.min(jnp.where((eids[None, :] > eids[:, None]) & (used[None, :] > 0), eids[None, :], nexp), axis=1)
    nxt_of = jnp.where(nxt_of >= nexp, -1, nxt_of)
    kk = tile_id - tile_start[texp]
    nn = jnp.maximum(tiles_per[texp], 1)
    wslot = jnp.where(active, ordinal[texp] % 2, 0).astype(jnp.int32)
    wfirst = (active & (kk == 0)).astype(jnp.int32)
    wnxt = jnp.where(active, nxt_of[texp], -1).astype(jnp.int32)
    clo = jnp.where(active, kk * nchunk // nn, 0).astype(jnp.int32)
    chi = jnp.where(active, (kk + 1) * nchunk // nn, 0).astype(jnp.int32)

    xs = pl.pallas_call(
        functools.partial(_dispatch_kernel, tm=tm, tme=tme, ntok=t, ntile=ntile, topk=topk),
        grid_spec=pltpu.PrefetchScalarGridSpec(
            num_scalar_prefetch=2,
            grid=(t // tm,),
            in_specs=[pl.BlockSpec(memory_space=pl.ANY)],
            out_specs=pl.BlockSpec(memory_space=pl.ANY),
            scratch_shapes=[pltpu.VMEM((tme, half), jnp.uint32),
                            pltpu.SemaphoreType.DMA((2,)), pltpu.SemaphoreType.DMA(())]),
        out_shape=jax.ShapeDtypeStruct((ntile * tme, half), jnp.uint32),
        compiler_params=_cparams(("arbitrary",)),
        name="moe_dispatch",
    )(pos, zflag, hpk)

    y = pl.pallas_call(
        functools.partial(_expert_kernel, half=half, layer=layer, d=d, f=f),
        grid_spec=pltpu.PrefetchScalarGridSpec(
            num_scalar_prefetch=8,
            grid=(ntile,),
            in_specs=[pl.BlockSpec((tme, half), lambda i, te, tv, xi, *_: (xi[i], 0)),
                      pl.BlockSpec(memory_space=pl.ANY),
                      pl.BlockSpec(memory_space=pl.ANY),
                      pl.BlockSpec(memory_space=pl.ANY)],
            out_specs=pl.BlockSpec((tme, half), lambda i, *_: (i, 0)),
            scratch_shapes=[pltpu.VMEM((2, d, f), BF16), pltpu.VMEM((2, d, f), BF16), pltpu.VMEM((2, f, d), BF16),
                            pltpu.VMEM((W_STAGES, d // W_CHUNKS, f), F32),
                            pltpu.VMEM((W_STAGES, f // W_CHUNKS, d), F32),
                            pltpu.SemaphoreType.DMA((W_STAGES,))]),
        out_shape=jax.ShapeDtypeStruct((ntile * tme, half), jnp.uint32),
        compiler_params=_cparams(("arbitrary",)),
        name="moe_experts",
    )(texp, tvalid, xidx, wslot, wfirst, wnxt, clo, chi, xs, w_gate, w_up, w_down)

    return pl.pallas_call(
        functools.partial(_combine_kernel, tm=tm, ntok=t, topk=topk, half=half),
        grid_spec=pltpu.PrefetchScalarGridSpec(
            num_scalar_prefetch=1,
            grid=(t // tm,),
            in_specs=[pl.BlockSpec((tm, d), lambda i, *_: (i, 0)),
                      pl.BlockSpec((tm, ROUTER_LANES), lambda i, *_: (i, 0)),
                      pl.BlockSpec((None, 1, d), lambda i, *_: (i // per, 0, 0)),
                      pl.BlockSpec(memory_space=pl.ANY)],
            out_specs=pl.BlockSpec((tm, d), lambda i, *_: (i, 0)),
            scratch_shapes=[pltpu.VMEM((2, topk, tm, half), jnp.uint32), pltpu.SemaphoreType.DMA((2,))]),
        out_shape=jax.ShapeDtypeStruct((t, d), F32),
        compiler_params=_cparams(("arbitrary",)),
        name="moe_combine",
    )(pos, x, route, gate.reshape(bsz, 1, d), y)


def kernel(x, c, norm1_g, norm2_g, ada_w, ada_b, ab_w_in, ab_w_out, lru_conv_w, lru_conv_b, lru_wa, lru_ba, lru_wx, lru_bx, lru_lambda, s5_lambda_re, s5_lambda_im, s5_log_dt, s5_b_re, s5_b_im, s5_c_re, s5_c_im, s5_d, s5_glu_w, s5_glu_b, cd_w_in, cd_w_out, sg_ln_g, sg_ln_b, sg_w, sg_b, da_q_norm, da_k_norm, da_lq1, da_lk1, da_lq2, da_lk2, da_sub_g, rel_bias, moe_wg, moe_bg, moe_we, moe_be, moe_w_gate, moe_w_up, moe_w_down):
    bsz, seq, d = x.shape
    depth = norm1_g.shape[0]
    t = bsz * seq
    xt = x.reshape(t, d)
    mod = ada_modulation(c, ada_w, ada_b)

    for layer in range(depth):
        sh1, sc1, g1, sh2, sc2, g2 = [mod[layer, :, i * d:(i + 1) * d] for i in range(6)]
        hmix = norm_modulate(xt, norm1_g[layer], sc1, sh1, seq)
        j = layer // 2
        if layer % 2 == 0:
            lw = lru_conv_w.shape[-1]
            w_in = ab_w_in[j].astype(BF16)
            w_out = ab_w_out[j].astype(BF16)
            z_lru = matmul([hmix], w_in, BF16, col_off=0, ncols=2 * lw)
            z_s5 = matmul([hmix], w_in, BF16, col_off=2 * lw)
            y_a = lru_mixer(z_lru, lru_conv_w[j], lru_conv_b[j], lru_wa[j], lru_ba[j], lru_wx[j], lru_bx[j],
                            lru_lambda[j], bsz, seq)
            y_b = s5_mixer(z_s5, s5_lambda_re[j], s5_lambda_im[j], s5_log_dt[j], s5_b_re[j], s5_b_im[j],
                           s5_c_re[j], s5_c_im[j], s5_d[j], s5_glu_w[j], s5_glu_b[j], bsz, seq)
            xt = matmul([y_a, y_b], w_out, F32, res=xt, gate=g1, seq=seq)
        else:
            sgw = sg_ln_g.shape[-1]
            dqk = da_q_norm.shape[-1]
            dv = da_sub_g.shape[-1]
            heads = rel_bias.shape[1]
            daw = heads * dv
            w_in = cd_w_in[j].astype(BF16)
            w_out = cd_w_out[j].astype(BF16)
            z_sg = matmul([hmix], w_in, BF16, col_off=0, ncols=2 * sgw)
            z_qkv = matmul([hmix], w_in, BF16, col_off=2 * sgw)
            y_c = sgu_mixer(z_sg, sg_ln_g[j], sg_ln_b[j], sg_w[j], sg_b[j])
            q_gain = jnp.tile(da_q_norm[j] * (dqk ** -0.5 * LOG2E), daw // dqk).reshape(1, daw)
            k_gain = jnp.tile(da_k_norm[j], daw // dqk).reshape(1, daw)
            qn = qk_norm(z_qkv, 0, daw, q_gain, dqk)
            kn = qk_norm(z_qkv, 1, daw, k_gain, dqk)
            y_d = diff_attention(qn, kn, z_qkv, 2 * daw // dv, rel_bias, da_lq1[j], da_lk1[j], da_lq2[j], da_lk2[j],
                                 da_sub_g[j], bsz, seq, layer)
            xt = matmul([y_c, y_d], w_out, F32, res=xt, gate=g1, seq=seq)
        route, hpk = moe_router(xt, norm2_g[layer], sc2, sh2, moe_wg[layer], moe_bg[layer], moe_we[layer],
                                moe_be[layer], seq)
        xt = moe_layer(xt, route, hpk, g2, moe_w_gate, moe_w_up, moe_w_down, layer, seq)
    return xt.reshape(bsz, seq, d)
```

```python
import functools
import math

import jax
import jax.numpy as jnp
import numpy as np
from jax import lax
from jax.experimental import pallas as pl
from jax.experimental.pallas import tpu as pltpu

F32 = jnp.float32
BF16 = jnp.bfloat16

EPS = 1e-6
LRU_C = 8.0
REL_MAX_DIST = 128
MOE_TOPK = 2
NEG_INF = -1e30
LOG2E = math.log2(math.e)

V7X_LANES = 128
V7X_SUBLANES = 8
V7X_VMEM_LIMIT_BYTES = 56 * 1024 * 1024


def _cparams(semantics):
    return pltpu.CompilerParams(dimension_semantics=semantics, vmem_limit_bytes=V7X_VMEM_LIMIT_BYTES)


def _sigmoid(x):
    return 0.5 * jnp.tanh(0.5 * x) + 0.5


def _gelu(x):
    return 0.5 * x * (1.0 + jnp.tanh(math.sqrt(2.0 / math.pi) * (x + 0.044715 * (x * x * x))))


def _tile(n, want):
    t = min(n, want)
    while n % t:
        t -= 1
    return t


def _ada_kernel(c_ref, w_ref, b_ref, o_ref):
    c = c_ref[...]
    cond = c * _sigmoid(c)
    o_ref[...] = jnp.dot(cond.astype(BF16), w_ref[...].astype(BF16), preferred_element_type=F32) + b_ref[...]


def ada_modulation(c, ada_w, ada_b):
    bsz, d = c.shape
    depth, _, n = ada_w.shape
    rows = 16
    cp = jnp.zeros((rows, d), F32).at[:bsz].set(c)
    tn = _tile(n, 512)
    out = pl.pallas_call(
        _ada_kernel,
        grid=(depth, n // tn),
        in_specs=[pl.BlockSpec((rows, d), lambda l, j: (0, 0)),
                  pl.BlockSpec((None, d, tn), lambda l, j: (l, 0, j)),
                  pl.BlockSpec((None, 1, tn), lambda l, j: (l, 0, j))],
        out_specs=pl.BlockSpec((None, rows, tn), lambda l, j: (l, 0, j)),
        out_shape=jax.ShapeDtypeStruct((depth, rows, n), F32),
        compiler_params=_cparams(("arbitrary", "arbitrary")),
        name="ada_modulation",
    )(cp, ada_w, ada_b.reshape(depth, 1, n))
    return out[:, :bsz]


def _norm_mod_kernel(x_ref, g_ref, sc_ref, sh_ref, o_ref):
    x = x_ref[...]
    ms = jnp.mean(x * x, axis=-1, keepdims=True)
    y = x * lax.rsqrt(ms + EPS) * g_ref[...]
    o_ref[...] = (y * (1.0 + sc_ref[...]) + sh_ref[...]).astype(o_ref.dtype)


def norm_modulate(x, g, sc, sh, seq, out_dtype=BF16):
    t, d = x.shape
    bsz = sc.shape[0]
    tm = _tile(seq, 256)
    per = seq // tm
    return pl.pallas_call(
        _norm_mod_kernel,
        grid=(t // tm,),
        in_specs=[pl.BlockSpec((tm, d), lambda i: (i, 0)),
                  pl.BlockSpec((1, d), lambda i: (0, 0)),
                  pl.BlockSpec((None, 1, d), lambda i: (i // per, 0, 0)),
                  pl.BlockSpec((None, 1, d), lambda i: (i // per, 0, 0))],
        out_specs=pl.BlockSpec((tm, d), lambda i: (i, 0)),
        out_shape=jax.ShapeDtypeStruct((t, d), out_dtype),
        compiler_params=_cparams(("arbitrary",)),
        name="norm_modulate",
    )(x, g.reshape(1, d), sc.reshape(bsz, 1, d), sh.reshape(bsz, 1, d))


def _matmul_kernel(*refs, ksplit, has_res):
    n_lhs = len(ksplit)
    a_refs = refs[:n_lhs]
    w_ref = refs[n_lhs]
    o_ref = refs[-1]
    acc = None
    k0 = 0
    for a_ref, kk in zip(a_refs, ksplit):
        part = jnp.dot(a_ref[...], w_ref[k0:k0 + kk, :], preferred_element_type=F32)
        acc = part if acc is None else acc + part
        k0 += kk
    if has_res:
        res_ref, gate_ref = refs[n_lhs + 1], refs[n_lhs + 2]
        acc = res_ref[...] + gate_ref[...] * acc
    o_ref[...] = acc.astype(o_ref.dtype)


def matmul(lhs, w, out_dtype, col_off=0, ncols=None, res=None, gate=None, seq=None):
    m = lhs[0].shape[0]
    ktot = w.shape[0]
    ksplit = tuple(a.shape[1] for a in lhs)
    assert sum(ksplit) == ktot
    n = w.shape[1] - col_off if ncols is None else ncols
    tm = _tile(m if seq is None else seq, 1024)
    wide = res is None and jnp.dtype(out_dtype).itemsize == 2
    tn = _tile(math.gcd(n, col_off) if col_off else n, 1024 if wide else 512)
    joff = col_off // tn
    in_specs = [pl.BlockSpec((tm, kk), lambda i, j: (i, 0)) for kk in ksplit]
    in_specs += [pl.BlockSpec((ktot, tn), lambda i, j: (0, j + joff))]
    args = list(lhs) + [w]
    if res is not None:
        per = seq // tm
        bsz = gate.shape[0]
        in_specs += [pl.BlockSpec((tm, tn), lambda i, j: (i, j)),
                     pl.BlockSpec((None, 1, tn), lambda i, j: (i // per, 0, j))]
        args += [res, gate.reshape(bsz, 1, n)]
    return pl.pallas_call(
        functools.partial(_matmul_kernel, ksplit=ksplit, has_res=res is not None),
        grid=(m // tm, n // tn),
        in_specs=in_specs,
        out_specs=pl.BlockSpec((tm, tn), lambda i, j: (i, j)),
        out_shape=jax.ShapeDtypeStruct((m, n), out_dtype),
        compiler_params=_cparams(("arbitrary", "arbitrary")),
        name="matmul",
    )(*args)


def _interleave_rows(src_ref, dst_ref, nslab, nsub, sub_len):
    for sl in range(nslab):
        lanes = slice(sl * V7X_LANES, (sl + 1) * V7X_LANES)
        for r in range(nsub):
            dst_ref[sl, pl.ds(r, sub_len, stride=nsub), :] = src_ref[r * sub_len:(r + 1) * sub_len, lanes].astype(F32)


def _deinterleave_rows(src_ref, dst_ref, nslab, nsub, sub_len):
    for sl in range(nslab):
        lanes = slice(sl * V7X_LANES, (sl + 1) * V7X_LANES)
        for r in range(nsub):
            dst_ref[r * sub_len:(r + 1) * sub_len, lanes] = src_ref[sl, pl.ds(r, sub_len, stride=nsub), :].astype(
                dst_ref.dtype)


def _sublane_scan(a, b, row):
    for d in (1, 2, 4):
        keep = row >= d
        sa = jnp.where(keep, pltpu.roll(a, d, 0), 1.0)
        sb = jnp.where(keep, pltpu.roll(b, d, 0), 0.0)
        b = b + a * sb
        a = a * sa
    return a, b


def _bcast_last(x):
    return jnp.broadcast_to(x[V7X_SUBLANES - 1:V7X_SUBLANES, :], x.shape)


def _lru_kernel(x_ref, gate_ref, cw_ref, cb_ref, wax_ref, bax_ref, lam_ref, o_ref,
                xp_ref, gp_ref, yp_ref, tail_ref, a_ref, b_ref, carry_ref, *, heads, hd, tc, kconv):
    s = pl.program_id(1)
    sub = V7X_SUBLANES
    sub_len = tc // sub
    nslab = heads * hd // V7X_LANES
    spl = hd // V7X_LANES
    halo = (kconv - 1) * sub
    row = lax.broadcasted_iota(jnp.int32, (sub, V7X_LANES), 0)
    rowh = lax.broadcasted_iota(jnp.int32, (sub, hd), 0)

    @pl.when(s == 0)
    def _():
        tail_ref[...] = jnp.zeros_like(tail_ref)
        carry_ref[...] = jnp.zeros_like(carry_ref)

    for sl in range(nslab):
        lanes = slice(sl * V7X_LANES, (sl + 1) * V7X_LANES)
        for r in range(sub):
            xp_ref[sl, pl.ds(halo + r, sub_len, stride=sub), :] = x_ref[r * sub_len:(r + 1) * sub_len, lanes].astype(F32)
        for e in range(kconv - 1):
            cur = xp_ref[sl, halo + (sub_len - (kconv - 1) + e) * sub:halo + (sub_len - (kconv - 1) + e + 1) * sub, :]
            prev = tail_ref[sl, e * sub:(e + 1) * sub, :]
            xp_ref[sl, e * sub:(e + 1) * sub, :] = jnp.where(row == 0, pltpu.roll(prev, 1, 0), pltpu.roll(cur, 1, 0))
            tail_ref[sl, e * sub:(e + 1) * sub, :] = cur
    _interleave_rows(gate_ref, gp_ref, nslab, sub, sub_len)

    for h in range(heads):
        cols = slice(h * hd, (h + 1) * hd)
        parts = []
        for q in range(spl):
            sl = h * spl + q
            lanes = slice(sl * V7X_LANES, (sl + 1) * V7X_LANES)
            acc = cb_ref[:, lanes] + cw_ref[0:1, lanes] * xp_ref[sl, 0:tc, :]
            for k in range(1, kconv):
                acc = acc + cw_ref[k:k + 1, lanes] * xp_ref[sl, k * sub:k * sub + tc, :]
            parts.append(acc)
        xc = jnp.concatenate(parts, axis=-1) if spl > 1 else parts[0]
        pre = jnp.dot(xc.astype(BF16), wax_ref[h], preferred_element_type=F32) + bax_ref[h]
        r = _sigmoid(pre[:, :hd])
        gi = _sigmoid(pre[:, hd:])
        nl = -lam_ref[:, cols]
        sp = jnp.maximum(nl, 0.0) + jnp.log1p(jnp.exp(-jnp.abs(nl)))
        log_a = (-LRU_C) * r * sp
        a = jnp.exp(log_a)
        a_ref[...] = a
        b_ref[...] = jnp.sqrt(1.0 - a * a) * (gi * xc)

        def pass1(t, carry):
            hprev, pprev = carry
            rows = pl.ds(pl.multiple_of(t * sub, sub), sub)
            a = a_ref[rows, :]
            hh = a * hprev + b_ref[rows, :]
            pp = a * pprev
            b_ref[rows, :] = hh
            a_ref[rows, :] = pp
            return hh, pp

        zero = jnp.zeros((sub, hd), F32)
        hfin, pfin = lax.fori_loop(0, sub_len, pass1, (zero, zero + 1.0), unroll=4)
        cin = carry_ref[:, cols]
        ptot, hloc = _sublane_scan(pfin, hfin, rowh)
        hend = hloc + ptot * cin
        carry_ref[:, cols] = _bcast_last(hend)
        entry = jnp.where(rowh == 0, cin, pltpu.roll(hend, 1, 0))

        def pass2(t, _):
            rows = pl.ds(pl.multiple_of(t * sub, sub), sub)
            hh = b_ref[rows, :] + a_ref[rows, :] * entry
            for q in range(spl):
                sl = h * spl + q
                yp_ref[sl, rows, :] = _gelu(gp_ref[sl, rows, :]) * hh[:, q * V7X_LANES:(q + 1) * V7X_LANES]
            return 0

        lax.fori_loop(0, sub_len, pass2, 0, unroll=4)

    _deinterleave_rows(yp_ref, o_ref, nslab, sub, sub_len)


def lru_mixer(z, conv_w, conv_b, wa, ba, wx, bx, lam, bsz, seq):
    t = z.shape[0]
    heads, hd, _ = wa.shape
    w = heads * hd
    kconv = conv_w.shape[0]
    tc = _tile(seq, 256)
    ns = seq // tc
    nslab = w // V7X_LANES
    halo = (kconv - 1) * V7X_SUBLANES
    wax = jnp.concatenate([wa, wx], axis=-1).astype(BF16)
    bax = jnp.concatenate([ba.reshape(heads, 1, hd), bx.reshape(heads, 1, hd)], axis=-1)
    return pl.pallas_call(
        functools.partial(_lru_kernel, heads=heads, hd=hd, tc=tc, kconv=kconv),
        grid=(bsz, ns),
        in_specs=[pl.BlockSpec((tc, w), lambda b, s: (b * ns + s, 0)),
                  pl.BlockSpec((tc, w), lambda b, s: (b * ns + s, 1)),
                  pl.BlockSpec((kconv, w), lambda b, s: (0, 0)),
                  pl.BlockSpec((1, w), lambda b, s: (0, 0)),
                  pl.BlockSpec((heads, hd, 2 * hd), lambda b, s: (0, 0, 0)),
                  pl.BlockSpec((heads, 1, 2 * hd), lambda b, s: (0, 0, 0)),
                  pl.BlockSpec((1, w), lambda b, s: (0, 0))],
        out_specs=pl.BlockSpec((tc, w), lambda b, s: (b * ns + s, 0)),
        out_shape=jax.ShapeDtypeStruct((t, w), BF16),
        scratch_shapes=[pltpu.VMEM((nslab, halo + tc, V7X_LANES), F32),
                        pltpu.VMEM((nslab, tc, V7X_LANES), F32),
                        pltpu.VMEM((nslab, tc, V7X_LANES), F32),
                        pltpu.VMEM((nslab, halo, V7X_LANES), F32),
                        pltpu.VMEM((tc, hd), F32),
                        pltpu.VMEM((tc, hd), F32),
                        pltpu.VMEM((V7X_SUBLANES, w), F32)],
        compiler_params=_cparams(("arbitrary", "arbitrary")),
        name="lru_mixer",
    )(z, z, conv_w, conv_b.reshape(1, w), wax, bax, lam.reshape(1, w))


S5_GROUPS_PER_BLOCK = 8


def _cmul(ar, ai, br, bi):
    return ar * br - ai * bi, ar * bi + ai * br


def _s5_kernel(u_ref, lre_ref, lim_ref, ldt_ref, bre_ref, bim_ref, cre_ref, cim_ref, d_ref, gw_ref, gb_ref,
               o_ref, wb_ref, a_ref, am_ref, p_ref, carry_ref, up_ref, re_ref, im_ref, y_ref, yp_ref,
               *, tc, nblk, cin, nst, lane_chunk):
    s = pl.program_id(1)
    gn = nblk * nst
    sub = V7X_SUBLANES
    sub_len = tc // sub
    nslab = nblk * cin // V7X_LANES

    @pl.when(s == 0)
    def _():
        lr = lre_ref[...]
        li = lim_ref[...]
        dt = jnp.exp(ldt_ref[...])
        mag = jnp.exp(lr * dt)
        ar = mag * jnp.cos(li * dt)
        ai = mag * jnp.sin(li * dt)
        den = lr * lr + li * li
        zr = ar - 1.0
        cr = (zr * lr + ai * li) / den
        ci = (ai * lr - zr * li) / den
        for j in range(nblk):
            cols = slice(j * nst, (j + 1) * nst)
            br = bre_ref[j]
            bi = bim_ref[j]
            wb_ref[j, :, 0:nst] = (cr[:, cols] * br - ci[:, cols] * bi).astype(BF16)
            wb_ref[j, :, nst:2 * nst] = (cr[:, cols] * bi + ci[:, cols] * br).astype(BF16)
        a_ref[0] = jnp.broadcast_to(ar, (sub, gn))
        a_ref[1] = jnp.broadcast_to(ai, (sub, gn))
        qr, qi = ar, ai
        for _ in range(sub_len - 1):
            qr, qi = _cmul(qr, qi, ar, ai)
        row = lax.broadcasted_iota(jnp.int32, (sub, gn), 0)
        pr, pi = qr, qi
        accr = jnp.zeros((sub, gn), F32)
        acci = jnp.zeros((sub, gn), F32)
        powers = {}
        for r in range(sub):
            powers[r + 1] = (pr, pi)
            accr = jnp.where(row == r, pr, accr)
            acci = jnp.where(row == r, pi, acci)
            pr, pi = _cmul(pr, pi, qr, qi)
        p_ref[0] = accr
        p_ref[1] = acci
        for idx, dd in enumerate((1, 2, 4)):
            wr, wi = powers[dd]
            am_ref[2 * idx] = jnp.where(row >= dd, wr, 0.0)
            am_ref[2 * idx + 1] = jnp.where(row >= dd, wi, 0.0)
        carry_ref[...] = jnp.zeros_like(carry_ref)

    _interleave_rows(u_ref, up_ref, nslab, sub, sub_len)
    for j in range(nblk):
        bu = jnp.dot(up_ref[j].astype(BF16), wb_ref[j], preferred_element_type=F32)
        re_ref[:, j * nst:(j + 1) * nst] = bu[:, :nst]
        im_ref[:, j * nst:(j + 1) * nst] = bu[:, nst:]

    row_c = lax.broadcasted_iota(jnp.int32, (sub, lane_chunk), 0)
    for c in range(gn // lane_chunk):
        lsl = slice(c * lane_chunk, (c + 1) * lane_chunk)

        def pass1(t, carry, lsl=lsl):
            pr, pi = carry
            rows = pl.ds(pl.multiple_of(t * sub, sub), sub)
            ar = a_ref[0, :, lsl]
            ai = a_ref[1, :, lsl]
            hr = re_ref[rows, lsl] + (ar * pr - ai * pi)
            hi = im_ref[rows, lsl] + (ar * pi + ai * pr)
            re_ref[rows, lsl] = hr
            im_ref[rows, lsl] = hi
            return hr, hi

        zero = jnp.zeros((sub, lane_chunk), F32)
        fr, fi = lax.fori_loop(0, sub_len, pass1, (zero, zero), unroll=2)
        for idx, dd in enumerate((1, 2, 4)):
            mr = am_ref[2 * idx, :, lsl]
            mi = am_ref[2 * idx + 1, :, lsl]
            sr = pltpu.roll(fr, dd, 0)
            si = pltpu.roll(fi, dd, 0)
            fr, fi = fr + (mr * sr - mi * si), fi + (mr * si + mi * sr)
        cr_ = carry_ref[0, :, lsl]
        ci_ = carry_ref[1, :, lsl]
        pr_ = p_ref[0, :, lsl]
        pi_ = p_ref[1, :, lsl]
        er = fr + (pr_ * cr_ - pi_ * ci_)
        ei = fi + (pr_ * ci_ + pi_ * cr_)
        carry_ref[0, :, lsl] = _bcast_last(er)
        carry_ref[1, :, lsl] = _bcast_last(ei)
        sr0 = jnp.where(row_c == 0, cr_, pltpu.roll(er, 1, 0))
        si0 = jnp.where(row_c == 0, ci_, pltpu.roll(ei, 1, 0))

        def pass2(t, carry, lsl=lsl):
            qr, qi = carry
            rows = pl.ds(pl.multiple_of(t * sub, sub), sub)
            ar = a_ref[0, :, lsl]
            ai = a_ref[1, :, lsl]
            qr, qi = ar * qr - ai * qi, ar * qi + ai * qr
            re_ref[rows, lsl] = re_ref[rows, lsl] + qr
            im_ref[rows, lsl] = im_ref[rows, lsl] + qi
            return qr, qi

        lax.fori_loop(0, sub_len, pass2, (sr0, si0), unroll=2)

    for j in range(nblk):
        cols = slice(j * nst, (j + 1) * nst)
        yj = (jnp.dot(re_ref[:, cols].astype(BF16), cre_ref[j], preferred_element_type=F32)
              - jnp.dot(im_ref[:, cols].astype(BF16), cim_ref[j], preferred_element_type=F32))
        ucols = slice(j * cin, (j + 1) * cin)
        y_ref[:, ucols] = yj + d_ref[:, ucols] * up_ref[j]
    g = _gelu(y_ref[...])
    gate = _sigmoid(jnp.dot(g.astype(BF16), gw_ref[...], preferred_element_type=F32) + gb_ref[...])
    fin = g * gate
    for sl in range(nslab):
        yp_ref[sl] = fin[:, sl * V7X_LANES:(sl + 1) * V7X_LANES]
    _deinterleave_rows(yp_ref, o_ref, nslab, sub, sub_len)


def s5_mixer(u, lam_re, lam_im, log_dt, b_re, b_im, c_re, c_im, d, glu_w, glu_b, bsz, seq):
    t, ws = u.shape
    groups, nstate, gch = b_re.shape
    gpb = S5_GROUPS_PER_BLOCK
    nblk = groups // gpb
    cin = gpb * gch
    assert cin == V7X_LANES
    nst = gpb * nstate
    gn = groups * nstate
    tc = _tile(seq, 256)
    ns = seq // tc
    lane_chunk = _tile(gn, 1024)
    nslab = ws // V7X_LANES
    eye = jnp.eye(gpb, dtype=F32)

    def bdiag_in(b):
        bb = b.reshape(nblk, gpb, nstate, gch).transpose(0, 1, 3, 2)
        return (bb[:, :, :, None, :] * eye[None, :, None, :, None]).reshape(nblk, cin, nst)

    def bdiag_out(c):
        cc = c.reshape(nblk, gpb, gch, nstate).transpose(0, 1, 3, 2)
        return (cc[:, :, :, None, :] * eye[None, :, None, :, None]).reshape(nblk, nst, cin)

    const2 = lambda b, s: (0, 0)
    const3 = lambda b, s: (0, 0, 0)
    return pl.pallas_call(
        functools.partial(_s5_kernel, tc=tc, nblk=nblk, cin=cin, nst=nst, lane_chunk=lane_chunk),
        grid=(bsz, ns),
        in_specs=[pl.BlockSpec((tc, ws), lambda b, s: (b * ns + s, 0)),
                  pl.BlockSpec((1, gn), const2), pl.BlockSpec((1, gn), const2), pl.BlockSpec((1, gn), const2),
                  pl.BlockSpec((nblk, cin, nst), const3), pl.BlockSpec((nblk, cin, nst), const3),
                  pl.BlockSpec((nblk, nst, cin), const3), pl.BlockSpec((nblk, nst, cin), const3),
                  pl.BlockSpec((1, ws), const2),
                  pl.BlockSpec((ws, ws), const2),
                  pl.BlockSpec((1, ws), const2)],
        out_specs=pl.BlockSpec((tc, ws), lambda b, s: (b * ns + s, 0)),
        out_shape=jax.ShapeDtypeStruct((t, ws), BF16),
        scratch_shapes=[pltpu.VMEM((nblk, cin, 2 * nst), BF16),
                        pltpu.VMEM((2, V7X_SUBLANES, gn), F32),
                        pltpu.VMEM((6, V7X_SUBLANES, gn), F32),
                        pltpu.VMEM((2, V7X_SUBLANES, gn), F32),
                        pltpu.VMEM((2, V7X_SUBLANES, gn), F32),
                        pltpu.VMEM((nslab, tc, V7X_LANES), F32),
                        pltpu.VMEM((tc, gn), F32),
                        pltpu.VMEM((tc, gn), F32),
                        pltpu.VMEM((tc, ws), F32),
                        pltpu.VMEM((nslab, tc, V7X_LANES), F32)],
        compiler_params=_cparams(("arbitrary", "arbitrary")),
        name="s5_mixer",
    )(u, lam_re.reshape(1, gn), lam_im.reshape(1, gn),
      jnp.broadcast_to(log_dt[:, None], (groups, nstate)).reshape(1, gn),
      bdiag_in(b_re), bdiag_in(b_im), bdiag_out(c_re).astype(BF16), bdiag_out(c_im).astype(BF16),
      d.reshape(1, ws), glu_w.astype(BF16), glu_b.reshape(1, ws))


def _sgu_kernel(u_ref, v_ref, g_ref, b_ref, w_ref, bs_ref, o_ref, *, heads, hd, chunk, nchunk):
    v = _gelu(v_ref[...].astype(F32))
    mu = jnp.mean(v, axis=-1, keepdims=True)
    vc = v - mu
    var = jnp.mean(vc * vc, axis=-1, keepdims=True)
    vn = (vc * lax.rsqrt(var + EPS) * g_ref[...] + b_ref[...]).astype(BF16)
    r = lax.broadcasted_iota(jnp.int32, (chunk, chunk), 0)
    c = lax.broadcasted_iota(jnp.int32, (chunk, chunk), 1)
    tril = r >= c
    for h in range(heads):
        wh = jnp.where(tril, w_ref[h], 0.0).astype(BF16)
        cols = slice(h * hd, (h + 1) * hd)
        for n in range(nchunk):
            rows = slice(n * chunk, (n + 1) * chunk)
            gsp = jnp.dot(wh, vn[rows, cols], preferred_element_type=F32) + bs_ref[h]
            o_ref[rows, cols] = (_gelu(u_ref[rows, cols].astype(F32)) * gsp).astype(o_ref.dtype)


def sgu_mixer(z, ln_g, ln_b, w_s, b_s):
    t = z.shape[0]
    heads, chunk, _ = w_s.shape
    w = ln_g.shape[0]
    hd = w // heads
    nchunk = 2 if (t // chunk) % 2 == 0 else 1
    tm = nchunk * chunk
    bs = jnp.broadcast_to(b_s[:, :, None], (heads, chunk, hd))
    return pl.pallas_call(
        functools.partial(_sgu_kernel, heads=heads, hd=hd, chunk=chunk, nchunk=nchunk),
        grid=(t // tm,),
        in_specs=[pl.BlockSpec((tm, w), lambda i: (i, 0)),
                  pl.BlockSpec((tm, w), lambda i: (i, 1)),
                  pl.BlockSpec((1, w), lambda i: (0, 0)),
                  pl.BlockSpec((1, w), lambda i: (0, 0)),
                  pl.BlockSpec((heads, chunk, chunk), lambda i: (0, 0, 0)),
                  pl.BlockSpec((heads, chunk, hd), lambda i: (0, 0, 0))],
        out_specs=pl.BlockSpec((tm, w), lambda i: (i, 0)),
        out_shape=jax.ShapeDtypeStruct((t, w), BF16),
        compiler_params=_cparams(("arbitrary",)),
        name="sgu_mixer",
    )(z, z, ln_g.reshape(1, w), ln_b.reshape(1, w), w_s, bs)


def _qk_norm_kernel(x_ref, g_ref, o_ref, *, nseg, seg):
    x = x_ref[...].astype(F32)
    for i in range(nseg):
        cols = slice(i * seg, (i + 1) * seg)
        xs = x[:, cols]
        ms = jnp.mean(xs * xs, axis=-1, keepdims=True)
        o_ref[:, cols] = (xs * lax.rsqrt(ms + EPS) * g_ref[:, cols]).astype(o_ref.dtype)


def qk_norm(z, col_block, width, gain_row, seg):
    t = z.shape[0]
    tm = _tile(t, 512)
    return pl.pallas_call(
        functools.partial(_qk_norm_kernel, nseg=width // seg, seg=seg),
        grid=(t // tm,),
        in_specs=[pl.BlockSpec((tm, width), lambda i: (i, col_block)),
                  pl.BlockSpec((1, width), lambda i: (0, 0))],
        out_specs=pl.BlockSpec((tm, width), lambda i: (i, 0)),
        out_shape=jax.ShapeDtypeStruct((t, width), BF16),
        compiler_params=_cparams(("arbitrary",)),
        name="qk_norm",
    )(z, gain_row)


ATTN_ROW_STRIP = 128


def _attn_kernel(q_ref, k_ref, v_ref, bias_ref, lq1_ref, lk1_ref, lq2_ref, lk2_ref, sg_ref, o_ref,
                 acc0_ref, acc1_ref, m0_ref, m1_ref, l0_ref, l1_ref, *, tq, dqk, lam_init):
    qi = pl.program_id(2)
    accs, ms, ls = (acc0_ref, acc1_ref), (m0_ref, m1_ref), (l0_ref, l1_ref)
    for mp in range(2):
        ms[mp][...] = jnp.full_like(ms[mp], NEG_INF)
        ls[mp][...] = jnp.zeros_like(ls[mp])
        accs[mp][...] = jnp.zeros_like(accs[mp])
    r = lax.broadcasted_iota(jnp.int32, (tq, tq), 0)
    c = lax.broadcasted_iota(jnp.int32, (tq, tq), 1)
    causal = r >= c
    krep = tq // V7X_LANES
    vrep = accs[0].shape[1] // V7X_LANES

    def block(kb, bias_idx, masked):
        k0 = pl.multiple_of(kb * tq, tq)
        v = v_ref[pl.ds(k0, tq), :]
        for r0 in range(0, tq, ATTN_ROW_STRIP):
            rs = slice(r0, r0 + ATTN_ROW_STRIP)
            for mp in range(2):
                q = q_ref[rs, mp * dqk:(mp + 1) * dqk]
                k = k_ref[pl.ds(k0, tq), mp * dqk:(mp + 1) * dqk]
                sc = lax.dot_general(q, k, (((1,), (1,)), ((), ())), preferred_element_type=F32)
                if bias_idx is not None:
                    sc = sc + bias_ref[bias_idx, rs, :]
                if masked:
                    sc = jnp.where(causal[rs, :], sc, NEG_INF)
                m_old = ms[mp][rs, :]
                m_new = jnp.maximum(m_old, jnp.max(sc, axis=-1, keepdims=True))
                alpha = jnp.exp2(m_old - m_new)
                p = jnp.exp2(sc - jnp.tile(m_new, (1, krep)))
                ls[mp][rs, :] = alpha * ls[mp][rs, :] + jnp.sum(p, axis=-1, keepdims=True)
                accs[mp][rs, :] = (jnp.tile(alpha, (1, vrep)) * accs[mp][rs, :]
                                   + jnp.dot(p.astype(BF16), v, preferred_element_type=F32))
                ms[mp][rs, :] = m_new

    nfar = jnp.maximum(qi - 1, 0)

    def far_pair(j, _):
        block(2 * j, None, False)
        block(2 * j + 1, None, False)
        return 0

    lax.fori_loop(0, nfar // 2, far_pair, 0)

    @pl.when(nfar % 2 == 1)
    def _():
        block(nfar - 1, None, False)

    @pl.when(qi > 0)
    def _():
        block(qi - 1, 1, False)
        block(qi, 0, True)

    @pl.when(qi == 0)
    def _():
        block(qi, 0, True)

    lam = (jnp.exp(jnp.sum(lq1_ref[...] * lk1_ref[...], axis=-1, keepdims=True))
           - jnp.exp(jnp.sum(lq2_ref[...] * lk2_ref[...], axis=-1, keepdims=True)) + lam_init)
    o = (accs[0][...] * jnp.tile(1.0 / ls[0][...], (1, vrep))
         - lam * (accs[1][...] * jnp.tile(1.0 / ls[1][...], (1, vrep))))
    ms_o = jnp.mean(o * o, axis=-1, keepdims=True)
    o_ref[...] = (o * lax.rsqrt(ms_o + EPS) * sg_ref[...] * (1.0 - lam_init)).astype(o_ref.dtype)


def _t5_bucket(n, buckets):
    max_exact = buckets // 2
    nf = jnp.maximum(n, 1).astype(F32)
    large = max_exact + (jnp.log(nf / max_exact) / math.log(REL_MAX_DIST / max_exact)
                         * (buckets - max_exact)).astype(jnp.int32)
    large = jnp.minimum(large, buckets - 1)
    return jnp.where(n < max_exact, n, large)


def _toeplitz(w, rows, cols):
    nh, lw = w.shape
    flat = jnp.tile(w, (1, rows))[:, :rows * (lw - 1)]
    return flat.reshape(nh, rows, lw - 1)[:, :, :cols]


def diff_attention(qn, kn, zqkv, v_col_block, rel_bias, lq1, lk1, lq2, lk2, sub_g, bsz, seq, layer):
    t = qn.shape[0]
    buckets, heads = rel_bias.shape
    dv = sub_g.shape[0]
    dqk = dv // 2
    tq = _tile(seq, 512)
    assert tq >= REL_MAX_DIST, "far key blocks must all fall in the last relative-position bucket"
    nq = seq // tq
    lam_init = 0.8 - 0.6 * math.exp(-0.3 * layer)
    table = jnp.transpose(rel_bias.astype(F32))
    fvals = table[:, _t5_bucket(jnp.arange(2 * tq + 1, dtype=jnp.int32), buckets)]
    fvals = (fvals - fvals[:, 2 * tq:]) * LOG2E
    jj = np.arange(2 * tq)
    n_diag = np.where(jj <= tq, 0, 2 * tq - jj)
    n_near = np.where(jj < tq, tq - jj, 3 * tq - jj)
    bias = jnp.stack([_toeplitz(fvals[:, n_diag], tq, tq), _toeplitz(fvals[:, n_near], tq, tq)], axis=1)
    row = lambda b, h, i: (0, 0)
    return pl.pallas_call(
        functools.partial(_attn_kernel, tq=tq, dqk=dqk, lam_init=lam_init),
        grid=(bsz, heads, nq),
        in_specs=[pl.BlockSpec((tq, dv), lambda b, h, i: (b * nq + i, h)),
                  pl.BlockSpec((seq, dv), lambda b, h, i: (b, h)),
                  pl.BlockSpec((seq, dv), lambda b, h, i: (b, v_col_block + h)),
                  pl.BlockSpec((None, 2, tq, tq), lambda b, h, i: (h, 0, 0, 0)),
                  pl.BlockSpec((1, dqk), row), pl.BlockSpec((1, dqk), row),
                  pl.BlockSpec((1, dqk), row), pl.BlockSpec((1, dqk), row),
                  pl.BlockSpec((1, dv), row)],
        out_specs=pl.BlockSpec((tq, dv), lambda b, h, i: (b * nq + i, h)),
        out_shape=jax.ShapeDtypeStruct((t, heads * dv), BF16),
        scratch_shapes=[pltpu.VMEM((tq, dv), F32), pltpu.VMEM((tq, dv), F32),
                        pltpu.VMEM((tq, V7X_LANES), F32), pltpu.VMEM((tq, V7X_LANES), F32),
                        pltpu.VMEM((tq, V7X_LANES), F32), pltpu.VMEM((tq, V7X_LANES), F32)],
        compiler_params=_cparams(("arbitrary", "arbitrary", "arbitrary")),
        name="diff_attention",
    )(qn, kn, zqkv, bias, lq1.reshape(1, dqk), lk1.reshape(1, dqk), lq2.reshape(1, dqk), lk2.reshape(1, dqk),
      sub_g.reshape(1, dv))


ROUTER_LANES = 128
HI16 = 0xFFFF0000


def _norm_mod(x, g_ref, sc_ref, sh_ref):
    ms = jnp.mean(x * x, axis=-1, keepdims=True)
    return (x * lax.rsqrt(ms + EPS) * g_ref[...]) * (1.0 + sc_ref[...]) + sh_ref[...]


def _router_kernel(x_ref, g_ref, sc_ref, sh_ref, wr_ref, wlo_ref, br_ref, route_ref, hpk_ref, *, ngroups, per_group):
    h = _norm_mod(x_ref[...], g_ref, sc_ref, sh_ref)
    half = h.shape[1] // 2
    bits = pltpu.bitcast(h.astype(BF16).astype(F32), jnp.uint32)
    hpk_ref[...] = (bits[:, half:] & jnp.uint32(HI16)) | (bits[:, :half] >> 16)
    hi = h.astype(BF16)
    lo = (h - hi.astype(F32)).astype(BF16)
    both = jnp.dot(hi, wr_ref[...], preferred_element_type=F32)
    nl = ROUTER_LANES
    logits = both[:, :nl] + both[:, nl:] + jnp.dot(lo, wlo_ref[...], preferred_element_type=F32) + br_ref[...]
    lane = lax.broadcasted_iota(jnp.int32, logits.shape, 1).astype(F32)
    big = float(nl)
    glog = jnp.where(lane < ngroups, logits, NEG_INF)
    gmax = jnp.max(glog, axis=-1, keepdims=True)
    gsum = jnp.sum(jnp.exp(glog - gmax), axis=-1, keepdims=True)
    gp = 1.0 / gsum
    gidx = jnp.min(jnp.where(glog == gmax, lane, big), axis=-1, keepdims=True)
    lo_lane = ngroups + gidx * per_group
    emask = (lane >= lo_lane) & (lane < lo_lane + per_group)
    elog = jnp.where(emask, logits, NEG_INF)
    emax = jnp.max(elog, axis=-1, keepdims=True)
    eexp = jnp.where(emask, jnp.exp(elog - emax), -1.0)
    i0 = jnp.min(jnp.where(eexp == 1.0, lane, big), axis=-1, keepdims=True)
    rest = jnp.where(lane == i0, -1.0, eexp)
    p1 = jnp.max(rest, axis=-1, keepdims=True)
    i1 = jnp.min(jnp.where(rest == p1, lane, big), axis=-1, keepdims=True)
    denom = 1.0 + p1
    w0 = gp * (1.0 / denom)
    w1 = gp * (p1 / denom)
    e0 = i0 - ngroups
    e1 = i1 - ngroups
    route_ref[...] = jnp.where(lane == 0, e0, jnp.where(lane == 1, e1, jnp.where(lane == 2, w0,
                               jnp.where(lane == 3, w1, 0.0))))


def moe_router(x, g, sc, sh, wg, bg, we, be, seq):
    t, d = x.shape
    bsz = sc.shape[0]
    ngroups = wg.shape[-1]
    per_group = we.shape[-1]
    nexp = ngroups * per_group
    nl = ROUTER_LANES
    assert ngroups + nexp <= nl
    wr = jnp.concatenate([wg, jnp.transpose(we, (1, 0, 2)).reshape(d, nexp)], axis=-1)
    wr = jnp.zeros((d, nl), F32).at[:, :ngroups + nexp].set(wr)
    w_hi = wr.astype(BF16)
    w_lo = (wr - w_hi.astype(F32)).astype(BF16)
    br = jnp.zeros((1, nl), F32).at[0, :ngroups + nexp].set(jnp.concatenate([bg, be.reshape(nexp)]))
    tm = _tile(seq, 256)
    per = seq // tm
    return pl.pallas_call(
        functools.partial(_router_kernel, ngroups=ngroups, per_group=per_group),
        grid=(t // tm,),
        in_specs=[pl.BlockSpec((tm, d), lambda i: (i, 0)),
                  pl.BlockSpec((1, d), lambda i: (0, 0)),
                  pl.BlockSpec((None, 1, d), lambda i: (i // per, 0, 0)),
                  pl.BlockSpec((None, 1, d), lambda i: (i // per, 0, 0)),
                  pl.BlockSpec((d, 2 * nl), lambda i: (0, 0)),
                  pl.BlockSpec((d, nl), lambda i: (0, 0)),
                  pl.BlockSpec((1, nl), lambda i: (0, 0))],
        out_specs=[pl.BlockSpec((tm, nl), lambda i: (i, 0)), pl.BlockSpec((tm, d // 2), lambda i: (i, 0))],
        out_shape=[jax.ShapeDtypeStruct((t, nl), F32), jax.ShapeDtypeStruct((t, d // 2), jnp.uint32)],
        compiler_params=_cparams(("arbitrary",)),
        name="moe_router",
    )(x, g.reshape(1, d), sc.reshape(bsz, 1, d), sh.reshape(bsz, 1, d),
      jnp.concatenate([w_hi, w_lo], axis=-1), w_hi, br)


def _dispatch_kernel(pos_ref, zf_ref, hpk_ref, xs_hbm, buf, zbuf, sem, zsem, *, tm, tme, ntok, ntile, topk):
    i = pl.program_id(0)
    nsteps = pl.num_programs(0)
    slot = lax.rem(i, 2)

    @pl.when(i == 0)
    def _():
        zbuf[...] = jnp.zeros_like(zbuf)

        def zcopy(tl):
            return pltpu.make_async_copy(zbuf, xs_hbm.at[pl.ds(tl * tme, tme)], zsem)

        def zstart(tl, _):
            @pl.when(zf_ref[tl] > 0)
            def _():
                zcopy(tl).start()
            return 0

        def zwait(tl, _):
            @pl.when(zf_ref[tl] > 0)
            def _():
                zcopy(tl).wait()
            return 0

        lax.fori_loop(0, ntile, zstart, 0)
        lax.fori_loop(0, ntile, zwait, 0)

    buf[slot] = hpk_ref[...]
    base = i * tm

    def issue(r, _):
        for kk in range(topk):
            pltpu.make_async_copy(buf.at[slot, pl.ds(r, 1)],
                                  xs_hbm.at[pl.ds(pos_ref[kk * ntok + base + r], 1)], sem.at[slot]).start()
        return 0

    lax.fori_loop(0, tm, issue, 0, unroll=4)

    def drain(s):
        for _ in range(topk):
            pltpu.make_async_copy(buf.at[s], xs_hbm.at[pl.ds(0, tm)], sem.at[s]).wait()

    @pl.when(i > 0)
    def _():
        drain(1 - slot)

    @pl.when(i == nsteps - 1)
    def _():
        drain(slot)


W_CHUNKS = 4
W_STAGES = 4


def _expert_kernel(texp_ref, tvalid_ref, xidx_ref, slot_ref, first_ref, nxt_ref, clo_ref, chi_ref,
                   xs_ref, wg_hbm, wu_hbm, wd_hbm, y_ref, wg_buf, wu_buf, wd_buf, st_a, st_d, sem,
                   *, half, layer, d, f):
    i = pl.program_id(0)
    nch = 3 * W_CHUNKS
    ra = d // W_CHUNKS
    rd = f // W_CHUNKS

    def chunk_dma(e, which, part, b):
        if which == 2:
            return pltpu.make_async_copy(wd_hbm.at[layer, e, pl.ds(pl.multiple_of(part * rd, rd), rd), :],
                                         st_d.at[b], sem.at[b])
        src = wg_hbm if which == 0 else wu_hbm
        return pltpu.make_async_copy(src.at[layer, e, pl.ds(pl.multiple_of(part * ra, ra), ra), :],
                                     st_a.at[b], sem.at[b])

    def for_chunk(c, fn):
        for which in range(3):
            @pl.when((c >= which * W_CHUNKS) & (c < (which + 1) * W_CHUNKS))
            def _(which=which):
                fn(which, c - which * W_CHUNKS)

    def start(e, c):
        b = lax.rem(c, W_STAGES)
        for_chunk(c, lambda which, part: chunk_dma(e, which, part, b).start())

    def finish(e, c, s):
        b = lax.rem(c, W_STAGES)

        def fn(which, part):
            chunk_dma(e, which, part, b).wait()
            if which == 2:
                wd_buf[s, pl.ds(pl.multiple_of(part * rd, rd), rd), :] = st_d[b].astype(BF16)
            else:
                dst = wg_buf if which == 0 else wu_buf
                dst[s, pl.ds(pl.multiple_of(part * ra, ra), ra), :] = st_a[b].astype(BF16)

        for_chunk(c, fn)

    slot = slot_ref[i]
    nxt = nxt_ref[i]

    @pl.when(i == 0)
    def _():
        def load_first(c, _):
            start(texp_ref[0], c)
            finish(texp_ref[0], c, 0)
            return 0
        lax.fori_loop(0, nch, load_first, 0)

    @pl.when((nxt >= 0) & (first_ref[i] > 0))
    def _():
        for c in range(W_STAGES):
            start(nxt, jnp.int32(c))

    @pl.when(tvalid_ref[i] > 0)
    def _():
        xp = xs_ref[...]
        x_lo = pltpu.bitcast(xp << 16, F32).astype(BF16)
        x_hi = pltpu.bitcast(xp & jnp.uint32(HI16), F32).astype(BF16)
        hg = (jnp.dot(x_lo, wg_buf[slot, 0:half, :], preferred_element_type=F32)
              + jnp.dot(x_hi, wg_buf[slot, half:, :], preferred_element_type=F32))
        hu = (jnp.dot(x_lo, wu_buf[slot, 0:half, :], preferred_element_type=F32)
              + jnp.dot(x_hi, wu_buf[slot, half:, :], preferred_element_type=F32))
        act = (hg * _sigmoid(hg)) * hu
        yv = jnp.dot(act.astype(BF16), wd_buf[slot], preferred_element_type=F32)
        bits = pltpu.bitcast(yv.astype(BF16).astype(F32), jnp.uint32)
        y_ref[...] = (bits[:, half:] & jnp.uint32(HI16)) | (bits[:, :half] >> 16)

    @pl.when(tvalid_ref[i] == 0)
    def _():
        y_ref[...] = jnp.zeros_like(y_ref)

    @pl.when(nxt >= 0)
    def _():
        def advance(c, _):
            finish(nxt, c, 1 - slot)

            @pl.when(c + W_STAGES < nch)
            def _():
                start(nxt, c + W_STAGES)
            return 0
        lax.fori_loop(clo_ref[i], chi_ref[i], advance, 0)


def _combine_kernel(pos_ref, x_ref, route_ref, g_ref, y_hbm, *rest, tm, ntok, topk, half, with_norm):
    if with_norm:
        ng_ref, nsc_ref, nsh_ref, o_ref, hn_ref, ybuf, sem = rest
    else:
        o_ref, ybuf, sem = rest
    i = pl.program_id(0)
    nsteps = pl.num_programs(0)
    slot = lax.rem(i, 2)

    def gather(step, s):
        base = step * tm

        def issue(r, _):
            for kk in range(topk):
                pltpu.make_async_copy(y_hbm.at[pl.ds(pos_ref[kk * ntok + base + r], 1)],
                                      ybuf.at[s, kk, pl.ds(r, 1)], sem.at[s]).start()
            return 0

        lax.fori_loop(0, tm, issue, 0, unroll=4)

    @pl.when(i == 0)
    def _():
        gather(0, 0)

    @pl.when(i + 1 < nsteps)
    def _():
        gather(i + 1, 1 - slot)

    for kk in range(topk):
        pltpu.make_async_copy(y_hbm.at[pl.ds(0, tm)], ybuf.at[slot, kk], sem.at[slot]).wait()

    def mix_rows(cidx, _):
        rows = pl.ds(pl.multiple_of(cidx * V7X_SUBLANES, V7X_SUBLANES), V7X_SUBLANES)
        w = route_ref[rows, :]
        ylo = yhi = None
        for kk in range(topk):
            yp = ybuf[slot, kk, rows, :]
            wk = w[:, topk + kk:topk + kk + 1]
            lo = wk * pltpu.bitcast(yp << 16, F32)
            hi = wk * pltpu.bitcast(yp & jnp.uint32(HI16), F32)
            ylo = lo if ylo is None else ylo + lo
            yhi = hi if yhi is None else yhi + hi
        olo = x_ref[rows, 0:half] + g_ref[:, 0:half] * ylo
        ohi = x_ref[rows, half:] + g_ref[:, half:] * yhi
        o_ref[rows, 0:half] = olo
        o_ref[rows, half:] = ohi
        if with_norm:
            ms = (jnp.sum(olo * olo, axis=-1, keepdims=True) + jnp.sum(ohi * ohi, axis=-1, keepdims=True)) / (2 * half)
            rs = lax.rsqrt(ms + EPS)
            for cols, ov in ((slice(0, half), olo), (slice(half, 2 * half), ohi)):
                hn = (ov * rs * ng_ref[:, cols]) * (1.0 + nsc_ref[:, cols]) + nsh_ref[:, cols]
                hn_ref[rows, cols] = hn.astype(hn_ref.dtype)
        return 0

    lax.fori_loop(0, tm // V7X_SUBLANES, mix_rows, 0, unroll=4)


def moe_layer(x, route, hpk, gate, w_gate, w_up, w_down, layer, seq, next_norm=None):
    t, d = x.shape
    bsz = gate.shape[0]
    _, nexp, _, f = w_gate.shape
    topk = MOE_TOPK
    half = d // 2
    npair = t * topk
    tme = _tile(npair // nexp, 256) if npair // nexp >= 8 else 8
    ntile = npair // tme + nexp
    tm = _tile(seq, 256)
    per = seq // tm

    flat_e = jnp.transpose(route[:, 0:topk]).astype(jnp.int32).reshape(npair)
    onehot = (flat_e[:, None] == jnp.arange(nexp, dtype=jnp.int32)[None, :]).astype(jnp.int32)
    csum = jnp.cumsum(onehot, axis=0)
    counts = csum[-1]
    tiles_per = (counts + tme - 1) // tme
    tile_end = jnp.cumsum(tiles_per)
    tile_start = tile_end - tiles_per
    pos = jnp.sum(onehot * (tile_start[None, :] * tme + csum - 1), axis=1).astype(jnp.int32)
    n_used = tile_end[-1]
    tile_id = jnp.arange(ntile, dtype=jnp.int32)
    active = tile_id < n_used
    texp = jnp.minimum(jnp.sum((tile_id[:, None] >= tile_end[None, :]).astype(jnp.int32), axis=1), nexp - 1)
    tvalid = jnp.where(active, jnp.clip(counts[texp] - (tile_id - tile_start[texp]) * tme, 0, tme), 0)
    tvalid = tvalid.astype(jnp.int32)
    texp = jnp.where(active, texp, texp[jnp.maximum(n_used - 1, 0)]).astype(jnp.int32)
    xidx = jnp.minimum(tile_id, jnp.maximum(n_used - 1, 0)).astype(jnp.int32)
    zflag = (tvalid < tme).astype(jnp.int32)
    nchunk = 3 * W_CHUNKS
    used = (tiles_per > 0).astype(jnp.int32)
    ordinal = jnp.cumsum(used) - 1
    eids = jnp.arange(nexp, dtype=jnp.int32)
    nxt_of = jnp.min(jnp.where((eids[None, :] > eids[:, None]) & (used[None, :] > 0), eids[None, :], nexp), axis=1)
    nxt_of = jnp.where(nxt_of >= nexp, -1, nxt_of)
    kk = tile_id - tile_start[texp]
    nn = jnp.maximum(tiles_per[texp], 1)
    wslot = jnp.where(active, ordinal[texp] % 2, 0).astype(jnp.int32)
    wfirst = (active & (kk == 0)).astype(jnp.int32)
    wnxt = jnp.where(active, nxt_of[texp], -1).astype(jnp.int32)
    clo = jnp.where(active, kk * nchunk // nn, 0).astype(jnp.int32)
    chi = jnp.where(active, (kk + 1) * nchunk // nn, 0).astype(jnp.int32)

    xs = pl.pallas_call(
        functools.partial(_dispatch_kernel, tm=tm, tme=tme, ntok=t, ntile=ntile, topk=topk),
        grid_spec=pltpu.PrefetchScalarGridSpec(
            num_scalar_prefetch=2,
            grid=(t // tm,),
            in_specs=[pl.BlockSpec((tm, half), lambda i, *_: (i, 0))],
            out_specs=pl.BlockSpec(memory_space=pl.ANY),
            scratch_shapes=[pltpu.VMEM((2, tm, half), jnp.uint32), pltpu.VMEM((tme, half), jnp.uint32),
                            pltpu.SemaphoreType.DMA((2,)), pltpu.SemaphoreType.DMA(())]),
        out_shape=jax.ShapeDtypeStruct((ntile * tme, half), jnp.uint32),
        compiler_params=_cparams(("arbitrary",)),
        name="moe_dispatch",
    )(pos, zflag, hpk)

    y = pl.pallas_call(
        functools.partial(_expert_kernel, half=half, layer=layer, d=d, f=f),
        grid_spec=pltpu.PrefetchScalarGridSpec(
            num_scalar_prefetch=8,
            grid=(ntile,),
            in_specs=[pl.BlockSpec((tme, half), lambda i, te, tv, xi, *_: (xi[i], 0)),
                      pl.BlockSpec(memory_space=pl.ANY),
                      pl.BlockSpec(memory_space=pl.ANY),
                      pl.BlockSpec(memory_space=pl.ANY)],
            out_specs=pl.BlockSpec((tme, half), lambda i, *_: (i, 0)),
            scratch_shapes=[pltpu.VMEM((2, d, f), BF16), pltpu.VMEM((2, d, f), BF16), pltpu.VMEM((2, f, d), BF16),
                            pltpu.VMEM((W_STAGES, d // W_CHUNKS, f), F32),
                            pltpu.VMEM((W_STAGES, f // W_CHUNKS, d), F32),
                            pltpu.SemaphoreType.DMA((W_STAGES,))]),
        out_shape=jax.ShapeDtypeStruct((ntile * tme, half), jnp.uint32),
        compiler_params=_cparams(("arbitrary",)),
        name="moe_experts",
    )(texp, tvalid, xidx, wslot, wfirst, wnxt, clo, chi, xs, w_gate, w_up, w_down)

    row_spec = pl.BlockSpec((tm, d), lambda i, *_: (i, 0))
    batch_spec = pl.BlockSpec((None, 1, d), lambda i, *_: (i // per, 0, 0))
    in_specs = [row_spec, pl.BlockSpec((tm, ROUTER_LANES), lambda i, *_: (i, 0)), batch_spec,
                pl.BlockSpec(memory_space=pl.ANY)]
    args = [pos, x, route, gate.reshape(bsz, 1, d), y]
    out_specs, out_shape = row_spec, jax.ShapeDtypeStruct((t, d), F32)
    if next_norm is not None:
        ng, nsc, nsh = next_norm
        in_specs += [pl.BlockSpec((1, d), lambda i, *_: (0, 0)), batch_spec, batch_spec]
        args += [ng.reshape(1, d), nsc.reshape(bsz, 1, d), nsh.reshape(bsz, 1, d)]
        out_specs = [row_spec, row_spec]
        out_shape = [out_shape, jax.ShapeDtypeStruct((t, d), BF16)]
    return pl.pallas_call(
        functools.partial(_combine_kernel, tm=tm, ntok=t, topk=topk, half=half, with_norm=next_norm is not None),
        grid_spec=pltpu.PrefetchScalarGridSpec(
            num_scalar_prefetch=1,
            grid=(t // tm,),
            in_specs=in_specs,
            out_specs=out_specs,
            scratch_shapes=[pltpu.VMEM((2, topk, tm, half), jnp.uint32), pltpu.SemaphoreType.DMA((2,))]),
        out_shape=out_shape,
        compiler_params=_cparams(("arbitrary",)),
        name="moe_combine",
    )(*args)


def kernel(x, c, norm1_g, norm2_g, ada_w, ada_b, ab_w_in, ab_w_out, lru_conv_w, lru_conv_b, lru_wa, lru_ba, lru_wx, lru_bx, lru_lambda, s5_lambda_re, s5_lambda_im, s5_log_dt, s5_b_re, s5_b_im, s5_c_re, s5_c_im, s5_d, s5_glu_w, s5_glu_b, cd_w_in, cd_w_out, sg_ln_g, sg_ln_b, sg_w, sg_b, da_q_norm, da_k_norm, da_lq1, da_lk1, da_lq2, da_lk2, da_sub_g, rel_bias, moe_wg, moe_bg, moe_we, moe_be, moe_w_gate, moe_w_up, moe_w_down):
    bsz, seq, d = x.shape
    depth = norm1_g.shape[0]
    t = bsz * seq
    xt = x.reshape(t, d)
    mod = ada_modulation(c, ada_w, ada_b)

    hmix = None
    for layer in range(depth):
        sh1, sc1, g1, sh2, sc2, g2 = [mod[layer, :, i * d:(i + 1) * d] for i in range(6)]
        if hmix is None:
            hmix = norm_modulate(xt, norm1_g[layer], sc1, sh1, seq)
        j = layer // 2
        if layer % 2 == 0:
            lw = lru_conv_w.shape[-1]
            w_in = ab_w_in[j].astype(BF16)
            w_out = ab_w_out[j].astype(BF16)
            z_lru = matmul([hmix], w_in, BF16, col_off=0, ncols=2 * lw)
            z_s5 = matmul([hmix], w_in, BF16, col_off=2 * lw)
            y_a = lru_mixer(z_lru, lru_conv_w[j], lru_conv_b[j], lru_wa[j], lru_ba[j], lru_wx[j], lru_bx[j],
                            lru_lambda[j], bsz, seq)
            y_b = s5_mixer(z_s5, s5_lambda_re[j], s5_lambda_im[j], s5_log_dt[j], s5_b_re[j], s5_b_im[j],
                           s5_c_re[j], s5_c_im[j], s5_d[j], s5_glu_w[j], s5_glu_b[j], bsz, seq)
            xt = matmul([y_a, y_b], w_out, F32, res=xt, gate=g1, seq=seq)
        else:
            sgw = sg_ln_g.shape[-1]
            dqk = da_q_norm.shape[-1]
            dv = da_sub_g.shape[-1]
            heads = rel_bias.shape[1]
            daw = heads * dv
            w_in = cd_w_in[j].astype(BF16)
            w_out = cd_w_out[j].astype(BF16)
            z_sg = matmul([hmix], w_in, BF16, col_off=0, ncols=2 * sgw)
            z_qkv = matmul([hmix], w_in, BF16, col_off=2 * sgw)
            y_c = sgu_mixer(z_sg, sg_ln_g[j], sg_ln_b[j], sg_w[j], sg_b[j])
            q_gain = jnp.tile(da_q_norm[j] * (dqk ** -0.5 * LOG2E), daw // dqk).reshape(1, daw)
            k_gain = jnp.tile(da_k_norm[j], daw // dqk).reshape(1, daw)
            qn = qk_norm(z_qkv, 0, daw, q_gain, dqk)
            kn = qk_norm(z_qkv, 1, daw, k_gain, dqk)
            y_d = diff_attention(qn, kn, z_qkv, 2 * daw // dv, rel_bias, da_lq1[j], da_lk1[j], da_lq2[j], da_lk2[j],
                                 da_sub_g[j], bsz, seq, layer)
            xt = matmul([y_c, y_d], w_out, F32, res=xt, gate=g1, seq=seq)
        route, hpk = moe_router(xt, norm2_g[layer], sc2, sh2, moe_wg[layer], moe_bg[layer], moe_we[layer],
                                moe_be[layer], seq)
        if layer + 1 < depth:
            nxt = (norm1_g[layer + 1], mod[layer + 1, :, d:2 * d], mod[layer + 1, :, 0:d])
            xt, hmix = moe_layer(xt, route, hpk, g2, moe_w_gate, moe_w_up, moe_w_down, layer, seq, next_norm=nxt)
        else:
            xt = moe_layer(xt, route, hpk, g2, moe_w_gate, moe_w_up, moe_w_down, layer, seq)
    return xt.reshape(bsz, seq, d)
```

```python
import functools
import math

import jax
import jax.numpy as jnp
import numpy as np
from jax import lax
from jax.experimental import pallas as pl
from jax.experimental.pallas import tpu as pltpu

F32 = jnp.float32
BF16 = jnp.bfloat16

EPS = 1e-6
LRU_C = 8.0
REL_MAX_DIST = 128
MOE_TOPK = 2
NEG_INF = -1e30
LOG2E = math.log2(math.e)

V7X_LANES = 128
V7X_SUBLANES = 8
V7X_VMEM_LIMIT_BYTES = 56 * 1024 * 1024


def _cparams(semantics):
    return pltpu.CompilerParams(dimension_semantics=semantics, vmem_limit_bytes=V7X_VMEM_LIMIT_BYTES)


def _sigmoid(x):
    return 0.5 * jnp.tanh(0.5 * x) + 0.5


def _gelu(x):
    return 0.5 * x * (1.0 + jnp.tanh(math.sqrt(2.0 / math.pi) * (x + 0.044715 * (x * x * x))))


def _tile(n, want):
    t = min(n, want)
    while n % t:
        t -= 1
    return t


def _ada_kernel(c_ref, w_ref, b_ref, o_ref):
    c = c_ref[...]
    cond = c * _sigmoid(c)
    o_ref[...] = jnp.dot(cond.astype(BF16), w_ref[...].astype(BF16), preferred_element_type=F32) + b_ref[...]


def ada_modulation(c, ada_w, ada_b):
    bsz, d = c.shape
    depth, _, n = ada_w.shape
    rows = 16
    cp = jnp.zeros((rows, d), F32).at[:bsz].set(c)
    tn = _tile(n, 512)
    out = pl.pallas_call(
        _ada_kernel,
        grid=(depth, n // tn),
        in_specs=[pl.BlockSpec((rows, d), lambda l, j: (0, 0)),
                  pl.BlockSpec((None, d, tn), lambda l, j: (l, 0, j)),
                  pl.BlockSpec((None, 1, tn), lambda l, j: (l, 0, j))],
        out_specs=pl.BlockSpec((None, rows, tn), lambda l, j: (l, 0, j)),
        out_shape=jax.ShapeDtypeStruct((depth, rows, n), F32),
        compiler_params=_cparams(("arbitrary", "arbitrary")),
        name="ada_modulation",
    )(cp, ada_w, ada_b.reshape(depth, 1, n))
    return out[:, :bsz]


def _norm_mod_kernel(x_ref, g_ref, sc_ref, sh_ref, o_ref):
    x = x_ref[...]
    ms = jnp.mean(x * x, axis=-1, keepdims=True)
    y = x * lax.rsqrt(ms + EPS) * g_ref[...]
    o_ref[...] = (y * (1.0 + sc_ref[...]) + sh_ref[...]).astype(o_ref.dtype)


def norm_modulate(x, g, sc, sh, seq, out_dtype=BF16):
    t, d = x.shape
    bsz = sc.shape[0]
    tm = _tile(seq, 256)
    per = seq // tm
    return pl.pallas_call(
        _norm_mod_kernel,
        grid=(t // tm,),
        in_specs=[pl.BlockSpec((tm, d), lambda i: (i, 0)),
                  pl.BlockSpec((1, d), lambda i: (0, 0)),
                  pl.BlockSpec((None, 1, d), lambda i: (i // per, 0, 0)),
                  pl.BlockSpec((None, 1, d), lambda i: (i // per, 0, 0))],
        out_specs=pl.BlockSpec((tm, d), lambda i: (i, 0)),
        out_shape=jax.ShapeDtypeStruct((t, d), out_dtype),
        compiler_params=_cparams(("arbitrary",)),
        name="norm_modulate",
    )(x, g.reshape(1, d), sc.reshape(bsz, 1, d), sh.reshape(bsz, 1, d))


def _matmul_kernel(*refs, ksplit, has_res):
    n_lhs = len(ksplit)
    a_refs = refs[:n_lhs]
    w_ref = refs[n_lhs]
    o_ref = refs[-1]
    acc = None
    k0 = 0
    for a_ref, kk in zip(a_refs, ksplit):
        part = jnp.dot(a_ref[...], w_ref[k0:k0 + kk, :], preferred_element_type=F32)
        acc = part if acc is None else acc + part
        k0 += kk
    if has_res:
        res_ref, gate_ref = refs[n_lhs + 1], refs[n_lhs + 2]
        acc = res_ref[...] + gate_ref[...] * acc
    o_ref[...] = acc.astype(o_ref.dtype)


def matmul(lhs, w, out_dtype, col_off=0, ncols=None, res=None, gate=None, seq=None):
    m = lhs[0].shape[0]
    ktot = w.shape[0]
    ksplit = tuple(a.shape[1] for a in lhs)
    assert sum(ksplit) == ktot
    n = w.shape[1] - col_off if ncols is None else ncols
    tm = _tile(m if seq is None else seq, 1024)
    wide = res is None and jnp.dtype(out_dtype).itemsize == 2
    tn = _tile(math.gcd(n, col_off) if col_off else n, 1024 if wide else 512)
    joff = col_off // tn
    in_specs = [pl.BlockSpec((tm, kk), lambda i, j: (i, 0)) for kk in ksplit]
    in_specs += [pl.BlockSpec((ktot, tn), lambda i, j: (0, j + joff))]
    args = list(lhs) + [w]
    if res is not None:
        per = seq // tm
        bsz = gate.shape[0]
        in_specs += [pl.BlockSpec((tm, tn), lambda i, j: (i, j)),
                     pl.BlockSpec((None, 1, tn), lambda i, j: (i // per, 0, j))]
        args += [res, gate.reshape(bsz, 1, n)]
    return pl.pallas_call(
        functools.partial(_matmul_kernel, ksplit=ksplit, has_res=res is not None),
        grid=(m // tm, n // tn),
        in_specs=in_specs,
        out_specs=pl.BlockSpec((tm, tn), lambda i, j: (i, j)),
        out_shape=jax.ShapeDtypeStruct((m, n), out_dtype),
        compiler_params=_cparams(("arbitrary", "arbitrary")),
        name="matmul",
    )(*args)


def _interleave_rows(src_ref, dst_ref, nslab, nsub, sub_len):
    for sl in range(nslab):
        lanes = slice(sl * V7X_LANES, (sl + 1) * V7X_LANES)
        for r in range(nsub):
            dst_ref[sl, pl.ds(r, sub_len, stride=nsub), :] = src_ref[r * sub_len:(r + 1) * sub_len, lanes].astype(F32)


def _deinterleave_rows(src_ref, dst_ref, nslab, nsub, sub_len):
    for sl in range(nslab):
        lanes = slice(sl * V7X_LANES, (sl + 1) * V7X_LANES)
        for r in range(nsub):
            dst_ref[r * sub_len:(r + 1) * sub_len, lanes] = src_ref[sl, pl.ds(r, sub_len, stride=nsub), :].astype(
                dst_ref.dtype)


def _sublane_scan(a, b, row):
    for d in (1, 2, 4):
        keep = row >= d
        sa = jnp.where(keep, pltpu.roll(a, d, 0), 1.0)
        sb = jnp.where(keep, pltpu.roll(b, d, 0), 0.0)
        b = b + a * sb
        a = a * sa
    return a, b


def _bcast_last(x):
    return jnp.broadcast_to(x[V7X_SUBLANES - 1:V7X_SUBLANES, :], x.shape)


def _lru_kernel(x_ref, gate_ref, cw_ref, cb_ref, wax_ref, bax_ref, lam_ref, o_ref,
                xp_ref, gp_ref, yp_ref, tail_ref, a_ref, b_ref, carry_ref, *, heads, hd, tc, kconv):
    s = pl.program_id(1)
    sub = V7X_SUBLANES
    sub_len = tc // sub
    nslab = heads * hd // V7X_LANES
    spl = hd // V7X_LANES
    halo = (kconv - 1) * sub
    row = lax.broadcasted_iota(jnp.int32, (sub, V7X_LANES), 0)
    rowh = lax.broadcasted_iota(jnp.int32, (sub, hd), 0)

    @pl.when(s == 0)
    def _():
        tail_ref[...] = jnp.zeros_like(tail_ref)
        carry_ref[...] = jnp.zeros_like(carry_ref)

    for sl in range(nslab):
        lanes = slice(sl * V7X_LANES, (sl + 1) * V7X_LANES)
        for r in range(sub):
            xp_ref[sl, pl.ds(halo + r, sub_len, stride=sub), :] = x_ref[r * sub_len:(r + 1) * sub_len, lanes].astype(F32)
        for e in range(kconv - 1):
            cur = xp_ref[sl, halo + (sub_len - (kconv - 1) + e) * sub:halo + (sub_len - (kconv - 1) + e + 1) * sub, :]
            prev = tail_ref[sl, e * sub:(e + 1) * sub, :]
            xp_ref[sl, e * sub:(e + 1) * sub, :] = jnp.where(row == 0, pltpu.roll(prev, 1, 0), pltpu.roll(cur, 1, 0))
            tail_ref[sl, e * sub:(e + 1) * sub, :] = cur
    _interleave_rows(gate_ref, gp_ref, nslab, sub, sub_len)

    for h in range(heads):
        cols = slice(h * hd, (h + 1) * hd)
        parts = []
        for q in range(spl):
            sl = h * spl + q
            lanes = slice(sl * V7X_LANES, (sl + 1) * V7X_LANES)
            acc = cb_ref[:, lanes] + cw_ref[0:1, lanes] * xp_ref[sl, 0:tc, :]
            for k in range(1, kconv):
                acc = acc + cw_ref[k:k + 1, lanes] * xp_ref[sl, k * sub:k * sub + tc, :]
            parts.append(acc)
        xc = jnp.concatenate(parts, axis=-1) if spl > 1 else parts[0]
        pre = jnp.dot(xc.astype(BF16), wax_ref[h], preferred_element_type=F32) + bax_ref[h]
        r = _sigmoid(pre[:, :hd])
        gi = _sigmoid(pre[:, hd:])
        nl = -lam_ref[:, cols]
        sp = jnp.maximum(nl, 0.0) + jnp.log1p(jnp.exp(-jnp.abs(nl)))
        log_a = (-LRU_C) * r * sp
        a = jnp.exp(log_a)
        a_ref[...] = a
        b_ref[...] = jnp.sqrt(1.0 - a * a) * (gi * xc)

        def pass1(t, carry):
            hprev, pprev = carry
            rows = pl.ds(pl.multiple_of(t * sub, sub), sub)
            a = a_ref[rows, :]
            hh = a * hprev + b_ref[rows, :]
            pp = a * pprev
            b_ref[rows, :] = hh
            a_ref[rows, :] = pp
            return hh, pp

        zero = jnp.zeros((sub, hd), F32)
        hfin, pfin = lax.fori_loop(0, sub_len, pass1, (zero, zero + 1.0), unroll=4)
        cin = carry_ref[:, cols]
        ptot, hloc = _sublane_scan(pfin, hfin, rowh)
        hend = hloc + ptot * cin
        carry_ref[:, cols] = _bcast_last(hend)
        entry = jnp.where(rowh == 0, cin, pltpu.roll(hend, 1, 0))

        def pass2(t, _):
            rows = pl.ds(pl.multiple_of(t * sub, sub), sub)
            hh = b_ref[rows, :] + a_ref[rows, :] * entry
            for q in range(spl):
                sl = h * spl + q
                yp_ref[sl, rows, :] = _gelu(gp_ref[sl, rows, :]) * hh[:, q * V7X_LANES:(q + 1) * V7X_LANES]
            return 0

        lax.fori_loop(0, sub_len, pass2, 0, unroll=4)

    _deinterleave_rows(yp_ref, o_ref, nslab, sub, sub_len)


def lru_mixer(z, conv_w, conv_b, wa, ba, wx, bx, lam, bsz, seq):
    t = z.shape[0]
    heads, hd, _ = wa.shape
    w = heads * hd
    kconv = conv_w.shape[0]
    tc = _tile(seq, 256)
    ns = seq // tc
    nslab = w // V7X_LANES
    halo = (kconv - 1) * V7X_SUBLANES
    wax = jnp.concatenate([wa, wx], axis=-1).astype(BF16)
    bax = jnp.concatenate([ba.reshape(heads, 1, hd), bx.reshape(heads, 1, hd)], axis=-1)
    return pl.pallas_call(
        functools.partial(_lru_kernel, heads=heads, hd=hd, tc=tc, kconv=kconv),
        grid=(bsz, ns),
        in_specs=[pl.BlockSpec((tc, w), lambda b, s: (b * ns + s, 0)),
                  pl.BlockSpec((tc, w), lambda b, s: (b * ns + s, 1)),
                  pl.BlockSpec((kconv, w), lambda b, s: (0, 0)),
                  pl.BlockSpec((1, w), lambda b, s: (0, 0)),
                  pl.BlockSpec((heads, hd, 2 * hd), lambda b, s: (0, 0, 0)),
                  pl.BlockSpec((heads, 1, 2 * hd), lambda b, s: (0, 0, 0)),
                  pl.BlockSpec((1, w), lambda b, s: (0, 0))],
        out_specs=pl.BlockSpec((tc, w), lambda b, s: (b * ns + s, 0)),
        out_shape=jax.ShapeDtypeStruct((t, w), BF16),
        scratch_shapes=[pltpu.VMEM((nslab, halo + tc, V7X_LANES), F32),
                        pltpu.VMEM((nslab, tc, V7X_LANES), F32),
                        pltpu.VMEM((nslab, tc, V7X_LANES), F32),
                        pltpu.VMEM((nslab, halo, V7X_LANES), F32),
                        pltpu.VMEM((tc, hd), F32),
                        pltpu.VMEM((tc, hd), F32),
                        pltpu.VMEM((V7X_SUBLANES, w), F32)],
        compiler_params=_cparams(("arbitrary", "arbitrary")),
        name="lru_mixer",
    )(z, z, conv_w, conv_b.reshape(1, w), wax, bax, lam.reshape(1, w))


S5_GROUPS_PER_BLOCK = 8


def _cmul(ar, ai, br, bi):
    return ar * br - ai * bi, ar * bi + ai * br


def _s5_kernel(u_ref, lre_ref, lim_ref, ldt_ref, bre_ref, bim_ref, cre_ref, cim_ref, d_ref, gw_ref, gb_ref,
               o_ref, wb_ref, a_ref, am_ref, p_ref, carry_ref, up_ref, re_ref, im_ref, y_ref, yp_ref,
               *, tc, nblk, cin, nst, lane_chunk):
    s = pl.program_id(1)
    gn = nblk * nst
    sub = V7X_SUBLANES
    sub_len = tc // sub
    nslab = nblk * cin // V7X_LANES

    @pl.when(s == 0)
    def _():
        lr = lre_ref[...]
        li = lim_ref[...]
        dt = jnp.exp(ldt_ref[...])
        mag = jnp.exp(lr * dt)
        ar = mag * jnp.cos(li * dt)
        ai = mag * jnp.sin(li * dt)
        den = lr * lr + li * li
        zr = ar - 1.0
        cr = (zr * lr + ai * li) / den
        ci = (ai * lr - zr * li) / den
        for j in range(nblk):
            cols = slice(j * nst, (j + 1) * nst)
            br = bre_ref[j]
            bi = bim_ref[j]
            wb_ref[j, :, 0:nst] = (cr[:, cols] * br - ci[:, cols] * bi).astype(BF16)
            wb_ref[j, :, nst:2 * nst] = (cr[:, cols] * bi + ci[:, cols] * br).astype(BF16)
        a_ref[0] = jnp.broadcast_to(ar, (sub, gn))
        a_ref[1] = jnp.broadcast_to(ai, (sub, gn))
        qr, qi = ar, ai
        for _ in range(sub_len - 1):
            qr, qi = _cmul(qr, qi, ar, ai)
        row = lax.broadcasted_iota(jnp.int32, (sub, gn), 0)
        pr, pi = qr, qi
        accr = jnp.zeros((sub, gn), F32)
        acci = jnp.zeros((sub, gn), F32)
        powers = {}
        for r in range(sub):
            powers[r + 1] = (pr, pi)
            accr = jnp.where(row == r, pr, accr)
            acci = jnp.where(row == r, pi, acci)
            pr, pi = _cmul(pr, pi, qr, qi)
        p_ref[0] = accr
        p_ref[1] = acci
        for idx, dd in enumerate((1, 2, 4)):
            wr, wi = powers[dd]
            am_ref[2 * idx] = jnp.where(row >= dd, wr, 0.0)
            am_ref[2 * idx + 1] = jnp.where(row >= dd, wi, 0.0)
        carry_ref[...] = jnp.zeros_like(carry_ref)

    _interleave_rows(u_ref, up_ref, nslab, sub, sub_len)
    for j in range(nblk):
        bu = jnp.dot(up_ref[j].astype(BF16), wb_ref[j], preferred_element_type=F32)
        re_ref[:, j * nst:(j + 1) * nst] = bu[:, :nst]
        im_ref[:, j * nst:(j + 1) * nst] = bu[:, nst:]

    row_c = lax.broadcasted_iota(jnp.int32, (sub, lane_chunk), 0)
    for c in range(gn // lane_chunk):
        lsl = slice(c * lane_chunk, (c + 1) * lane_chunk)

        def pass1(t, carry, lsl=lsl):
            pr, pi = carry
            rows = pl.ds(pl.multiple_of(t * sub, sub), sub)
            ar = a_ref[0, :, lsl]
            ai = a_ref[1, :, lsl]
            hr = re_ref[rows, lsl] + (ar * pr - ai * pi)
            hi = im_ref[rows, lsl] + (ar * pi + ai * pr)
            re_ref[rows, lsl] = hr
            im_ref[rows, lsl] = hi
            return hr, hi

        zero = jnp.zeros((sub, lane_chunk), F32)
        fr, fi = lax.fori_loop(0, sub_len, pass1, (zero, zero), unroll=2)
        for idx, dd in enumerate((1, 2, 4)):
            mr = am_ref[2 * idx, :, lsl]
            mi = am_ref[2 * idx + 1, :, lsl]
            sr = pltpu.roll(fr, dd, 0)
            si = pltpu.roll(fi, dd, 0)
            fr, fi = fr + (mr * sr - mi * si), fi + (mr * si + mi * sr)
        cr_ = carry_ref[0, :, lsl]
        ci_ = carry_ref[1, :, lsl]
        pr_ = p_ref[0, :, lsl]
        pi_ = p_ref[1, :, lsl]
        er = fr + (pr_ * cr_ - pi_ * ci_)
        ei = fi + (pr_ * ci_ + pi_ * cr_)
        carry_ref[0, :, lsl] = _bcast_last(er)
        carry_ref[1, :, lsl] = _bcast_last(ei)
        sr0 = jnp.where(row_c == 0, cr_, pltpu.roll(er, 1, 0))
        si0 = jnp.where(row_c == 0, ci_, pltpu.roll(ei, 1, 0))

        def pass2(t, carry, lsl=lsl):
            qr, qi = carry
            rows = pl.ds(pl.multiple_of(t * sub, sub), sub)
            ar = a_ref[0, :, lsl]
            ai = a_ref[1, :, lsl]
            qr, qi = ar * qr - ai * qi, ar * qi + ai * qr
            re_ref[rows, lsl] = re_ref[rows, lsl] + qr
            im_ref[rows, lsl] = im_ref[rows, lsl] + qi
            return qr, qi

        lax.fori_loop(0, sub_len, pass2, (sr0, si0), unroll=2)

    for j in range(nblk):
        cols = slice(j * nst, (j + 1) * nst)
        yj = (jnp.dot(re_ref[:, cols].astype(BF16), cre_ref[j], preferred_element_type=F32)
              - jnp.dot(im_ref[:, cols].astype(BF16), cim_ref[j], preferred_element_type=F32))
        ucols = slice(j * cin, (j + 1) * cin)
        y_ref[:, ucols] = yj + d_ref[:, ucols] * up_ref[j]
    g = _gelu(y_ref[...])
    gate = _sigmoid(jnp.dot(g.astype(BF16), gw_ref[...], preferred_element_type=F32) + gb_ref[...])
    fin = g * gate
    for sl in range(nslab):
        yp_ref[sl] = fin[:, sl * V7X_LANES:(sl + 1) * V7X_LANES]
    _deinterleave_rows(yp_ref, o_ref, nslab, sub, sub_len)


def s5_mixer(u, lam_re, lam_im, log_dt, b_re, b_im, c_re, c_im, d, glu_w, glu_b, bsz, seq):
    t, ws = u.shape
    groups, nstate, gch = b_re.shape
    gpb = S5_GROUPS_PER_BLOCK
    nblk = groups // gpb
    cin = gpb * gch
    assert cin == V7X_LANES
    nst = gpb * nstate
    gn = groups * nstate
    tc = _tile(seq, 256)
    ns = seq // tc
    lane_chunk = _tile(gn, 1024)
    nslab = ws // V7X_LANES
    eye = jnp.eye(gpb, dtype=F32)

    def bdiag_in(b):
        bb = b.reshape(nblk, gpb, nstate, gch).transpose(0, 1, 3, 2)
        return (bb[:, :, :, None, :] * eye[None, :, None, :, None]).reshape(nblk, cin, nst)

    def bdiag_out(c):
        cc = c.reshape(nblk, gpb, gch, nstate).transpose(0, 1, 3, 2)
        return (cc[:, :, :, None, :] * eye[None, :, None, :, None]).reshape(nblk, nst, cin)

    const2 = lambda b, s: (0, 0)
    const3 = lambda b, s: (0, 0, 0)
    return pl.pallas_call(
        functools.partial(_s5_kernel, tc=tc, nblk=nblk, cin=cin, nst=nst, lane_chunk=lane_chunk),
        grid=(bsz, ns),
        in_specs=[pl.BlockSpec((tc, ws), lambda b, s: (b * ns + s, 0)),
                  pl.BlockSpec((1, gn), const2), pl.BlockSpec((1, gn), const2), pl.BlockSpec((1, gn), const2),
                  pl.BlockSpec((nblk, cin, nst), const3), pl.BlockSpec((nblk, cin, nst), const3),
                  pl.BlockSpec((nblk, nst, cin), const3), pl.BlockSpec((nblk, nst, cin), const3),
                  pl.BlockSpec((1, ws), const2),
                  pl.BlockSpec((ws, ws), const2),
                  pl.BlockSpec((1, ws), const2)],
        out_specs=pl.BlockSpec((tc, ws), lambda b, s: (b * ns + s, 0)),
        out_shape=jax.ShapeDtypeStruct((t, ws), BF16),
        scratch_shapes=[pltpu.VMEM((nblk, cin, 2 * nst), BF16),
                        pltpu.VMEM((2, V7X_SUBLANES, gn), F32),
                        pltpu.VMEM((6, V7X_SUBLANES, gn), F32),
                        pltpu.VMEM((2, V7X_SUBLANES, gn), F32),
                        pltpu.VMEM((2, V7X_SUBLANES, gn), F32),
                        pltpu.VMEM((nslab, tc, V7X_LANES), F32),
                        pltpu.VMEM((tc, gn), F32),
                        pltpu.VMEM((tc, gn), F32),
                        pltpu.VMEM((tc, ws), F32),
                        pltpu.VMEM((nslab, tc, V7X_LANES), F32)],
        compiler_params=_cparams(("arbitrary", "arbitrary")),
        name="s5_mixer",
    )(u, lam_re.reshape(1, gn), lam_im.reshape(1, gn),
      jnp.broadcast_to(log_dt[:, None], (groups, nstate)).reshape(1, gn),
      bdiag_in(b_re), bdiag_in(b_im), bdiag_out(c_re).astype(BF16), bdiag_out(c_im).astype(BF16),
      d.reshape(1, ws), glu_w.astype(BF16), glu_b.reshape(1, ws))


def _sgu_kernel(u_ref, v_ref, g_ref, b_ref, w_ref, bs_ref, o_ref, *, heads, hd, chunk, nchunk):
    v = _gelu(v_ref[...].astype(F32))
    mu = jnp.mean(v, axis=-1, keepdims=True)
    vc = v - mu
    var = jnp.mean(vc * vc, axis=-1, keepdims=True)
    vn = (vc * lax.rsqrt(var + EPS) * g_ref[...] + b_ref[...]).astype(BF16)
    r = lax.broadcasted_iota(jnp.int32, (chunk, chunk), 0)
    c = lax.broadcasted_iota(jnp.int32, (chunk, chunk), 1)
    tril = r >= c
    for h in range(heads):
        wh = jnp.where(tril, w_ref[h], 0.0).astype(BF16)
        cols = slice(h * hd, (h + 1) * hd)
        for n in range(nchunk):
            rows = slice(n * chunk, (n + 1) * chunk)
            gsp = jnp.dot(wh, vn[rows, cols], preferred_element_type=F32) + bs_ref[h]
            o_ref[rows, cols] = (_gelu(u_ref[rows, cols].astype(F32)) * gsp).astype(o_ref.dtype)


def sgu_mixer(z, ln_g, ln_b, w_s, b_s):
    t = z.shape[0]
    heads, chunk, _ = w_s.shape
    w = ln_g.shape[0]
    hd = w // heads
    nchunk = 2 if (t // chunk) % 2 == 0 else 1
    tm = nchunk * chunk
    bs = jnp.broadcast_to(b_s[:, :, None], (heads, chunk, hd))
    return pl.pallas_call(
        functools.partial(_sgu_kernel, heads=heads, hd=hd, chunk=chunk, nchunk=nchunk),
        grid=(t // tm,),
        in_specs=[pl.BlockSpec((tm, w), lambda i: (i, 0)),
                  pl.BlockSpec((tm, w), lambda i: (i, 1)),
                  pl.BlockSpec((1, w), lambda i: (0, 0)),
                  pl.BlockSpec((1, w), lambda i: (0, 0)),
                  pl.BlockSpec((heads, chunk, chunk), lambda i: (0, 0, 0)),
                  pl.BlockSpec((heads, chunk, hd), lambda i: (0, 0, 0))],
        out_specs=pl.BlockSpec((tm, w), lambda i: (i, 0)),
        out_shape=jax.ShapeDtypeStruct((t, w), BF16),
        compiler_params=_cparams(("arbitrary",)),
        name="sgu_mixer",
    )(z, z, ln_g.reshape(1, w), ln_b.reshape(1, w), w_s, bs)


def _qk_norm_kernel(x_ref, g_ref, o_ref, *, nseg, seg):
    x = x_ref[...].astype(F32)
    for i in range(nseg):
        cols = slice(i * seg, (i + 1) * seg)
        xs = x[:, cols]
        ms = jnp.mean(xs * xs, axis=-1, keepdims=True)
        o_ref[:, cols] = (xs * lax.rsqrt(ms + EPS) * g_ref[:, cols]).astype(o_ref.dtype)


def qk_norm(z, col_block, width, gain_row, seg):
    t = z.shape[0]
    tm = _tile(t, 512)
    return pl.pallas_call(
        functools.partial(_qk_norm_kernel, nseg=width // seg, seg=seg),
        grid=(t // tm,),
        in_specs=[pl.BlockSpec((tm, width), lambda i: (i, col_block)),
                  pl.BlockSpec((1, width), lambda i: (0, 0))],
        out_specs=pl.BlockSpec((tm, width), lambda i: (i, 0)),
        out_shape=jax.ShapeDtypeStruct((t, width), BF16),
        compiler_params=_cparams(("arbitrary",)),
        name="qk_norm",
    )(z, gain_row)


ATTN_ROW_STRIP = 128


def _attn_kernel(q_ref, k_ref, v_ref, bias_ref, lq1_ref, lk1_ref, lq2_ref, lk2_ref, sg_ref, o_ref,
                 acc0_ref, acc1_ref, m0_ref, m1_ref, l0_ref, l1_ref, *, tq, dqk, lam_init):
    qi = pl.program_id(2)
    accs, ms, ls = (acc0_ref, acc1_ref), (m0_ref, m1_ref), (l0_ref, l1_ref)
    for mp in range(2):
        ms[mp][...] = jnp.full_like(ms[mp], NEG_INF)
        ls[mp][...] = jnp.zeros_like(ls[mp])
        accs[mp][...] = jnp.zeros_like(accs[mp])
    r = lax.broadcasted_iota(jnp.int32, (tq, tq), 0)
    c = lax.broadcasted_iota(jnp.int32, (tq, tq), 1)
    causal = r >= c
    krep = tq // V7X_LANES
    vrep = accs[0].shape[1] // V7X_LANES

    def block(kb, bias_idx, masked):
        k0 = pl.multiple_of(kb * tq, tq)
        v = v_ref[pl.ds(k0, tq), :]
        for r0 in range(0, tq, ATTN_ROW_STRIP):
            rs = slice(r0, r0 + ATTN_ROW_STRIP)
            use_bias = bias_idx is not None and (masked or r0 < REL_MAX_DIST)
            for mp in range(2):
                q = q_ref[rs, mp * dqk:(mp + 1) * dqk]
                k = k_ref[pl.ds(k0, tq), mp * dqk:(mp + 1) * dqk]
                sc = lax.dot_general(q, k, (((1,), (1,)), ((), ())), preferred_element_type=F32)
                if use_bias:
                    sc = sc + bias_ref[bias_idx, rs, :]
                if masked:
                    sc = jnp.where(causal[rs, :], sc, NEG_INF)
                m_old = ms[mp][rs, :]
                m_new = jnp.maximum(m_old, jnp.max(sc, axis=-1, keepdims=True))
                alpha = jnp.exp2(m_old - m_new)
                p = jnp.exp2(sc - jnp.tile(m_new, (1, krep)))
                ls[mp][rs, :] = alpha * ls[mp][rs, :] + jnp.sum(p, axis=-1, keepdims=True)
                accs[mp][rs, :] = (jnp.tile(alpha, (1, vrep)) * accs[mp][rs, :]
                                   + jnp.dot(p.astype(BF16), v, preferred_element_type=F32))
                ms[mp][rs, :] = m_new

    nfar = jnp.maximum(qi - 1, 0)

    def far_pair(j, _):
        block(2 * j, None, False)
        block(2 * j + 1, None, False)
        return 0

    lax.fori_loop(0, nfar // 2, far_pair, 0)

    @pl.when(nfar % 2 == 1)
    def _():
        block(nfar - 1, None, False)

    @pl.when(qi > 0)
    def _():
        block(qi - 1, 1, False)
        block(qi, 0, True)

    @pl.when(qi == 0)
    def _():
        block(qi, 0, True)

    lam = (jnp.exp(jnp.sum(lq1_ref[...] * lk1_ref[...], axis=-1, keepdims=True))
           - jnp.exp(jnp.sum(lq2_ref[...] * lk2_ref[...], axis=-1, keepdims=True)) + lam_init)
    o = (accs[0][...] * jnp.tile(1.0 / ls[0][...], (1, vrep))
         - lam * (accs[1][...] * jnp.tile(1.0 / ls[1][...], (1, vrep))))
    ms_o = jnp.mean(o * o, axis=-1, keepdims=True)
    o_ref[...] = (o * lax.rsqrt(ms_o + EPS) * sg_ref[...] * (1.0 - lam_init)).astype(o_ref.dtype)


def _t5_bucket(n, buckets):
    max_exact = buckets // 2
    nf = jnp.maximum(n, 1).astype(F32)
    large = max_exact + (jnp.log(nf / max_exact) / math.log(REL_MAX_DIST / max_exact)
                         * (buckets - max_exact)).astype(jnp.int32)
    large = jnp.minimum(large, buckets - 1)
    return jnp.where(n < max_exact, n, large)


def _toeplitz(w, rows, cols):
    nh, lw = w.shape
    flat = jnp.tile(w, (1, rows))[:, :rows * (lw - 1)]
    return flat.reshape(nh, rows, lw - 1)[:, :, :cols]


def diff_attention(qn, kn, zqkv, v_col_block, rel_bias, lq1, lk1, lq2, lk2, sub_g, bsz, seq, layer):
    t = qn.shape[0]
    buckets, heads = rel_bias.shape
    dv = sub_g.shape[0]
    dqk = dv // 2
    tq = _tile(seq, 512)
    assert tq >= REL_MAX_DIST, "far key blocks must all fall in the last relative-position bucket"
    nq = seq // tq
    lam_init = 0.8 - 0.6 * math.exp(-0.3 * layer)
    table = jnp.transpose(rel_bias.astype(F32))
    fvals = table[:, _t5_bucket(jnp.arange(2 * tq + 1, dtype=jnp.int32), buckets)]
    fvals = (fvals - fvals[:, 2 * tq:]) * LOG2E
    jj = np.arange(2 * tq)
    n_diag = np.where(jj <= tq, 0, 2 * tq - jj)
    n_near = np.where(jj < tq, tq - jj, 3 * tq - jj)
    bias = jnp.stack([_toeplitz(fvals[:, n_diag], tq, tq), _toeplitz(fvals[:, n_near], tq, tq)], axis=1)
    row = lambda b, h, i: (0, 0)
    return pl.pallas_call(
        functools.partial(_attn_kernel, tq=tq, dqk=dqk, lam_init=lam_init),
        grid=(bsz, heads, nq),
        in_specs=[pl.BlockSpec((tq, dv), lambda b, h, i: (b * nq + i, h)),
                  pl.BlockSpec((seq, dv), lambda b, h, i: (b, h)),
                  pl.BlockSpec((seq, dv), lambda b, h, i: (b, v_col_block + h)),
                  pl.BlockSpec((None, 2, tq, tq), lambda b, h, i: (h, 0, 0, 0)),
                  pl.BlockSpec((1, dqk), row), pl.BlockSpec((1, dqk), row),
                  pl.BlockSpec((1, dqk), row), pl.BlockSpec((1, dqk), row),
                  pl.BlockSpec((1, dv), row)],
        out_specs=pl.BlockSpec((tq, dv), lambda b, h, i: (b * nq + i, h)),
        out_shape=jax.ShapeDtypeStruct((t, heads * dv), BF16),
        scratch_shapes=[pltpu.VMEM((tq, dv), F32), pltpu.VMEM((tq, dv), F32),
                        pltpu.VMEM((tq, V7X_LANES), F32), pltpu.VMEM((tq, V7X_LANES), F32),
                        pltpu.VMEM((tq, V7X_LANES), F32), pltpu.VMEM((tq, V7X_LANES), F32)],
        compiler_params=_cparams(("arbitrary", "arbitrary", "arbitrary")),
        name="diff_attention",
    )(qn, kn, zqkv, bias, lq1.reshape(1, dqk), lk1.reshape(1, dqk), lq2.reshape(1, dqk), lk2.reshape(1, dqk),
      sub_g.reshape(1, dv))


ROUTER_LANES = 128
HI16 = 0xFFFF0000


def _norm_mod(x, g_ref, sc_ref, sh_ref):
    ms = jnp.mean(x * x, axis=-1, keepdims=True)
    return (x * lax.rsqrt(ms + EPS) * g_ref[...]) * (1.0 + sc_ref[...]) + sh_ref[...]


def _router_kernel(x_ref, g_ref, sc_ref, sh_ref, wr_ref, wlo_ref, br_ref, route_ref, hpk_ref, *, ngroups, per_group):
    h = _norm_mod(x_ref[...], g_ref, sc_ref, sh_ref)
    half = h.shape[1] // 2
    bits = pltpu.bitcast(h.astype(BF16).astype(F32), jnp.uint32)
    hpk_ref[...] = (bits[:, half:] & jnp.uint32(HI16)) | (bits[:, :half] >> 16)
    hi = h.astype(BF16)
    lo = (h - hi.astype(F32)).astype(BF16)
    both = jnp.dot(hi, wr_ref[...], preferred_element_type=F32)
    nl = ROUTER_LANES
    logits = both[:, :nl] + both[:, nl:] + jnp.dot(lo, wlo_ref[...], preferred_element_type=F32) + br_ref[...]
    lane = lax.broadcasted_iota(jnp.int32, logits.shape, 1).astype(F32)
    big = float(nl)
    glog = jnp.where(lane < ngroups, logits, NEG_INF)
    gmax = jnp.max(glog, axis=-1, keepdims=True)
    gsum = jnp.sum(jnp.exp(glog - gmax), axis=-1, keepdims=True)
    gp = 1.0 / gsum
    gidx = jnp.min(jnp.where(glog == gmax, lane, big), axis=-1, keepdims=True)
    lo_lane = ngroups + gidx * per_group
    emask = (lane >= lo_lane) & (lane < lo_lane + per_group)
    elog = jnp.where(emask, logits, NEG_INF)
    emax = jnp.max(elog, axis=-1, keepdims=True)
    eexp = jnp.where(emask, jnp.exp(elog - emax), -1.0)
    i0 = jnp.min(jnp.where(eexp == 1.0, lane, big), axis=-1, keepdims=True)
    rest = jnp.where(lane == i0, -1.0, eexp)
    p1 = jnp.max(rest, axis=-1, keepdims=True)
    i1 = jnp.min(jnp.where(rest == p1, lane, big), axis=-1, keepdims=True)
    denom = 1.0 + p1
    w0 = gp * (1.0 / denom)
    w1 = gp * (p1 / denom)
    e0 = i0 - ngroups
    e1 = i1 - ngroups
    route_ref[...] = jnp.where(lane == 0, e0, jnp.where(lane == 1, e1, jnp.where(lane == 2, w0,
                               jnp.where(lane == 3, w1, 0.0))))


def moe_router(x, g, sc, sh, wg, bg, we, be, seq):
    t, d = x.shape
    bsz = sc.shape[0]
    ngroups = wg.shape[-1]
    per_group = we.shape[-1]
    nexp = ngroups * per_group
    nl = ROUTER_LANES
    assert ngroups + nexp <= nl
    wr = jnp.concatenate([wg, jnp.transpose(we, (1, 0, 2)).reshape(d, nexp)], axis=-1)
    wr = jnp.zeros((d, nl), F32).at[:, :ngroups + nexp].set(wr)
    w_hi = wr.astype(BF16)
    w_lo = (wr - w_hi.astype(F32)).astype(BF16)
    br = jnp.zeros((1, nl), F32).at[0, :ngroups + nexp].set(jnp.concatenate([bg, be.reshape(nexp)]))
    tm = _tile(seq, 256)
    per = seq // tm
    return pl.pallas_call(
        functools.partial(_router_kernel, ngroups=ngroups, per_group=per_group),
        grid=(t // tm,),
        in_specs=[pl.BlockSpec((tm, d), lambda i: (i, 0)),
                  pl.BlockSpec((1, d), lambda i: (0, 0)),
                  pl.BlockSpec((None, 1, d), lambda i: (i // per, 0, 0)),
                  pl.BlockSpec((None, 1, d), lambda i: (i // per, 0, 0)),
                  pl.BlockSpec((d, 2 * nl), lambda i: (0, 0)),
                  pl.BlockSpec((d, nl), lambda i: (0, 0)),
                  pl.BlockSpec((1, nl), lambda i: (0, 0))],
        out_specs=[pl.BlockSpec((tm, nl), lambda i: (i, 0)), pl.BlockSpec((tm, d // 2), lambda i: (i, 0))],
        out_shape=[jax.ShapeDtypeStruct((t, nl), F32), jax.ShapeDtypeStruct((t, d // 2), jnp.uint32)],
        compiler_params=_cparams(("arbitrary",)),
        name="moe_router",
    )(x, g.reshape(1, d), sc.reshape(bsz, 1, d), sh.reshape(bsz, 1, d),
      jnp.concatenate([w_hi, w_lo], axis=-1), w_hi, br)


def _dispatch_kernel(pos_ref, zf_ref, hpk_ref, xs_hbm, buf, zbuf, sem, zsem, *, tm, tme, ntok, ntile, topk):
    i = pl.program_id(0)
    nsteps = pl.num_programs(0)
    slot = lax.rem(i, 2)

    @pl.when(i == 0)
    def _():
        zbuf[...] = jnp.zeros_like(zbuf)

        def zcopy(tl):
            return pltpu.make_async_copy(zbuf, xs_hbm.at[pl.ds(tl * tme, tme)], zsem)

        def zstart(tl, _):
            @pl.when(zf_ref[tl] > 0)
            def _():
                zcopy(tl).start()
            return 0

        def zwait(tl, _):
            @pl.when(zf_ref[tl] > 0)
            def _():
                zcopy(tl).wait()
            return 0

        lax.fori_loop(0, ntile, zstart, 0)
        lax.fori_loop(0, ntile, zwait, 0)

    buf[slot] = hpk_ref[...]
    base = i * tm

    def issue(r, _):
        for kk in range(topk):
            pltpu.make_async_copy(buf.at[slot, pl.ds(r, 1)],
                                  xs_hbm.at[pl.ds(pos_ref[kk * ntok + base + r], 1)], sem.at[slot]).start()
        return 0

    lax.fori_loop(0, tm, issue, 0, unroll=4)

    def drain(s):
        for _ in range(topk):
            pltpu.make_async_copy(buf.at[s], xs_hbm.at[pl.ds(0, tm)], sem.at[s]).wait()

    @pl.when(i > 0)
    def _():
        drain(1 - slot)

    @pl.when(i == nsteps - 1)
    def _():
        drain(slot)


W_CHUNKS = 4
W_STAGES = 4


def _expert_kernel(texp_ref, tvalid_ref, xidx_ref, slot_ref, first_ref, nxt_ref, clo_ref, chi_ref,
                   xs_ref, wg_hbm, wu_hbm, wd_hbm, y_ref, wg_buf, wu_buf, wd_buf, st_a, st_d, sem,
                   *, half, layer, d, f):
    i = pl.program_id(0)
    nch = 3 * W_CHUNKS
    ra = d // W_CHUNKS
    rd = f // W_CHUNKS

    def chunk_dma(e, which, part, b):
        if which == 2:
            return pltpu.make_async_copy(wd_hbm.at[layer, e, pl.ds(pl.multiple_of(part * rd, rd), rd), :],
                                         st_d.at[b], sem.at[b])
        src = wg_hbm if which == 0 else wu_hbm
        return pltpu.make_async_copy(src.at[layer, e, pl.ds(pl.multiple_of(part * ra, ra), ra), :],
                                     st_a.at[b], sem.at[b])

    def for_chunk(c, fn):
        for which in range(3):
            @pl.when((c >= which * W_CHUNKS) & (c < (which + 1) * W_CHUNKS))
            def _(which=which):
                fn(which, c - which * W_CHUNKS)

    def start(e, c):
        b = lax.rem(c, W_STAGES)
        for_chunk(c, lambda which, part: chunk_dma(e, which, part, b).start())

    def finish(e, c, s):
        b = lax.rem(c, W_STAGES)

        def fn(which, part):
            chunk_dma(e, which, part, b).wait()
            if which == 2:
                wd_buf[s, pl.ds(pl.multiple_of(part * rd, rd), rd), :] = st_d[b].astype(BF16)
            else:
                dst = wg_buf if which == 0 else wu_buf
                dst[s, pl.ds(pl.multiple_of(part * ra, ra), ra), :] = st_a[b].astype(BF16)

        for_chunk(c, fn)

    slot = slot_ref[i]
    nxt = nxt_ref[i]

    @pl.when(i == 0)
    def _():
        def load_first(c, _):
            start(texp_ref[0], c)
            finish(texp_ref[0], c, 0)
            return 0
        lax.fori_loop(0, nch, load_first, 0)

    @pl.when((nxt >= 0) & (first_ref[i] > 0))
    def _():
        for c in range(W_STAGES):
            start(nxt, jnp.int32(c))

    @pl.when(tvalid_ref[i] > 0)
    def _():
        xp = xs_ref[...]
        x_lo = pltpu.bitcast(xp << 16, F32).astype(BF16)
        x_hi = pltpu.bitcast(xp & jnp.uint32(HI16), F32).astype(BF16)
        hg = (jnp.dot(x_lo, wg_buf[slot, 0:half, :], preferred_element_type=F32)
              + jnp.dot(x_hi, wg_buf[slot, half:, :], preferred_element_type=F32))
        hu = (jnp.dot(x_lo, wu_buf[slot, 0:half, :], preferred_element_type=F32)
              + jnp.dot(x_hi, wu_buf[slot, half:, :], preferred_element_type=F32))
        act = (hg * _sigmoid(hg)) * hu
        yv = jnp.dot(act.astype(BF16), wd_buf[slot], preferred_element_type=F32)
        bits = pltpu.bitcast(yv.astype(BF16).astype(F32), jnp.uint32)
        y_ref[...] = (bits[:, half:] & jnp.uint32(HI16)) | (bits[:, :half] >> 16)

    @pl.when(tvalid_ref[i] == 0)
    def _():
        y_ref[...] = jnp.zeros_like(y_ref)

    @pl.when(nxt >= 0)
    def _():
        def advance(c, _):
            finish(nxt, c, 1 - slot)

            @pl.when(c + W_STAGES < nch)
            def _():
                start(nxt, c + W_STAGES)
            return 0
        lax.fori_loop(clo_ref[i], chi_ref[i], advance, 0)


def _combine_kernel(pos_ref, x_ref, route_ref, g_ref, y_hbm, *rest, tm, ntok, topk, half, with_norm):
    if with_norm:
        ng_ref, nsc_ref, nsh_ref, o_ref, hn_ref, ybuf, sem = rest
    else:
        o_ref, ybuf, sem = rest
    i = pl.program_id(0)
    nsteps = pl.num_programs(0)
    slot = lax.rem(i, 2)

    def gather(step, s):
        base = step * tm

        def issue(r, _):
            for kk in range(topk):
                pltpu.make_async_copy(y_hbm.at[pl.ds(pos_ref[kk * ntok + base + r], 1)],
                                      ybuf.at[s, kk, pl.ds(r, 1)], sem.at[s]).start()
            return 0

        lax.fori_loop(0, tm, issue, 0, unroll=4)

    @pl.when(i == 0)
    def _():
        gather(0, 0)

    @pl.when(i + 1 < nsteps)
    def _():
        gather(i + 1, 1 - slot)

    for kk in range(topk):
        pltpu.make_async_copy(y_hbm.at[pl.ds(0, tm)], ybuf.at[slot, kk], sem.at[slot]).wait()

    def mix_rows(cidx, _):
        rows = pl.ds(pl.multiple_of(cidx * V7X_SUBLANES, V7X_SUBLANES), V7X_SUBLANES)
        w = route_ref[rows, :]
        ylo = yhi = None
        for kk in range(topk):
            yp = ybuf[slot, kk, rows, :]
            wk = w[:, topk + kk:topk + kk + 1]
            lo = wk * pltpu.bitcast(yp << 16, F32)
            hi = wk * pltpu.bitcast(yp & jnp.uint32(HI16), F32)
            ylo = lo if ylo is None else ylo + lo
            yhi = hi if yhi is None else yhi + hi
        olo = x_ref[rows, 0:half] + g_ref[:, 0:half] * ylo
        ohi = x_ref[rows, half:] + g_ref[:, half:] * yhi
        o_ref[rows, 0:half] = olo
        o_ref[rows, half:] = ohi
        if with_norm:
            ms = (jnp.sum(olo * olo, axis=-1, keepdims=True) + jnp.sum(ohi * ohi, axis=-1, keepdims=True)) / (2 * half)
            rs = lax.rsqrt(ms + EPS)
            for cols, ov in ((slice(0, half), olo), (slice(half, 2 * half), ohi)):
                hn = (ov * rs * ng_ref[:, cols]) * (1.0 + nsc_ref[:, cols]) + nsh_ref[:, cols]
                hn_ref[rows, cols] = hn.astype(hn_ref.dtype)
        return 0

    lax.fori_loop(0, tm // V7X_SUBLANES, mix_rows, 0, unroll=4)


def moe_layer(x, route, hpk, gate, w_gate, w_up, w_down, layer, seq, next_norm=None):
    t, d = x.shape
    bsz = gate.shape[0]
    _, nexp, _, f = w_gate.shape
    topk = MOE_TOPK
    half = d // 2
    npair = t * topk
    tme = _tile(npair // nexp, 256) if npair // nexp >= 8 else 8
    ntile = npair // tme + nexp
    tm = _tile(seq, 256)
    per = seq // tm

    flat_e = jnp.transpose(route[:, 0:topk]).astype(jnp.int32).reshape(npair)
    onehot = (flat_e[:, None] == jnp.arange(nexp, dtype=jnp.int32)[None, :]).astype(jnp.int32)
    csum = jnp.cumsum(onehot, axis=0)
    counts = csum[-1]
    tiles_per = (counts + tme - 1) // tme
    tile_end = jnp.cumsum(tiles_per)
    tile_start = tile_end - tiles_per
    pos = jnp.sum(onehot * (tile_start[None, :] * tme + csum - 1), axis=1).astype(jnp.int32)
    n_used = tile_end[-1]
    tile_id = jnp.arange(ntile, dtype=jnp.int32)
    active = tile_id < n_used
    owner = (tile_id[:, None] >= tile_start[None, :]) & (tile_id[:, None] < tile_end[None, :])
    eids = jnp.arange(nexp, dtype=jnp.int32)
    used = tiles_per > 0
    ordinal = jnp.cumsum(used.astype(jnp.int32)) - 1
    nxt_of = jnp.min(jnp.where((eids[None, :] > eids[:, None]) & used[None, :], eids[None, :], nexp), axis=1)
    nxt_of = jnp.where(nxt_of >= nexp, -1, nxt_of)

    def of_tile(per_expert):
        return jnp.sum(jnp.where(owner, per_expert[None, :], 0), axis=1)

    last_used = jnp.max(jnp.where(used, eids, 0))
    texp = jnp.where(active, of_tile(eids), last_used).astype(jnp.int32)
    kk = tile_id - of_tile(tile_start)
    tvalid = jnp.where(active, jnp.clip(of_tile(counts) - kk * tme, 0, tme), 0).astype(jnp.int32)
    xidx = jnp.minimum(tile_id, jnp.maximum(n_used - 1, 0)).astype(jnp.int32)
    zflag = (tvalid < tme).astype(jnp.int32)
    nchunk = 3 * W_CHUNKS
    nn = jnp.maximum(of_tile(tiles_per), 1)
    wslot = jnp.where(active, of_tile(ordinal) % 2, 0).astype(jnp.int32)
    wfirst = (active & (kk == 0)).astype(jnp.int32)
    wnxt = jnp.where(active, of_tile(nxt_of), -1).astype(jnp.int32)
    clo = jnp.where(active, kk * nchunk // nn, 0).astype(jnp.int32)
    chi = jnp.where(active, (kk + 1) * nchunk // nn, 0).astype(jnp.int32)

    xs = pl.pallas_call(
        functools.partial(_dispatch_kernel, tm=tm, tme=tme, ntok=t, ntile=ntile, topk=topk),
        grid_spec=pltpu.PrefetchScalarGridSpec(
            num_scalar_prefetch=2,
            grid=(t // tm,),
            in_specs=[pl.BlockSpec((tm, half), lambda i, *_: (i, 0))],
            out_specs=pl.BlockSpec(memory_space=pl.ANY),
            scratch_shapes=[pltpu.VMEM((2, tm, half), jnp.uint32), pltpu.VMEM((tme, half), jnp.uint32),
                            pltpu.SemaphoreType.DMA((2,)), pltpu.SemaphoreType.DMA(())]),
        out_shape=jax.ShapeDtypeStruct((ntile * tme, half), jnp.uint32),
        compiler_params=_cparams(("arbitrary",)),
        name="moe_dispatch",
    )(pos, zflag, hpk)

    y = pl.pallas_call(
        functools.partial(_expert_kernel, half=half, layer=layer, d=d, f=f),
        grid_spec=pltpu.PrefetchScalarGridSpec(
            num_scalar_prefetch=8,
            grid=(ntile,),
            in_specs=[pl.BlockSpec((tme, half), lambda i, te, tv, xi, *_: (xi[i], 0)),
                      pl.BlockSpec(memory_space=pl.ANY),
                      pl.BlockSpec(memory_space=pl.ANY),
                      pl.BlockSpec(memory_space=pl.ANY)],
            out_specs=pl.BlockSpec((tme, half), lambda i, *_: (i, 0)),
            scratch_shapes=[pltpu.VMEM((2, d, f), BF16), pltpu.VMEM((2, d, f), BF16), pltpu.VMEM((2, f, d), BF16),
                            pltpu.VMEM((W_STAGES, d // W_CHUNKS, f), F32),
                            pltpu.VMEM((W_STAGES, f // W_CHUNKS, d), F32),
                            pltpu.SemaphoreType.DMA((W_STAGES,))]),
        out_shape=jax.ShapeDtypeStruct((ntile * tme, half), jnp.uint32),
        compiler_params=_cparams(("arbitrary",)),
        name="moe_experts",
    )(texp, tvalid, xidx, wslot, wfirst, wnxt, clo, chi, xs, w_gate, w_up, w_down)

    row_spec = pl.BlockSpec((tm, d), lambda i, *_: (i, 0))
    batch_spec = pl.BlockSpec((None, 1, d), lambda i, *_: (i // per, 0, 0))
    in_specs = [row_spec, pl.BlockSpec((tm, ROUTER_LANES), lambda i, *_: (i, 0)), batch_spec,
                pl.BlockSpec(memory_space=pl.ANY)]
    args = [pos, x, route, gate.reshape(bsz, 1, d), y]
    out_specs, out_shape = row_spec, jax.ShapeDtypeStruct((t, d), F32)
    if next_norm is not None:
        ng, nsc, nsh = next_norm
        in_specs += [pl.BlockSpec((1, d), lambda i, *_: (0, 0)), batch_spec, batch_spec]
        args += [ng.reshape(1, d), nsc.reshape(bsz, 1, d), nsh.reshape(bsz, 1, d)]
        out_specs = [row_spec, row_spec]
        out_shape = [out_shape, jax.ShapeDtypeStruct((t, d), BF16)]
    return pl.pallas_call(
        functools.partial(_combine_kernel, tm=tm, ntok=t, topk=topk, half=half, with_norm=next_norm is not None),
        grid_spec=pltpu.PrefetchScalarGridSpec(
            num_scalar_prefetch=1,
            grid=(t // tm,),
            in_specs=in_specs,
            out_specs=out_specs,
            scratch_shapes=[pltpu.VMEM((2, topk, tm, half), jnp.uint32), pltpu.SemaphoreType.DMA((2,))]),
        out_shape=out_shape,
        compiler_params=_cparams(("arbitrary",)),
        name="moe_combine",
    )(*args)


def kernel(x, c, norm1_g, norm2_g, ada_w, ada_b, ab_w_in, ab_w_out, lru_conv_w, lru_conv_b, lru_wa, lru_ba, lru_wx, lru_bx, lru_lambda, s5_lambda_re, s5_lambda_im, s5_log_dt, s5_b_re, s5_b_im, s5_c_re, s5_c_im, s5_d, s5_glu_w, s5_glu_b, cd_w_in, cd_w_out, sg_ln_g, sg_ln_b, sg_w, sg_b, da_q_norm, da_k_norm, da_lq1, da_lk1, da_lq2, da_lk2, da_sub_g, rel_bias, moe_wg, moe_bg, moe_we, moe_be, moe_w_gate, moe_w_up, moe_w_down):
    bsz, seq, d = x.shape
    depth = norm1_g.shape[0]
    t = bsz * seq
    xt = x.reshape(t, d)
    mod = ada_modulation(c, ada_w, ada_b)

    hmix = None
    for layer in range(depth):
        sh1, sc1, g1, sh2, sc2, g2 = [mod[layer, :, i * d:(i + 1) * d] for i in range(6)]
        if hmix is None:
            hmix = norm_modulate(xt, norm1_g[layer], sc1, sh1, seq)
        j = layer // 2
        if layer % 2 == 0:
            lw = lru_conv_w.shape[-1]
            w_in = ab_w_in[j].astype(BF16)
            w_out = ab_w_out[j].astype(BF16)
            z_lru = matmul([hmix], w_in, BF16, col_off=0, ncols=2 * lw)
            z_s5 = matmul([hmix], w_in, BF16, col_off=2 * lw)
            y_a = lru_mixer(z_lru, lru_conv_w[j], lru_conv_b[j], lru_wa[j], lru_ba[j], lru_wx[j], lru_bx[j],
                            lru_lambda[j], bsz, seq)
            y_b = s5_mixer(z_s5, s5_lambda_re[j], s5_lambda_im[j], s5_log_dt[j], s5_b_re[j], s5_b_im[j],
                           s5_c_re[j], s5_c_im[j], s5_d[j], s5_glu_w[j], s5_glu_b[j], bsz, seq)
            xt = matmul([y_a, y_b], w_out, F32, res=xt, gate=g1, seq=seq)
        else:
            sgw = sg_ln_g.shape[-1]
            dqk = da_q_norm.shape[-1]
            dv = da_sub_g.shape[-1]
            heads = rel_bias.shape[1]
            daw = heads * dv
            w_in = cd_w_in[j].astype(BF16)
            w_out = cd_w_out[j].astype(BF16)
            z_sg = matmul([hmix], w_in, BF16, col_off=0, ncols=2 * sgw)
            z_qkv = matmul([hmix], w_in, BF16, col_off=2 * sgw)
            y_c = sgu_mixer(z_sg, sg_ln_g[j], sg_ln_b[j], sg_w[j], sg_b[j])
            q_gain = jnp.tile(da_q_norm[j] * (dqk ** -0.5 * LOG2E), daw // dqk).reshape(1, daw)
            k_gain = jnp.tile(da_k_norm[j], daw // dqk).reshape(1, daw)
            qn = qk_norm(z_qkv, 0, daw, q_gain, dqk)
            kn = qk_norm(z_qkv, 1, daw, k_gain, dqk)
            y_d = diff_attention(qn, kn, z_qkv, 2 * daw // dv, rel_bias, da_lq1[j], da_lk1[j], da_lq2[j], da_lk2[j],
                                 da_sub_g[j], bsz, seq, layer)
            xt = matmul([y_c, y_d], w_out, F32, res=xt, gate=g1, seq=seq)
        route, hpk = moe_router(xt, norm2_g[layer], sc2, sh2, moe_wg[layer], moe_bg[layer], moe_we[layer],
                                moe_be[layer], seq)
        if layer + 1 < depth:
            nxt = (norm1_g[layer + 1], mod[layer + 1, :, d:2 * d], mod[layer + 1, :, 0:d])
            xt, hmix = moe_layer(xt, route, hpk, g2, moe_w_gate, moe_w_up, moe_w_down, layer, seq, next_norm=nxt)
        else:
            xt = moe_layer(xt, route, hpk, g2, moe_w_gate, moe_w_up, moe_w_down, layer, seq)
    return xt.reshape(bsz, seq, d)
```

```python
import functools
import math

import jax
import jax.numpy as jnp
import numpy as np
from jax import lax
from jax.experimental import pallas as pl
from jax.experimental.pallas import tpu as pltpu

F32 = jnp.float32
BF16 = jnp.bfloat16

EPS = 1e-6
LRU_C = 8.0
REL_MAX_DIST = 128
MOE_TOPK = 2
NEG_INF = -1e30
LOG2E = math.log2(math.e)

V7X_LANES = 128
V7X_SUBLANES = 8
V7X_VMEM_LIMIT_BYTES = 56 * 1024 * 1024


def _cparams(semantics):
    return pltpu.CompilerParams(dimension_semantics=semantics, vmem_limit_bytes=V7X_VMEM_LIMIT_BYTES)


def _sigmoid(x):
    return 0.5 * jnp.tanh(0.5 * x) + 0.5


def _gelu(x):
    return 0.5 * x * (1.0 + jnp.tanh(math.sqrt(2.0 / math.pi) * (x + 0.044715 * (x * x * x))))


def _tile(n, want):
    t = min(n, want)
    while n % t:
        t -= 1
    return t


def _ada_kernel(c_ref, w_ref, b_ref, o_ref):
    c = c_ref[...]
    cond = c * _sigmoid(c)
    o_ref[...] = jnp.dot(cond.astype(BF16), w_ref[...].astype(BF16), preferred_element_type=F32) + b_ref[...]


def ada_modulation(c, ada_w, ada_b):
    bsz, d = c.shape
    depth, _, n = ada_w.shape
    rows = 16
    cp = jnp.zeros((rows, d), F32).at[:bsz].set(c)
    tn = _tile(n, 512)
    out = pl.pallas_call(
        _ada_kernel,
        grid=(depth, n // tn),
        in_specs=[pl.BlockSpec((rows, d), lambda l, j: (0, 0)),
                  pl.BlockSpec((None, d, tn), lambda l, j: (l, 0, j)),
                  pl.BlockSpec((None, 1, tn), lambda l, j: (l, 0, j))],
        out_specs=pl.BlockSpec((None, rows, tn), lambda l, j: (l, 0, j)),
        out_shape=jax.ShapeDtypeStruct((depth, rows, n), F32),
        compiler_params=_cparams(("arbitrary", "arbitrary")),
        name="ada_modulation",
    )(cp, ada_w, ada_b.reshape(depth, 1, n))
    return out[:, :bsz]


def _norm_mod_kernel(x_ref, g_ref, sc_ref, sh_ref, o_ref):
    x = x_ref[...]
    ms = jnp.mean(x * x, axis=-1, keepdims=True)
    y = x * lax.rsqrt(ms + EPS) * g_ref[...]
    o_ref[...] = (y * (1.0 + sc_ref[...]) + sh_ref[...]).astype(o_ref.dtype)


def norm_modulate(x, g, sc, sh, seq, out_dtype=BF16):
    t, d = x.shape
    bsz = sc.shape[0]
    tm = _tile(seq, 256)
    per = seq // tm
    return pl.pallas_call(
        _norm_mod_kernel,
        grid=(t // tm,),
        in_specs=[pl.BlockSpec((tm, d), lambda i: (i, 0)),
                  pl.BlockSpec((1, d), lambda i: (0, 0)),
                  pl.BlockSpec((None, 1, d), lambda i: (i // per, 0, 0)),
                  pl.BlockSpec((None, 1, d), lambda i: (i // per, 0, 0))],
        out_specs=pl.BlockSpec((tm, d), lambda i: (i, 0)),
        out_shape=jax.ShapeDtypeStruct((t, d), out_dtype),
        compiler_params=_cparams(("arbitrary",)),
        name="norm_modulate",
    )(x, g.reshape(1, d), sc.reshape(bsz, 1, d), sh.reshape(bsz, 1, d))


def _matmul_kernel(*refs, ksplit, has_res):
    n_lhs = len(ksplit)
    a_refs = refs[:n_lhs]
    w_ref = refs[n_lhs]
    o_ref = refs[-1]
    acc = None
    k0 = 0
    for a_ref, kk in zip(a_refs, ksplit):
        part = jnp.dot(a_ref[...], w_ref[k0:k0 + kk, :], preferred_element_type=F32)
        acc = part if acc is None else acc + part
        k0 += kk
    if has_res:
        res_ref, gate_ref = refs[n_lhs + 1], refs[n_lhs + 2]
        acc = res_ref[...] + gate_ref[...] * acc
    o_ref[...] = acc.astype(o_ref.dtype)


def matmul(lhs, w, out_dtype, col_off=0, ncols=None, res=None, gate=None, seq=None):
    m = lhs[0].shape[0]
    ktot = w.shape[0]
    ksplit = tuple(a.shape[1] for a in lhs)
    assert sum(ksplit) == ktot
    n = w.shape[1] - col_off if ncols is None else ncols
    tm = _tile(m if seq is None else seq, 1024)
    wide = res is None and jnp.dtype(out_dtype).itemsize == 2
    tn = _tile(math.gcd(n, col_off) if col_off else n, 1024 if wide else 512)
    joff = col_off // tn
    in_specs = [pl.BlockSpec((tm, kk), lambda i, j: (i, 0)) for kk in ksplit]
    in_specs += [pl.BlockSpec((ktot, tn), lambda i, j: (0, j + joff))]
    args = list(lhs) + [w]
    if res is not None:
        per = seq // tm
        bsz = gate.shape[0]
        in_specs += [pl.BlockSpec((tm, tn), lambda i, j: (i, j)),
                     pl.BlockSpec((None, 1, tn), lambda i, j: (i // per, 0, j))]
        args += [res, gate.reshape(bsz, 1, n)]
    return pl.pallas_call(
        functools.partial(_matmul_kernel, ksplit=ksplit, has_res=res is not None),
        grid=(m // tm, n // tn),
        in_specs=in_specs,
        out_specs=pl.BlockSpec((tm, tn), lambda i, j: (i, j)),
        out_shape=jax.ShapeDtypeStruct((m, n), out_dtype),
        compiler_params=_cparams(("arbitrary", "arbitrary")),
        name="matmul",
    )(*args)


def _interleave_rows(src_ref, dst_ref, nslab, nsub, sub_len):
    for sl in range(nslab):
        lanes = slice(sl * V7X_LANES, (sl + 1) * V7X_LANES)
        for r in range(nsub):
            dst_ref[sl, pl.ds(r, sub_len, stride=nsub), :] = src_ref[r * sub_len:(r + 1) * sub_len, lanes].astype(F32)


def _deinterleave_rows(src_ref, dst_ref, nslab, nsub, sub_len):
    for sl in range(nslab):
        lanes = slice(sl * V7X_LANES, (sl + 1) * V7X_LANES)
        for r in range(nsub):
            dst_ref[r * sub_len:(r + 1) * sub_len, lanes] = src_ref[sl, pl.ds(r, sub_len, stride=nsub), :].astype(
                dst_ref.dtype)


def _sublane_scan(a, b, row):
    for d in (1, 2, 4):
        keep = row >= d
        sa = jnp.where(keep, pltpu.roll(a, d, 0), 1.0)
        sb = jnp.where(keep, pltpu.roll(b, d, 0), 0.0)
        b = b + a * sb
        a = a * sa
    return a, b


def _bcast_last(x):
    return jnp.broadcast_to(x[V7X_SUBLANES - 1:V7X_SUBLANES, :], x.shape)


def _lru_kernel(x_ref, gate_ref, cw_ref, cb_ref, wax_ref, bax_ref, lam_ref, o_ref,
                xp_ref, yp_ref, tail_ref, a_ref, b_ref, carry_ref, *, heads, hd, tc, kconv):
    s = pl.program_id(1)
    sub = V7X_SUBLANES
    sub_len = tc // sub
    nslab = heads * hd // V7X_LANES
    spl = hd // V7X_LANES
    halo = (kconv - 1) * sub
    row = lax.broadcasted_iota(jnp.int32, (sub, V7X_LANES), 0)
    rowh = lax.broadcasted_iota(jnp.int32, (sub, hd), 0)

    @pl.when(s == 0)
    def _():
        tail_ref[...] = jnp.zeros_like(tail_ref)
        carry_ref[...] = jnp.zeros_like(carry_ref)

    for sl in range(nslab):
        lanes = slice(sl * V7X_LANES, (sl + 1) * V7X_LANES)
        for r in range(sub):
            xp_ref[sl, pl.ds(halo + r, sub_len, stride=sub), :] = x_ref[r * sub_len:(r + 1) * sub_len, lanes].astype(F32)
        for e in range(kconv - 1):
            cur = xp_ref[sl, halo + (sub_len - (kconv - 1) + e) * sub:halo + (sub_len - (kconv - 1) + e + 1) * sub, :]
            prev = tail_ref[sl, e * sub:(e + 1) * sub, :]
            xp_ref[sl, e * sub:(e + 1) * sub, :] = jnp.where(row == 0, pltpu.roll(prev, 1, 0), pltpu.roll(cur, 1, 0))
            tail_ref[sl, e * sub:(e + 1) * sub, :] = cur

    for h in range(heads):
        cols = slice(h * hd, (h + 1) * hd)
        parts = []
        for q in range(spl):
            sl = h * spl + q
            lanes = slice(sl * V7X_LANES, (sl + 1) * V7X_LANES)
            acc = cb_ref[:, lanes] + cw_ref[0:1, lanes] * xp_ref[sl, 0:tc, :]
            for k in range(1, kconv):
                acc = acc + cw_ref[k:k + 1, lanes] * xp_ref[sl, k * sub:k * sub + tc, :]
            parts.append(acc)
        xc = jnp.concatenate(parts, axis=-1) if spl > 1 else parts[0]
        pre = jnp.dot(xc.astype(BF16), wax_ref[h], preferred_element_type=F32) + bax_ref[h]
        r = _sigmoid(pre[:, :hd])
        gi = _sigmoid(pre[:, hd:])
        nl = -lam_ref[:, cols]
        sp = jnp.maximum(nl, 0.0) + jnp.log1p(jnp.exp(-jnp.abs(nl)))
        log_a = (-LRU_C) * r * sp
        a = jnp.exp(log_a)
        a_ref[...] = a
        b_ref[...] = jnp.sqrt(1.0 - a * a) * (gi * xc)

        def pass1(t, carry):
            hprev, pprev = carry
            rows = pl.ds(pl.multiple_of(t * sub, sub), sub)
            a = a_ref[rows, :]
            hh = a * hprev + b_ref[rows, :]
            pp = a * pprev
            b_ref[rows, :] = hh
            a_ref[rows, :] = pp
            return hh, pp

        zero = jnp.zeros((sub, hd), F32)
        hfin, pfin = lax.fori_loop(0, sub_len, pass1, (zero, zero + 1.0), unroll=4)
        cin = carry_ref[:, cols]
        ptot, hloc = _sublane_scan(pfin, hfin, rowh)
        hend = hloc + ptot * cin
        carry_ref[:, cols] = _bcast_last(hend)
        entry = jnp.where(rowh == 0, cin, pltpu.roll(hend, 1, 0))

        def pass2(t, _):
            rows = pl.ds(pl.multiple_of(t * sub, sub), sub)
            hh = b_ref[rows, :] + a_ref[rows, :] * entry
            for q in range(spl):
                yp_ref[h * spl + q, rows, :] = hh[:, q * V7X_LANES:(q + 1) * V7X_LANES]
            return 0

        lax.fori_loop(0, sub_len, pass2, 0, unroll=4)

    for sl in range(nslab):
        lanes = slice(sl * V7X_LANES, (sl + 1) * V7X_LANES)
        for r in range(sub):
            trows = slice(r * sub_len, (r + 1) * sub_len)
            hrows = yp_ref[sl, pl.ds(r, sub_len, stride=sub), :]
            o_ref[trows, lanes] = (_gelu(gate_ref[trows, lanes].astype(F32)) * hrows).astype(o_ref.dtype)


def lru_mixer(z, conv_w, conv_b, wa, ba, wx, bx, lam, bsz, seq):
    t = z.shape[0]
    heads, hd, _ = wa.shape
    w = heads * hd
    kconv = conv_w.shape[0]
    tc = _tile(seq, 256)
    ns = seq // tc
    nslab = w // V7X_LANES
    halo = (kconv - 1) * V7X_SUBLANES
    wax = jnp.concatenate([wa, wx], axis=-1).astype(BF16)
    bax = jnp.concatenate([ba.reshape(heads, 1, hd), bx.reshape(heads, 1, hd)], axis=-1)
    return pl.pallas_call(
        functools.partial(_lru_kernel, heads=heads, hd=hd, tc=tc, kconv=kconv),
        grid=(bsz, ns),
        in_specs=[pl.BlockSpec((tc, w), lambda b, s: (b * ns + s, 0)),
                  pl.BlockSpec((tc, w), lambda b, s: (b * ns + s, 1)),
                  pl.BlockSpec((kconv, w), lambda b, s: (0, 0)),
                  pl.BlockSpec((1, w), lambda b, s: (0, 0)),
                  pl.BlockSpec((heads, hd, 2 * hd), lambda b, s: (0, 0, 0)),
                  pl.BlockSpec((heads, 1, 2 * hd), lambda b, s: (0, 0, 0)),
                  pl.BlockSpec((1, w), lambda b, s: (0, 0))],
        out_specs=pl.BlockSpec((tc, w), lambda b, s: (b * ns + s, 0)),
        out_shape=jax.ShapeDtypeStruct((t, w), BF16),
        scratch_shapes=[pltpu.VMEM((nslab, halo + tc, V7X_LANES), F32),
                        pltpu.VMEM((nslab, tc, V7X_LANES), F32),
                        pltpu.VMEM((nslab, halo, V7X_LANES), F32),
                        pltpu.VMEM((tc, hd), F32),
                        pltpu.VMEM((tc, hd), F32),
                        pltpu.VMEM((V7X_SUBLANES, w), F32)],
        compiler_params=_cparams(("arbitrary", "arbitrary")),
        name="lru_mixer",
    )(z, z, conv_w, conv_b.reshape(1, w), wax, bax, lam.reshape(1, w))


S5_GROUPS_PER_BLOCK = 8


def _cmul(ar, ai, br, bi):
    return ar * br - ai * bi, ar * bi + ai * br


def _s5_kernel(u_ref, lre_ref, lim_ref, ldt_ref, bre_ref, bim_ref, cre_ref, cim_ref, d_ref, gw_ref, gb_ref,
               o_ref, wb_ref, a_ref, am_ref, p_ref, carry_ref, up_ref, re_ref, im_ref, y_ref, yp_ref,
               *, tc, nblk, cin, nst, lane_chunk):
    s = pl.program_id(1)
    gn = nblk * nst
    sub = V7X_SUBLANES
    sub_len = tc // sub
    nslab = nblk * cin // V7X_LANES

    @pl.when(s == 0)
    def _():
        lr = lre_ref[...]
        li = lim_ref[...]
        dt = jnp.exp(ldt_ref[...])
        mag = jnp.exp(lr * dt)
        ar = mag * jnp.cos(li * dt)
        ai = mag * jnp.sin(li * dt)
        den = lr * lr + li * li
        zr = ar - 1.0
        cr = (zr * lr + ai * li) / den
        ci = (ai * lr - zr * li) / den
        for j in range(nblk):
            cols = slice(j * nst, (j + 1) * nst)
            br = bre_ref[j]
            bi = bim_ref[j]
            wb_ref[j, :, 0:nst] = (cr[:, cols] * br - ci[:, cols] * bi).astype(BF16)
            wb_ref[j, :, nst:2 * nst] = (cr[:, cols] * bi + ci[:, cols] * br).astype(BF16)
        a_ref[0] = jnp.broadcast_to(ar, (sub, gn))
        a_ref[1] = jnp.broadcast_to(ai, (sub, gn))
        qr, qi = ar, ai
        for _ in range(sub_len - 1):
            qr, qi = _cmul(qr, qi, ar, ai)
        row = lax.broadcasted_iota(jnp.int32, (sub, gn), 0)
        pr, pi = qr, qi
        accr = jnp.zeros((sub, gn), F32)
        acci = jnp.zeros((sub, gn), F32)
        powers = {}
        for r in range(sub):
            powers[r + 1] = (pr, pi)
            accr = jnp.where(row == r, pr, accr)
            acci = jnp.where(row == r, pi, acci)
            pr, pi = _cmul(pr, pi, qr, qi)
        p_ref[0] = accr
        p_ref[1] = acci
        for idx, dd in enumerate((1, 2, 4)):
            wr, wi = powers[dd]
            am_ref[2 * idx] = jnp.where(row >= dd, wr, 0.0)
            am_ref[2 * idx + 1] = jnp.where(row >= dd, wi, 0.0)
        carry_ref[...] = jnp.zeros_like(carry_ref)

    _interleave_rows(u_ref, up_ref, nslab, sub, sub_len)
    for j in range(nblk):
        bu = jnp.dot(up_ref[j].astype(BF16), wb_ref[j], preferred_element_type=F32)
        re_ref[:, j * nst:(j + 1) * nst] = bu[:, :nst]
        im_ref[:, j * nst:(j + 1) * nst] = bu[:, nst:]

    row_c = lax.broadcasted_iota(jnp.int32, (sub, lane_chunk), 0)
    for c in range(gn // lane_chunk):
        lsl = slice(c * lane_chunk, (c + 1) * lane_chunk)

        def pass1(t, carry, lsl=lsl):
            pr, pi = carry
            rows = pl.ds(pl.multiple_of(t * sub, sub), sub)
            ar = a_ref[0, :, lsl]
            ai = a_ref[1, :, lsl]
            hr = re_ref[rows, lsl] + (ar * pr - ai * pi)
            hi = im_ref[rows, lsl] + (ar * pi + ai * pr)
            re_ref[rows, lsl] = hr
            im_ref[rows, lsl] = hi
            return hr, hi

        zero = jnp.zeros((sub, lane_chunk), F32)
        fr, fi = lax.fori_loop(0, sub_len, pass1, (zero, zero), unroll=2)
        for idx, dd in enumerate((1, 2, 4)):
            mr = am_ref[2 * idx, :, lsl]
            mi = am_ref[2 * idx + 1, :, lsl]
            sr = pltpu.roll(fr, dd, 0)
            si = pltpu.roll(fi, dd, 0)
            fr, fi = fr + (mr * sr - mi * si), fi + (mr * si + mi * sr)
        cr_ = carry_ref[0, :, lsl]
        ci_ = carry_ref[1, :, lsl]
        pr_ = p_ref[0, :, lsl]
        pi_ = p_ref[1, :, lsl]
        er = fr + (pr_ * cr_ - pi_ * ci_)
        ei = fi + (pr_ * ci_ + pi_ * cr_)
        carry_ref[0, :, lsl] = _bcast_last(er)
        carry_ref[1, :, lsl] = _bcast_last(ei)
        sr0 = jnp.where(row_c == 0, cr_, pltpu.roll(er, 1, 0))
        si0 = jnp.where(row_c == 0, ci_, pltpu.roll(ei, 1, 0))

        def pass2(t, carry, lsl=lsl):
            qr, qi = carry
            rows = pl.ds(pl.multiple_of(t * sub, sub), sub)
            ar = a_ref[0, :, lsl]
            ai = a_ref[1, :, lsl]
            qr, qi = ar * qr - ai * qi, ar * qi + ai * qr
            re_ref[rows, lsl] = re_ref[rows, lsl] + qr
            im_ref[rows, lsl] = im_ref[rows, lsl] + qi
            return qr, qi

        lax.fori_loop(0, sub_len, pass2, (sr0, si0), unroll=2)

    for j in range(nblk):
        cols = slice(j * nst, (j + 1) * nst)
        yj = (jnp.dot(re_ref[:, cols].astype(BF16), cre_ref[j], preferred_element_type=F32)
              - jnp.dot(im_ref[:, cols].astype(BF16), cim_ref[j], preferred_element_type=F32))
        ucols = slice(j * cin, (j + 1) * cin)
        y_ref[:, ucols] = yj + d_ref[:, ucols] * up_ref[j]
    g = _gelu(y_ref[...])
    gate = _sigmoid(jnp.dot(g.astype(BF16), gw_ref[...], preferred_element_type=F32) + gb_ref[...])
    fin = g * gate
    for sl in range(nslab):
        yp_ref[sl] = fin[:, sl * V7X_LANES:(sl + 1) * V7X_LANES]
    _deinterleave_rows(yp_ref, o_ref, nslab, sub, sub_len)


def s5_mixer(u, lam_re, lam_im, log_dt, b_re, b_im, c_re, c_im, d, glu_w, glu_b, bsz, seq):
    t, ws = u.shape
    groups, nstate, gch = b_re.shape
    gpb = S5_GROUPS_PER_BLOCK
    nblk = groups // gpb
    cin = gpb * gch
    assert cin == V7X_LANES
    nst = gpb * nstate
    gn = groups * nstate
    tc = _tile(seq, 256)
    ns = seq // tc
    lane_chunk = _tile(gn, 1024)
    nslab = ws // V7X_LANES
    eye = jnp.eye(gpb, dtype=F32)

    def bdiag_in(b):
        bb = b.reshape(nblk, gpb, nstate, gch).transpose(0, 1, 3, 2)
        return (bb[:, :, :, None, :] * eye[None, :, None, :, None]).reshape(nblk, cin, nst)

    def bdiag_out(c):
        cc = c.reshape(nblk, gpb, gch, nstate).transpose(0, 1, 3, 2)
        return (cc[:, :, :, None, :] * eye[None, :, None, :, None]).reshape(nblk, nst, cin)

    const2 = lambda b, s: (0, 0)
    const3 = lambda b, s: (0, 0, 0)
    return pl.pallas_call(
        functools.partial(_s5_kernel, tc=tc, nblk=nblk, cin=cin, nst=nst, lane_chunk=lane_chunk),
        grid=(bsz, ns),
        in_specs=[pl.BlockSpec((tc, ws), lambda b, s: (b * ns + s, 0)),
                  pl.BlockSpec((1, gn), const2), pl.BlockSpec((1, gn), const2), pl.BlockSpec((1, gn), const2),
                  pl.BlockSpec((nblk, cin, nst), const3), pl.BlockSpec((nblk, cin, nst), const3),
                  pl.BlockSpec((nblk, nst, cin), const3), pl.BlockSpec((nblk, nst, cin), const3),
                  pl.BlockSpec((1, ws), const2),
                  pl.BlockSpec((ws, ws), const2),
                  pl.BlockSpec((1, ws), const2)],
        out_specs=pl.BlockSpec((tc, ws), lambda b, s: (b * ns + s, 0)),
        out_shape=jax.ShapeDtypeStruct((t, ws), BF16),
        scratch_shapes=[pltpu.VMEM((nblk, cin, 2 * nst), BF16),
                        pltpu.VMEM((2, V7X_SUBLANES, gn), F32),
                        pltpu.VMEM((6, V7X_SUBLANES, gn), F32),
                        pltpu.VMEM((2, V7X_SUBLANES, gn), F32),
                        pltpu.VMEM((2, V7X_SUBLANES, gn), F32),
                        pltpu.VMEM((nslab, tc, V7X_LANES), F32),
                        pltpu.VMEM((tc, gn), F32),
                        pltpu.VMEM((tc, gn), F32),
                        pltpu.VMEM((tc, ws), F32),
                        pltpu.VMEM((nslab, tc, V7X_LANES), F32)],
        compiler_params=_cparams(("arbitrary", "arbitrary")),
        name="s5_mixer",
    )(u, lam_re.reshape(1, gn), lam_im.reshape(1, gn),
      jnp.broadcast_to(log_dt[:, None], (groups, nstate)).reshape(1, gn),
      bdiag_in(b_re), bdiag_in(b_im), bdiag_out(c_re).astype(BF16), bdiag_out(c_im).astype(BF16),
      d.reshape(1, ws), glu_w.astype(BF16), glu_b.reshape(1, ws))


def _sgu_kernel(u_ref, v_ref, g_ref, b_ref, w_ref, bs_ref, o_ref, *, heads, hd, chunk, nchunk):
    v = _gelu(v_ref[...].astype(F32))
    mu = jnp.mean(v, axis=-1, keepdims=True)
    vc = v - mu
    var = jnp.mean(vc * vc, axis=-1, keepdims=True)
    vn = (vc * lax.rsqrt(var + EPS) * g_ref[...] + b_ref[...]).astype(BF16)
    r = lax.broadcasted_iota(jnp.int32, (chunk, chunk), 0)
    c = lax.broadcasted_iota(jnp.int32, (chunk, chunk), 1)
    tril = r >= c
    for h in range(heads):
        wh = jnp.where(tril, w_ref[h], 0.0).astype(BF16)
        cols = slice(h * hd, (h + 1) * hd)
        for n in range(nchunk):
            rows = slice(n * chunk, (n + 1) * chunk)
            gsp = jnp.dot(wh, vn[rows, cols], preferred_element_type=F32) + bs_ref[h]
            o_ref[rows, cols] = (_gelu(u_ref[rows, cols].astype(F32)) * gsp).astype(o_ref.dtype)


def sgu_mixer(z, ln_g, ln_b, w_s, b_s):
    t = z.shape[0]
    heads, chunk, _ = w_s.shape
    w = ln_g.shape[0]
    hd = w // heads
    nchunk = 2 if (t // chunk) % 2 == 0 else 1
    tm = nchunk * chunk
    bs = jnp.broadcast_to(b_s[:, :, None], (heads, chunk, hd))
    return pl.pallas_call(
        functools.partial(_sgu_kernel, heads=heads, hd=hd, chunk=chunk, nchunk=nchunk),
        grid=(t // tm,),
        in_specs=[pl.BlockSpec((tm, w), lambda i: (i, 0)),
                  pl.BlockSpec((tm, w), lambda i: (i, 1)),
                  pl.BlockSpec((1, w), lambda i: (0, 0)),
                  pl.BlockSpec((1, w), lambda i: (0, 0)),
                  pl.BlockSpec((heads, chunk, chunk), lambda i: (0, 0, 0)),
                  pl.BlockSpec((heads, chunk, hd), lambda i: (0, 0, 0))],
        out_specs=pl.BlockSpec((tm, w), lambda i: (i, 0)),
        out_shape=jax.ShapeDtypeStruct((t, w), BF16),
        compiler_params=_cparams(("arbitrary",)),
        name="sgu_mixer",
    )(z, z, ln_g.reshape(1, w), ln_b.reshape(1, w), w_s, bs)


def _qk_norm_kernel(x_ref, g_ref, o_ref, *, nseg, seg):
    x = x_ref[...].astype(F32)
    for i in range(nseg):
        cols = slice(i * seg, (i + 1) * seg)
        xs = x[:, cols]
        ms = jnp.mean(xs * xs, axis=-1, keepdims=True)
        o_ref[:, cols] = (xs * lax.rsqrt(ms + EPS) * g_ref[:, cols]).astype(o_ref.dtype)


def qk_norm(z, col_block, width, gain_row, seg):
    t = z.shape[0]
    tm = _tile(t, 512)
    return pl.pallas_call(
        functools.partial(_qk_norm_kernel, nseg=width // seg, seg=seg),
        grid=(t // tm,),
        in_specs=[pl.BlockSpec((tm, width), lambda i: (i, col_block)),
                  pl.BlockSpec((1, width), lambda i: (0, 0))],
        out_specs=pl.BlockSpec((tm, width), lambda i: (i, 0)),
        out_shape=jax.ShapeDtypeStruct((t, width), BF16),
        compiler_params=_cparams(("arbitrary",)),
        name="qk_norm",
    )(z, gain_row)


ATTN_ROW_STRIP = 128


def _attn_kernel(q_ref, k_ref, v_ref, bias_ref, lq1_ref, lk1_ref, lq2_ref, lk2_ref, sg_ref, o_ref,
                 acc0_ref, acc1_ref, m0_ref, m1_ref, l0_ref, l1_ref, *, tq, dqk, lam_init):
    qi = pl.program_id(2)
    accs, ms, ls = (acc0_ref, acc1_ref), (m0_ref, m1_ref), (l0_ref, l1_ref)
    for mp in range(2):
        ms[mp][...] = jnp.full_like(ms[mp], NEG_INF)
        ls[mp][...] = jnp.zeros_like(ls[mp])
        accs[mp][...] = jnp.zeros_like(accs[mp])
    r = lax.broadcasted_iota(jnp.int32, (tq, tq), 0)
    c = lax.broadcasted_iota(jnp.int32, (tq, tq), 1)
    causal = r >= c
    krep = tq // V7X_LANES
    vrep = accs[0].shape[1] // V7X_LANES

    def block(kb, bias_idx, masked):
        k0 = pl.multiple_of(kb * tq, tq)
        v = v_ref[pl.ds(k0, tq), :]
        for r0 in range(0, tq, ATTN_ROW_STRIP):
            rs = slice(r0, r0 + ATTN_ROW_STRIP)
            use_bias = bias_idx is not None and (masked or r0 < REL_MAX_DIST)
            for mp in range(2):
                q = q_ref[rs, mp * dqk:(mp + 1) * dqk]
                k = k_ref[pl.ds(k0, tq), mp * dqk:(mp + 1) * dqk]
                sc = lax.dot_general(q, k, (((1,), (1,)), ((), ())), preferred_element_type=F32)
                if use_bias:
                    sc = sc + bias_ref[bias_idx, rs, :]
                if masked:
                    sc = jnp.where(causal[rs, :], sc, NEG_INF)
                m_old = ms[mp][rs, :]
                m_new = jnp.maximum(m_old, jnp.max(sc, axis=-1, keepdims=True))
                alpha = jnp.exp2(m_old - m_new)
                p = jnp.exp2(sc - jnp.tile(m_new, (1, krep)))
                ls[mp][rs, :] = alpha * ls[mp][rs, :] + jnp.sum(p, axis=-1, keepdims=True)
                accs[mp][rs, :] = (jnp.tile(alpha, (1, vrep)) * accs[mp][rs, :]
                                   + jnp.dot(p.astype(BF16), v, preferred_element_type=F32))
                ms[mp][rs, :] = m_new

    nfar = jnp.maximum(qi - 1, 0)

    def far_pair(j, _):
        block(2 * j, None, False)
        block(2 * j + 1, None, False)
        return 0

    lax.fori_loop(0, nfar // 2, far_pair, 0)

    @pl.when(nfar % 2 == 1)
    def _():
        block(nfar - 1, None, False)

    @pl.when(qi > 0)
    def _():
        block(qi - 1, 1, False)
        block(qi, 0, True)

    @pl.when(qi == 0)
    def _():
        block(qi, 0, True)

    lam = (jnp.exp(jnp.sum(lq1_ref[...] * lk1_ref[...], axis=-1, keepdims=True))
           - jnp.exp(jnp.sum(lq2_ref[...] * lk2_ref[...], axis=-1, keepdims=True)) + lam_init)
    o = (accs[0][...] * jnp.tile(1.0 / ls[0][...], (1, vrep))
         - lam * (accs[1][...] * jnp.tile(1.0 / ls[1][...], (1, vrep))))
    ms_o = jnp.mean(o * o, axis=-1, keepdims=True)
    o_ref[...] = (o * lax.rsqrt(ms_o + EPS) * sg_ref[...] * (1.0 - lam_init)).astype(o_ref.dtype)


def _t5_bucket(n, buckets):
    max_exact = buckets // 2
    nf = jnp.maximum(n, 1).astype(F32)
    large = max_exact + (jnp.log(nf / max_exact) / math.log(REL_MAX_DIST / max_exact)
                         * (buckets - max_exact)).astype(jnp.int32)
    large = jnp.minimum(large, buckets - 1)
    return jnp.where(n < max_exact, n, large)


def _toeplitz(w, rows, cols):
    nh, lw = w.shape
    flat = jnp.tile(w, (1, rows))[:, :rows * (lw - 1)]
    return flat.reshape(nh, rows, lw - 1)[:, :, :cols]


def diff_attention(qn, kn, zqkv, v_col_block, rel_bias, lq1, lk1, lq2, lk2, sub_g, bsz, seq, layer):
    t = qn.shape[0]
    buckets, heads = rel_bias.shape
    dv = sub_g.shape[0]
    dqk = dv // 2
    tq = _tile(seq, 512)
    assert tq >= REL_MAX_DIST, "far key blocks must all fall in the last relative-position bucket"
    nq = seq // tq
    lam_init = 0.8 - 0.6 * math.exp(-0.3 * layer)
    table = jnp.transpose(rel_bias.astype(F32))
    fvals = table[:, _t5_bucket(jnp.arange(2 * tq + 1, dtype=jnp.int32), buckets)]
    fvals = (fvals - fvals[:, 2 * tq:]) * LOG2E
    jj = np.arange(2 * tq)
    n_diag = np.where(jj <= tq, 0, 2 * tq - jj)
    n_near = np.where(jj < tq, tq - jj, 3 * tq - jj)
    bias = jnp.stack([_toeplitz(fvals[:, n_diag], tq, tq), _toeplitz(fvals[:, n_near], tq, tq)], axis=1)
    row = lambda b, h, i: (0, 0)
    return pl.pallas_call(
        functools.partial(_attn_kernel, tq=tq, dqk=dqk, lam_init=lam_init),
        grid=(bsz, heads, nq),
        in_specs=[pl.BlockSpec((tq, dv), lambda b, h, i: (b * nq + i, h)),
                  pl.BlockSpec((seq, dv), lambda b, h, i: (b, h)),
                  pl.BlockSpec((seq, dv), lambda b, h, i: (b, v_col_block + h)),
                  pl.BlockSpec((None, 2, tq, tq), lambda b, h, i: (h, 0, 0, 0)),
                  pl.BlockSpec((1, dqk), row), pl.BlockSpec((1, dqk), row),
                  pl.BlockSpec((1, dqk), row), pl.BlockSpec((1, dqk), row),
                  pl.BlockSpec((1, dv), row)],
        out_specs=pl.BlockSpec((tq, dv), lambda b, h, i: (b * nq + i, h)),
        out_shape=jax.ShapeDtypeStruct((t, heads * dv), BF16),
        scratch_shapes=[pltpu.VMEM((tq, dv), F32), pltpu.VMEM((tq, dv), F32),
                        pltpu.VMEM((tq, V7X_LANES), F32), pltpu.VMEM((tq, V7X_LANES), F32),
                        pltpu.VMEM((tq, V7X_LANES), F32), pltpu.VMEM((tq, V7X_LANES), F32)],
        compiler_params=_cparams(("arbitrary", "arbitrary", "arbitrary")),
        name="diff_attention",
    )(qn, kn, zqkv, bias, lq1.reshape(1, dqk), lk1.reshape(1, dqk), lq2.reshape(1, dqk), lk2.reshape(1, dqk),
      sub_g.reshape(1, dv))


ROUTER_LANES = 128
HI16 = 0xFFFF0000


def _norm_mod(x, g_ref, sc_ref, sh_ref):
    ms = jnp.mean(x * x, axis=-1, keepdims=True)
    return (x * lax.rsqrt(ms + EPS) * g_ref[...]) * (1.0 + sc_ref[...]) + sh_ref[...]


def _router_kernel(x_ref, g_ref, sc_ref, sh_ref, wr_ref, wlo_ref, br_ref, route_ref, hpk_ref, *, ngroups, per_group):
    h = _norm_mod(x_ref[...], g_ref, sc_ref, sh_ref)
    half = h.shape[1] // 2
    bits = pltpu.bitcast(h.astype(BF16).astype(F32), jnp.uint32)
    hpk_ref[...] = (bits[:, half:] & jnp.uint32(HI16)) | (bits[:, :half] >> 16)
    hi = h.astype(BF16)
    lo = (h - hi.astype(F32)).astype(BF16)
    both = jnp.dot(hi, wr_ref[...], preferred_element_type=F32)
    nl = ROUTER_LANES
    logits = both[:, :nl] + both[:, nl:] + jnp.dot(lo, wlo_ref[...], preferred_element_type=F32) + br_ref[...]
    lane = lax.broadcasted_iota(jnp.int32, logits.shape, 1).astype(F32)
    big = float(nl)
    glog = jnp.where(lane < ngroups, logits, NEG_INF)
    gmax = jnp.max(glog, axis=-1, keepdims=True)
    gsum = jnp.sum(jnp.exp(glog - gmax), axis=-1, keepdims=True)
    gp = 1.0 / gsum
    gidx = jnp.min(jnp.where(glog == gmax, lane, big), axis=-1, keepdims=True)
    lo_lane = ngroups + gidx * per_group
    emask = (lane >= lo_lane) & (lane < lo_lane + per_group)
    elog = jnp.where(emask, logits, NEG_INF)
    emax = jnp.max(elog, axis=-1, keepdims=True)
    eexp = jnp.where(emask, jnp.exp(elog - emax), -1.0)
    i0 = jnp.min(jnp.where(eexp == 1.0, lane, big), axis=-1, keepdims=True)
    rest = jnp.where(lane == i0, -1.0, eexp)
    p1 = jnp.max(rest, axis=-1, keepdims=True)
    i1 = jnp.min(jnp.where(rest == p1, lane, big), axis=-1, keepdims=True)
    denom = 1.0 + p1
    w0 = gp * (1.0 / denom)
    w1 = gp * (p1 / denom)
    e0 = i0 - ngroups
    e1 = i1 - ngroups
    route_ref[...] = jnp.where(lane == 0, e0, jnp.where(lane == 1, e1, jnp.where(lane == 2, w0,
                               jnp.where(lane == 3, w1, 0.0))))


def moe_router(x, g, sc, sh, wg, bg, we, be, seq):
    t, d = x.shape
    bsz = sc.shape[0]
    ngroups = wg.shape[-1]
    per_group = we.shape[-1]
    nexp = ngroups * per_group
    nl = ROUTER_LANES
    assert ngroups + nexp <= nl
    wr = jnp.concatenate([wg, jnp.transpose(we, (1, 0, 2)).reshape(d, nexp)], axis=-1)
    wr = jnp.zeros((d, nl), F32).at[:, :ngroups + nexp].set(wr)
    w_hi = wr.astype(BF16)
    w_lo = (wr - w_hi.astype(F32)).astype(BF16)
    br = jnp.zeros((1, nl), F32).at[0, :ngroups + nexp].set(jnp.concatenate([bg, be.reshape(nexp)]))
    tm = _tile(seq, 256)
    per = seq // tm
    return pl.pallas_call(
        functools.partial(_router_kernel, ngroups=ngroups, per_group=per_group),
        grid=(t // tm,),
        in_specs=[pl.BlockSpec((tm, d), lambda i: (i, 0)),
                  pl.BlockSpec((1, d), lambda i: (0, 0)),
                  pl.BlockSpec((None, 1, d), lambda i: (i // per, 0, 0)),
                  pl.BlockSpec((None, 1, d), lambda i: (i // per, 0, 0)),
                  pl.BlockSpec((d, 2 * nl), lambda i: (0, 0)),
                  pl.BlockSpec((d, nl), lambda i: (0, 0)),
                  pl.BlockSpec((1, nl), lambda i: (0, 0))],
        out_specs=[pl.BlockSpec((tm, nl), lambda i: (i, 0)), pl.BlockSpec((tm, d // 2), lambda i: (i, 0))],
        out_shape=[jax.ShapeDtypeStruct((t, nl), F32), jax.ShapeDtypeStruct((t, d // 2), jnp.uint32)],
        compiler_params=_cparams(("arbitrary",)),
        name="moe_router",
    )(x, g.reshape(1, d), sc.reshape(bsz, 1, d), sh.reshape(bsz, 1, d),
      jnp.concatenate([w_hi, w_lo], axis=-1), w_hi, br)


def _dispatch_kernel(pos_ref, zf_ref, hpk_ref, xs_hbm, buf, zbuf, sem, zsem, *, tm, tme, ntok, ntile, topk):
    i = pl.program_id(0)
    nsteps = pl.num_programs(0)
    slot = lax.rem(i, 2)

    @pl.when(i == 0)
    def _():
        zbuf[...] = jnp.zeros_like(zbuf)

        def zcopy(tl):
            return pltpu.make_async_copy(zbuf, xs_hbm.at[pl.ds(tl * tme, tme)], zsem)

        def zstart(tl, _):
            @pl.when(zf_ref[tl] > 0)
            def _():
                zcopy(tl).start()
            return 0

        def zwait(tl, _):
            @pl.when(zf_ref[tl] > 0)
            def _():
                zcopy(tl).wait()
            return 0

        lax.fori_loop(0, ntile, zstart, 0)
        lax.fori_loop(0, ntile, zwait, 0)

    buf[slot] = hpk_ref[...]
    base = i * tm

    def issue(r, _):
        for kk in range(topk):
            pltpu.make_async_copy(buf.at[slot, pl.ds(r, 1)],
                                  xs_hbm.at[pl.ds(pos_ref[kk * ntok + base + r], 1)], sem.at[slot]).start()
        return 0

    lax.fori_loop(0, tm, issue, 0, unroll=4)

    def drain(s):
        for _ in range(topk):
            pltpu.make_async_copy(buf.at[s], xs_hbm.at[pl.ds(0, tm)], sem.at[s]).wait()

    @pl.when(i > 0)
    def _():
        drain(1 - slot)

    @pl.when(i == nsteps - 1)
    def _():
        drain(slot)


W_CHUNKS = 4
W_STAGES = 4


def _expert_kernel(texp_ref, tvalid_ref, xidx_ref, slot_ref, first_ref, nxt_ref, clo_ref, chi_ref,
                   xs_ref, wg_hbm, wu_hbm, wd_hbm, y_ref, wg_buf, wu_buf, wd_buf, st_a, st_d, sem,
                   *, half, layer, d, f):
    i = pl.program_id(0)
    nch = 3 * W_CHUNKS
    ra = d // W_CHUNKS
    rd = f // W_CHUNKS

    def chunk_dma(e, which, part, b):
        if which == 2:
            return pltpu.make_async_copy(wd_hbm.at[layer, e, pl.ds(pl.multiple_of(part * rd, rd), rd), :],
                                         st_d.at[b], sem.at[b])
        src = wg_hbm if which == 0 else wu_hbm
        return pltpu.make_async_copy(src.at[layer, e, pl.ds(pl.multiple_of(part * ra, ra), ra), :],
                                     st_a.at[b], sem.at[b])

    def for_chunk(c, fn):
        for which in range(3):
            @pl.when((c >= which * W_CHUNKS) & (c < (which + 1) * W_CHUNKS))
            def _(which=which):
                fn(which, c - which * W_CHUNKS)

    def start(e, c):
        b = lax.rem(c, W_STAGES)
        for_chunk(c, lambda which, part: chunk_dma(e, which, part, b).start())

    def finish(e, c, s):
        b = lax.rem(c, W_STAGES)

        def fn(which, part):
            chunk_dma(e, which, part, b).wait()
            if which == 2:
                wd_buf[s, pl.ds(pl.multiple_of(part * rd, rd), rd), :] = st_d[b].astype(BF16)
            else:
                dst = wg_buf if which == 0 else wu_buf
                dst[s, pl.ds(pl.multiple_of(part * ra, ra), ra), :] = st_a[b].astype(BF16)

        for_chunk(c, fn)

    slot = slot_ref[i]
    nxt = nxt_ref[i]

    @pl.when(i == 0)
    def _():
        def load_first(c, _):
            start(texp_ref[0], c)
            finish(texp_ref[0], c, 0)
            return 0
        lax.fori_loop(0, nch, load_first, 0)

    @pl.when((nxt >= 0) & (first_ref[i] > 0))
    def _():
        for c in range(W_STAGES):
            start(nxt, jnp.int32(c))

    @pl.when(tvalid_ref[i] > 0)
    def _():
        xp = xs_ref[...]
        x_lo = pltpu.bitcast(xp << 16, F32).astype(BF16)
        x_hi = pltpu.bitcast(xp & jnp.uint32(HI16), F32).astype(BF16)
        hg = (jnp.dot(x_lo, wg_buf[slot, 0:half, :], preferred_element_type=F32)
              + jnp.dot(x_hi, wg_buf[slot, half:, :], preferred_element_type=F32))
        hu = (jnp.dot(x_lo, wu_buf[slot, 0:half, :], preferred_element_type=F32)
              + jnp.dot(x_hi, wu_buf[slot, half:, :], preferred_element_type=F32))
        act = (hg * _sigmoid(hg)) * hu
        yv = jnp.dot(act.astype(BF16), wd_buf[slot], preferred_element_type=F32)
        bits = pltpu.bitcast(yv.astype(BF16).astype(F32), jnp.uint32)
        y_ref[...] = (bits[:, half:] & jnp.uint32(HI16)) | (bits[:, :half] >> 16)

    @pl.when(tvalid_ref[i] == 0)
    def _():
        y_ref[...] = jnp.zeros_like(y_ref)

    @pl.when(nxt >= 0)
    def _():
        def advance(c, _):
            finish(nxt, c, 1 - slot)

            @pl.when(c + W_STAGES < nch)
            def _():
                start(nxt, c + W_STAGES)
            return 0
        lax.fori_loop(clo_ref[i], chi_ref[i], advance, 0)


def _combine_kernel(pos_ref, x_ref, route_ref, g_ref, y_hbm, *rest, tm, ntok, topk, half, with_norm):
    if with_norm:
        ng_ref, nsc_ref, nsh_ref, o_ref, hn_ref, ybuf, sem = rest
    else:
        o_ref, ybuf, sem = rest
    i = pl.program_id(0)
    nsteps = pl.num_programs(0)
    slot = lax.rem(i, 2)

    def gather(step, s):
        base = step * tm

        def issue(r, _):
            for kk in range(topk):
                pltpu.make_async_copy(y_hbm.at[pl.ds(pos_ref[kk * ntok + base + r], 1)],
                                      ybuf.at[s, kk, pl.ds(r, 1)], sem.at[s]).start()
            return 0

        lax.fori_loop(0, tm, issue, 0, unroll=4)

    @pl.when(i == 0)
    def _():
        gather(0, 0)

    for kk in range(topk):
        pltpu.make_async_copy(y_hbm.at[pl.ds(0, tm)], ybuf.at[slot, kk], sem.at[slot]).wait()

    nbase = jnp.minimum(i + 1, nsteps - 1) * tm

    def mix_rows(cidx, _):
        rows = pl.ds(pl.multiple_of(cidx * V7X_SUBLANES, V7X_SUBLANES), V7X_SUBLANES)
        for rr in range(V7X_SUBLANES):
            r = cidx * V7X_SUBLANES + rr
            for kk in range(topk):
                pltpu.make_async_copy(y_hbm.at[pl.ds(pos_ref[kk * ntok + nbase + r], 1)],
                                      ybuf.at[1 - slot, kk, pl.ds(r, 1)], sem.at[1 - slot]).start()
        w = route_ref[rows, :]
        ylo = yhi = None
        for kk in range(topk):
            yp = ybuf[slot, kk, rows, :]
            wk = w[:, topk + kk:topk + kk + 1]
            lo = wk * pltpu.bitcast(yp << 16, F32)
            hi = wk * pltpu.bitcast(yp & jnp.uint32(HI16), F32)
            ylo = lo if ylo is None else ylo + lo
            yhi = hi if yhi is None else yhi + hi
        olo = x_ref[rows, 0:half] + g_ref[:, 0:half] * ylo
        ohi = x_ref[rows, half:] + g_ref[:, half:] * yhi
        o_ref[rows, 0:half] = olo
        o_ref[rows, half:] = ohi
        if with_norm:
            ms = (jnp.sum(olo * olo, axis=-1, keepdims=True) + jnp.sum(ohi * ohi, axis=-1, keepdims=True)) / (2 * half)
            rs = lax.rsqrt(ms + EPS)
            for cols, ov in ((slice(0, half), olo), (slice(half, 2 * half), ohi)):
                hn = (ov * rs * ng_ref[:, cols]) * (1.0 + nsc_ref[:, cols]) + nsh_ref[:, cols]
                hn_ref[rows, cols] = hn.astype(hn_ref.dtype)
        return 0

    lax.fori_loop(0, tm // V7X_SUBLANES, mix_rows, 0, unroll=4)

    @pl.when(i == nsteps - 1)
    def _():
        for kk in range(topk):
            pltpu.make_async_copy(y_hbm.at[pl.ds(0, tm)], ybuf.at[1 - slot, kk], sem.at[1 - slot]).wait()


def moe_layer(x, route, hpk, gate, w_gate, w_up, w_down, layer, seq, next_norm=None):
    t, d = x.shape
    bsz = gate.shape[0]
    _, nexp, _, f = w_gate.shape
    topk = MOE_TOPK
    half = d // 2
    npair = t * topk
    tme = _tile(npair // nexp, 256) if npair // nexp >= 8 else 8
    ntile = npair // tme + nexp
    tm = _tile(seq, 256)
    per = seq // tm

    flat_e = jnp.transpose(route[:, 0:topk]).astype(jnp.int32).reshape(npair)
    onehot = (flat_e[:, None] == jnp.arange(nexp, dtype=jnp.int32)[None, :]).astype(jnp.int32)
    csum = jnp.cumsum(onehot, axis=0)
    counts = csum[-1]
    tiles_per = (counts + tme - 1) // tme
    tile_end = jnp.cumsum(tiles_per)
    tile_start = tile_end - tiles_per
    pos = jnp.sum(onehot * (tile_start[None, :] * tme + csum - 1), axis=1).astype(jnp.int32)
    n_used = tile_end[-1]
    tile_id = jnp.arange(ntile, dtype=jnp.int32)
    active = tile_id < n_used
    owner = (tile_id[:, None] >= tile_start[None, :]) & (tile_id[:, None] < tile_end[None, :])
    eids = jnp.arange(nexp, dtype=jnp.int32)
    used = tiles_per > 0
    ordinal = jnp.cumsum(used.astype(jnp.int32)) - 1
    nxt_of = jnp.min(jnp.where((eids[None, :] > eids[:, None]) & used[None, :], eids[None, :], nexp), axis=1)
    nxt_of = jnp.where(nxt_of >= nexp, -1, nxt_of)

    def of_tile(per_expert):
        return jnp.sum(jnp.where(owner, per_expert[None, :], 0), axis=1)

    last_used = jnp.max(jnp.where(used, eids, 0))
    texp = jnp.where(active, of_tile(eids), last_used).astype(jnp.int32)
    kk = tile_id - of_tile(tile_start)
    tvalid = jnp.where(active, jnp.clip(of_tile(counts) - kk * tme, 0, tme), 0).astype(jnp.int32)
    xidx = jnp.minimum(tile_id, jnp.maximum(n_used - 1, 0)).astype(jnp.int32)
    zflag = (tvalid < tme).astype(jnp.int32)
    nchunk = 3 * W_CHUNKS
    nn = jnp.maximum(of_tile(tiles_per), 1)
    wslot = jnp.where(active, of_tile(ordinal) % 2, 0).astype(jnp.int32)
    wfirst = (active & (kk == 0)).astype(jnp.int32)
    wnxt = jnp.where(active, of_tile(nxt_of), -1).astype(jnp.int32)
    clo = jnp.where(active, kk * nchunk // nn, 0).astype(jnp.int32)
    chi = jnp.where(active, (kk + 1) * nchunk // nn, 0).astype(jnp.int32)

    xs = pl.pallas_call(
        functools.partial(_dispatch_kernel, tm=tm, tme=tme, ntok=t, ntile=ntile, topk=topk),
        grid_spec=pltpu.PrefetchScalarGridSpec(
            num_scalar_prefetch=2,
            grid=(t // tm,),
            in_specs=[pl.BlockSpec((tm, half), lambda i, *_: (i, 0))],
            out_specs=pl.BlockSpec(memory_space=pl.ANY),
            scratch_shapes=[pltpu.VMEM((2, tm, half), jnp.uint32), pltpu.VMEM((tme, half), jnp.uint32),
                            pltpu.SemaphoreType.DMA((2,)), pltpu.SemaphoreType.DMA(())]),
        out_shape=jax.ShapeDtypeStruct((ntile * tme, half), jnp.uint32),
        compiler_params=_cparams(("arbitrary",)),
        name="moe_dispatch",
    )(pos, zflag, hpk)

    y = pl.pallas_call(
        functools.partial(_expert_kernel, half=half, layer=layer, d=d, f=f),
        grid_spec=pltpu.PrefetchScalarGridSpec(
            num_scalar_prefetch=8,
            grid=(ntile,),
            in_specs=[pl.BlockSpec((tme, half), lambda i, te, tv, xi, *_: (xi[i], 0)),
                      pl.BlockSpec(memory_space=pl.ANY),
                      pl.BlockSpec(memory_space=pl.ANY),
                      pl.BlockSpec(memory_space=pl.ANY)],
            out_specs=pl.BlockSpec((tme, half), lambda i, *_: (i, 0)),
            scratch_shapes=[pltpu.VMEM((2, d, f), BF16), pltpu.VMEM((2, d, f), BF16), pltpu.VMEM((2, f, d), BF16),
                            pltpu.VMEM((W_STAGES, d // W_CHUNKS, f), F32),
                            pltpu.VMEM((W_STAGES, f // W_CHUNKS, d), F32),
                            pltpu.SemaphoreType.DMA((W_STAGES,))]),
        out_shape=jax.ShapeDtypeStruct((ntile * tme, half), jnp.uint32),
        compiler_params=_cparams(("arbitrary",)),
        name="moe_experts",
    )(texp, tvalid, xidx, wslot, wfirst, wnxt, clo, chi, xs, w_gate, w_up, w_down)

    row_spec = pl.BlockSpec((tm, d), lambda i, *_: (i, 0))
    batch_spec = pl.BlockSpec((None, 1, d), lambda i, *_: (i // per, 0, 0))
    in_specs = [row_spec, pl.BlockSpec((tm, ROUTER_LANES), lambda i, *_: (i, 0)), batch_spec,
                pl.BlockSpec(memory_space=pl.ANY)]
    args = [pos, x, route, gate.reshape(bsz, 1, d), y]
    out_specs, out_shape = row_spec, jax.ShapeDtypeStruct((t, d), F32)
    if next_norm is not None:
        ng, nsc, nsh = next_norm
        in_specs += [pl.BlockSpec((1, d), lambda i, *_: (0, 0)), batch_spec, batch_spec]
        args += [ng.reshape(1, d), nsc.reshape(bsz, 1, d), nsh.reshape(bsz, 1, d)]
        out_specs = [row_spec, row_spec]
        out_shape = [out_shape, jax.ShapeDtypeStruct((t, d), BF16)]
    return pl.pallas_call(
        functools.partial(_combine_kernel, tm=tm, ntok=t, topk=topk, half=half, with_norm=next_norm is not None),
        grid_spec=pltpu.PrefetchScalarGridSpec(
            num_scalar_prefetch=1,
            grid=(t // tm,),
            in_specs=in_specs,
            out_specs=out_specs,
            scratch_shapes=[pltpu.VMEM((2, topk, tm, half), jnp.uint32), pltpu.SemaphoreType.DMA((2,))]),
        out_shape=out_shape,
        compiler_params=_cparams(("arbitrary",)),
        name="moe_combine",
    )(*args)


def kernel(x, c, norm1_g, norm2_g, ada_w, ada_b, ab_w_in, ab_w_out, lru_conv_w, lru_conv_b, lru_wa, lru_ba, lru_wx, lru_bx, lru_lambda, s5_lambda_re, s5_lambda_im, s5_log_dt, s5_b_re, s5_b_im, s5_c_re, s5_c_im, s5_d, s5_glu_w, s5_glu_b, cd_w_in, cd_w_out, sg_ln_g, sg_ln_b, sg_w, sg_b, da_q_norm, da_k_norm, da_lq1, da_lk1, da_lq2, da_lk2, da_sub_g, rel_bias, moe_wg, moe_bg, moe_we, moe_be, moe_w_gate, moe_w_up, moe_w_down):
    bsz, seq, d = x.shape
    depth = norm1_g.shape[0]
    t = bsz * seq
    xt = x.reshape(t, d)
    mod = ada_modulation(c, ada_w, ada_b)

    hmix = None
    for layer in range(depth):
        sh1, sc1, g1, sh2, sc2, g2 = [mod[layer, :, i * d:(i + 1) * d] for i in range(6)]
        if hmix is None:
            hmix = norm_modulate(xt, norm1_g[layer], sc1, sh1, seq)
        j = layer // 2
        if layer % 2 == 0:
            lw = lru_conv_w.shape[-1]
            w_in = ab_w_in[j].astype(BF16)
            w_out = ab_w_out[j].astype(BF16)
            z_lru = matmul([hmix], w_in, BF16, col_off=0, ncols=2 * lw)
            z_s5 = matmul([hmix], w_in, BF16, col_off=2 * lw)
            y_a = lru_mixer(z_lru, lru_conv_w[j], lru_conv_b[j], lru_wa[j], lru_ba[j], lru_wx[j], lru_bx[j],
                            lru_lambda[j], bsz, seq)
            y_b = s5_mixer(z_s5, s5_lambda_re[j], s5_lambda_im[j], s5_log_dt[j], s5_b_re[j], s5_b_im[j],
                           s5_c_re[j], s5_c_im[j], s5_d[j], s5_glu_w[j], s5_glu_b[j], bsz, seq)
            xt = matmul([y_a, y_b], w_out, F32, res=xt, gate=g1, seq=seq)
        else:
            sgw = sg_ln_g.shape[-1]
            dqk = da_q_norm.shape[-1]
            dv = da_sub_g.shape[-1]
            heads = rel_bias.shape[1]
            daw = heads * dv
            w_in = cd_w_in[j].astype(BF16)
            w_out = cd_w_out[j].astype(BF16)
            z_sg = matmul([hmix], w_in, BF16, col_off=0, ncols=2 * sgw)
            z_qkv = matmul([hmix], w_in, BF16, col_off=2 * sgw)
            y_c = sgu_mixer(z_sg, sg_ln_g[j], sg_ln_b[j], sg_w[j], sg_b[j])
            q_gain = jnp.tile(da_q_norm[j] * (dqk ** -0.5 * LOG2E), daw // dqk).reshape(1, daw)
            k_gain = jnp.tile(da_k_norm[j], daw // dqk).reshape(1, daw)
            qn = qk_norm(z_qkv, 0, daw, q_gain, dqk)
            kn = qk_norm(z_qkv, 1, daw, k_gain, dqk)
            y_d = diff_attention(qn, kn, z_qkv, 2 * daw // dv, rel_bias, da_lq1[j], da_lk1[j], da_lq2[j], da_lk2[j],
                                 da_sub_g[j], bsz, seq, layer)
            xt = matmul([y_c, y_d], w_out, F32, res=xt, gate=g1, seq=seq)
        route, hpk = moe_router(xt, norm2_g[layer], sc2, sh2, moe_wg[layer], moe_bg[layer], moe_we[layer],
                                moe_be[layer], seq)
        if layer + 1 < depth:
            nxt = (norm1_g[layer + 1], mod[layer + 1, :, d:2 * d], mod[layer + 1, :, 0:d])
            xt, hmix = moe_layer(xt, route, hpk, g2, moe_w_gate, moe_w_up, moe_w_down, layer, seq, next_norm=nxt)
        else:
            xt = moe_layer(xt, route, hpk, g2, moe_w_gate, moe_w_up, moe_w_down, layer, seq)
    return xt.reshape(bsz, seq, d)
```
